```python
import jax, jax.numpy as jnp
from jax import lax
import numpy as np

D_MODEL = 1024
BATCH = 4
SEQ = 4096
DEPTH = 2

MEM_LEN = 256
D_MIX = D_MODEL
HEAD_DIM = 64
SG_WIDTH = D_MIX // 2
SG_GROUPS = SG_WIDTH // HEAD_DIM
SG_CHUNK = 128
NSA_WIDTH = D_MIX - SG_WIDTH
NSA_HEADS = NSA_WIDTH // HEAD_DIM
NSA_KV_HEADS = 2
NSA_GROUP = NSA_HEADS // NSA_KV_HEADS
KV_WIDTH = NSA_KV_HEADS * HEAD_DIM
N_BRANCH = 3
CMP_BLOCK = 32
CMP_STRIDE = 16
CMP_HIDDEN = 256
SLC_BLOCK = 64
SLC_TOPK = 16
WINDOW = 512
Q_BLOCK = 128
FORCE_SCORE = 1e4
IN_COLS = 2 * SG_WIDTH + NSA_WIDTH + 6 * KV_WIDTH + NSA_HEADS * N_BRANCH
MEM_HEADS = 4
MEM_HEAD_DIM = 128
D_FF = 4 * D_MODEL
EPS = 1e-6

kernel_name = "hybrid_sgmlp_nsa_memory_block"


def rms_norm(x, g):
    xf = x.astype(jnp.float32)
    y = xf * lax.rsqrt(jnp.mean(xf * xf, axis=-1, keepdims=True) + EPS)
    return (y * g.astype(jnp.float32)).astype(x.dtype)


def layer_norm(x, g, b):
    xf = x.astype(jnp.float32)
    mu = jnp.mean(xf, axis=-1, keepdims=True)
    xc = xf - mu
    y = xc * lax.rsqrt(jnp.mean(xc * xc, axis=-1, keepdims=True) + EPS)
    return (y * g.astype(jnp.float32) + b.astype(jnp.float32)).astype(x.dtype)


def masked_softmax(s, mask):
    s = jnp.where(mask, s.astype(jnp.float32), -jnp.inf)
    m = jnp.max(s, axis=-1, keepdims=True)
    m = jnp.where(jnp.isfinite(m), m, 0.0)
    e = jnp.exp(s - m)
    d = jnp.sum(e, axis=-1, keepdims=True)
    return e / jnp.where(d > 0, d, 1.0)


def spatial_gating_unit(u, v, ln_g, ln_b, w_s, b_s):
    B, T, _ = u.shape
    v = layer_norm(v, ln_g, ln_b)
    vb = v.reshape(B, T // SG_CHUNK, SG_CHUNK, SG_GROUPS, HEAD_DIM)
    causal = jnp.tril(jnp.ones((SG_CHUNK, SG_CHUNK), dtype=bool))
    w = jnp.where(causal[None], w_s, 0)
    s = jnp.einsum('gts,bcsgd->bctgd', w, vb) + b_s.T[None, None, :, :, None]
    return u * s.reshape(B, T, SG_WIDTH)


def compress_blocks(k, pos, w1, b1, w2, b2):
    B, T, H, d = k.shape
    kc = k.reshape(B, T // CMP_STRIDE, CMP_STRIDE, H, d)
    blocks = jnp.concatenate([kc[:, :-1], kc[:, 1:]], axis=2) + pos[None, None, :, None, :]
    flat = blocks.transpose(0, 1, 3, 2, 4).reshape(B, -1, H, CMP_BLOCK * d)
    h = jax.nn.gelu(flat @ w1 + b1)
    return h @ w2 + b2


def nsa_attention(q, kv, gate_logits, q_norm_g, k_norm_g, cmp_pos, cmp_w1, cmp_b1, cmp_w2, cmp_b2):
    B, T = q.shape[:2]
    H, G, d = NSA_KV_HEADS, NSA_GROUP, HEAD_DIM
    n_cmp = T // CMP_STRIDE - 1
    n_slc = T // SLC_BLOCK
    n_sel = min(SLC_TOPK, n_slc)
    n_qb = T // Q_BLOCK
    scale = HEAD_DIM ** -0.5
    t_pos = jnp.arange(T)

    q = rms_norm(q.reshape(B, T, H, G, d), q_norm_g)
    k_cmp, v_cmp, k_slc, v_slc, k_win, v_win = [
        a.reshape(B, T, H, d) for a in jnp.split(kv, 6, axis=-1)]

    kc = rms_norm(compress_blocks(k_cmp, cmp_pos[0], cmp_w1[0], cmp_b1[0], cmp_w2[0], cmp_b2[0]), k_norm_g[0])
    vc = compress_blocks(v_cmp, cmp_pos[1], cmp_w1[1], cmp_b1[1], cmp_w2[1], cmp_b2[1])
    s_cmp = jnp.einsum('bthgd,bnhd->bhgtn', q, kc) * scale
    cmp_end = jnp.arange(n_cmp) * CMP_STRIDE + CMP_BLOCK - 1
    p_cmp = masked_softmax(s_cmp, cmp_end[None, :] <= t_pos[:, None])
    o_cmp = jnp.einsum('bhgtn,bnhd->bthgd', p_cmp.astype(vc.dtype), vc)

    cs = jnp.arange(n_cmp) * CMP_STRIDE
    ss = jnp.arange(n_slc) * SLC_BLOCK
    overlap = jnp.clip(jnp.minimum(cs[:, None] + CMP_BLOCK, ss[None, :] + SLC_BLOCK)
                       - jnp.maximum(cs[:, None], ss[None, :]), 0, None).astype(jnp.float32) / CMP_BLOCK
    imp = jnp.einsum('bhgtn,nj->bhtj', p_cmp, overlap)
    t_blk = t_pos[:, None] // SLC_BLOCK
    j = jnp.arange(n_slc)[None, :]
    forced = (j == 0) | (j == t_blk) | (j == t_blk - 1)
    imp = jnp.where(forced, FORCE_SCORE, jnp.where(j <= t_blk, imp, -FORCE_SCORE))
    _, sel = lax.top_k(imp, n_sel)

    k_slc = rms_norm(k_slc, k_norm_g[1])
    kb = k_slc.reshape(B, n_slc, SLC_BLOCK, H, d).transpose(0, 3, 1, 2, 4)
    vb = v_slc.reshape(B, n_slc, SLC_BLOCK, H, d).transpose(0, 3, 1, 2, 4)
    qb = q.reshape(B, n_qb, Q_BLOCK, H, G, d).transpose(1, 0, 2, 3, 4, 5)
    sb = sel.reshape(B, H, n_qb, Q_BLOCK, n_sel).transpose(2, 0, 1, 3, 4)
    tb = t_pos.reshape(n_qb, Q_BLOCK)
    bi = jnp.arange(B)[:, None, None, None]
    hi = jnp.arange(H)[None, :, None, None]

    def selected_block(args):
        qc, ic, tc = args
        kg = kb[bi, hi, ic].reshape(B, H, Q_BLOCK, n_sel * SLC_BLOCK, d)
        vg = vb[bi, hi, ic].reshape(B, H, Q_BLOCK, n_sel * SLC_BLOCK, d)
        key_pos = (ic[..., None] * SLC_BLOCK + jnp.arange(SLC_BLOCK)).reshape(B, H, Q_BLOCK, n_sel * SLC_BLOCK)
        mask = (key_pos <= tc[None, None, :, None])[:, :, None]
        s = jnp.einsum('bthgd,bhtmd->bhgtm', qc, kg) * scale
        p = masked_softmax(s, mask)
        return jnp.einsum('bhgtm,bhtmd->bthgd', p.astype(vg.dtype), vg)

    o_slc = lax.map(selected_block, (qb, sb, tb)).transpose(1, 0, 2, 3, 4, 5).reshape(B, T, H, G, d)

    k_win = rms_norm(k_win, k_norm_g[2])
    pad = ((0, 0), (WINDOW, 0), (0, 0), (0, 0))
    kp = jnp.pad(k_win, pad)
    vp = jnp.pad(v_win, pad)
    win_idx = jnp.arange(n_qb)[:, None] * Q_BLOCK + jnp.arange(WINDOW + Q_BLOCK)[None, :]
    kw = kp[:, win_idx]
    vw = vp[:, win_idx]
    qw = q.reshape(B, n_qb, Q_BLOCK, H, G, d)
    s_win = jnp.einsum('bcthgd,bckhd->bhgctk', qw, kw) * scale
    q_abs = tb[:, :, None]
    k_abs = (win_idx - WINDOW)[:, None, :]
    win_mask = (k_abs <= q_abs) & (k_abs > q_abs - WINDOW) & (k_abs >= 0)
    p_win = masked_softmax(s_win, win_mask)
    o_win = jnp.einsum('bhgctk,bckhd->bcthgd', p_win.astype(vw.dtype), vw).reshape(B, T, H, G, d)

    g = jax.nn.sigmoid(gate_logits.astype(jnp.float32)).reshape(B, T, H, G, N_BRANCH).astype(q.dtype)
    o = g[..., 0:1] * o_cmp + g[..., 1:2] * o_slc + g[..., 2:3] * o_win
    return o.reshape(B, T, NSA_WIDTH)


def memory_cross_attention(h, mem, kv_norm_g, w_mq, w_mkv, q_g, k_g, w_mo):
    B, T, _ = h.shape
    M = mem.shape[1]
    q = rms_norm((h @ w_mq).reshape(B, T, MEM_HEADS, MEM_HEAD_DIM), q_g)
    k, v = jnp.split(rms_norm(mem, kv_norm_g) @ w_mkv, 2, axis=-1)
    k = rms_norm(k.reshape(B, M, MEM_HEADS, MEM_HEAD_DIM), k_g)
    v = v.reshape(B, M, MEM_HEADS, MEM_HEAD_DIM)
    s = jnp.einsum('bthd,bmhd->bhtm', q, k) * MEM_HEAD_DIM ** -0.5
    p = jax.nn.softmax(s.astype(jnp.float32), axis=-1).astype(v.dtype)
    o = jnp.einsum('bhtm,bmhd->bthd', p, v).reshape(B, T, MEM_HEADS * MEM_HEAD_DIM)
    return o @ w_mo


def setup_inputs(seed: int = 0) -> dict:
    key = jax.random.key(seed)
    ks = jax.random.split(key, 32)

    def nrm(k, shape, scale):
        return jax.random.normal(k, shape, jnp.float32) * scale

    def gain(k, shape):
        return 1.0 + 0.05 * jax.random.normal(k, shape, jnp.float32)

    L = DEPTH
    return {
        'x': nrm(ks[0], (BATCH, SEQ, D_MODEL), 1.0),
        'mem': nrm(ks[1], (BATCH, MEM_LEN, D_MODEL), 1.0),
        'norm_mix_g': gain(ks[2], (L, D_MODEL)),
        'w_in': nrm(ks[3], (L, D_MODEL, IN_COLS), D_MODEL ** -0.5),
        'sg_ln_g': gain(ks[4], (L, SG_WIDTH)),
        'sg_ln_b': nrm(ks[5], (L, SG_WIDTH), 0.02),
        'sg_w': nrm(ks[6], (L, SG_GROUPS, SG_CHUNK, SG_CHUNK), SG_CHUNK ** -0.5),
        'sg_b': 1.0 + nrm(ks[7], (L, SG_GROUPS, SG_CHUNK), 0.1),
        'q_norm_g': gain(ks[8], (L, HEAD_DIM)),
        'k_norm_g': gain(ks[9], (L, N_BRANCH, HEAD_DIM)),
        'cmp_pos': nrm(ks[10], (L, 2, CMP_BLOCK, HEAD_DIM), 0.5),
        'cmp_w1': nrm(ks[11], (L, 2, CMP_BLOCK * HEAD_DIM, CMP_HIDDEN), (CMP_BLOCK * HEAD_DIM) ** -0.5),
        'cmp_b1': nrm(ks[12], (L, 2, CMP_HIDDEN), 0.02),
        'cmp_w2': nrm(ks[13], (L, 2, CMP_HIDDEN, HEAD_DIM), CMP_HIDDEN ** -0.5),
        'cmp_b2': nrm(ks[14], (L, 2, HEAD_DIM), 0.02),
        'mix_out_g': gain(ks[15], (L, 2, SG_WIDTH)),
        'w_out': nrm(ks[16], (L, D_MIX, D_MODEL), D_MIX ** -0.5),
        'norm_mem_g': gain(ks[17], (L, D_MODEL)),
        'mem_kv_norm_g': gain(ks[18], (L, D_MODEL)),
        'w_mq': nrm(ks[19], (L, D_MODEL, MEM_HEADS * MEM_HEAD_DIM), D_MODEL ** -0.5),
        'w_mkv': nrm(ks[20], (L, D_MODEL, 2 * MEM_HEADS * MEM_HEAD_DIM), D_MODEL ** -0.5),
        'mem_q_norm_g': gain(ks[21], (L, MEM_HEAD_DIM)),
        'mem_k_norm_g': gain(ks[22], (L, MEM_HEAD_DIM)),
        'w_mo': nrm(ks[23], (L, MEM_HEADS * MEM_HEAD_DIM, D_MODEL), (MEM_HEADS * MEM_HEAD_DIM) ** -0.5),
        'norm_ffn_g': gain(ks[24], (L, D_MODEL)),
        'w_ff1': nrm(ks[25], (L, D_MODEL, D_FF), D_MODEL ** -0.5),
        'w_ff2': nrm(ks[26], (L, D_FF, D_MODEL), D_FF ** -0.5),
    }


def reference(x, mem, norm_mix_g, w_in, sg_ln_g, sg_ln_b, sg_w, sg_b, q_norm_g, k_norm_g,
              cmp_pos, cmp_w1, cmp_b1, cmp_w2, cmp_b2, mix_out_g, w_out,
              norm_mem_g, mem_kv_norm_g, w_mq, w_mkv, mem_q_norm_g, mem_k_norm_g, w_mo,
              norm_ffn_g, w_ff1, w_ff2):
    splits = [SG_WIDTH, 2 * SG_WIDTH, 2 * SG_WIDTH + NSA_WIDTH,
              2 * SG_WIDTH + NSA_WIDTH + 6 * KV_WIDTH]
    for l in range(DEPTH):
        h = rms_norm(x, norm_mix_g[l])
        z = h @ w_in[l]
        u, v, q, kv, gl = jnp.split(z, splits, axis=-1)
        a = spatial_gating_unit(jax.nn.gelu(u), jax.nn.gelu(v), sg_ln_g[l], sg_ln_b[l], sg_w[l], sg_b[l])
        b = nsa_attention(q, kv, gl, q_norm_g[l], k_norm_g[l], cmp_pos[l], cmp_w1[l], cmp_b1[l],
                          cmp_w2[l], cmp_b2[l])
        mixed = jnp.concatenate([rms_norm(a, mix_out_g[l, 0]), rms_norm(b, mix_out_g[l, 1])], axis=-1)
        x = x + mixed @ w_out[l]
        x = x + memory_cross_attention(rms_norm(x, norm_mem_g[l]), mem, mem_kv_norm_g[l], w_mq[l],
                                       w_mkv[l], mem_q_norm_g[l], mem_k_norm_g[l], w_mo[l])
        h = rms_norm(x, norm_ffn_g[l])
        x = x + jnp.square(jax.nn.relu(h @ w_ff1[l])) @ w_ff2[l]
    return x
```

```python
import functools

import jax
import jax.numpy as jnp
from jax import lax
from jax.experimental import pallas as pl
from jax.experimental.pallas import tpu as pltpu

F32 = jnp.float32
BF16 = jnp.bfloat16

D_MODEL = 1024
HEAD_DIM = 64
SG_WIDTH = 512
SG_GROUPS = 8
SG_CHUNK = 128
NSA_WIDTH = 512
NSA_KV_HEADS = 2
NSA_GROUP = 4
KV_WIDTH = NSA_KV_HEADS * HEAD_DIM
N_BRANCH = 3
CMP_BLOCK = 32
CMP_STRIDE = 16
CMP_HIDDEN = 256
SLC_BLOCK = 64
SLC_TOPK = 16
WINDOW = 512
FORCE_SCORE = 1e4
MEM_HEADS = 4
MEM_HEAD_DIM = 128
D_FF = 4 * D_MODEL
EPS = 1e-6

COL_U = 0
COL_V = SG_WIDTH
COL_Q = 2 * SG_WIDTH
COL_KV = COL_Q + NSA_WIDTH
COL_GATE = COL_KV + 6 * KV_WIDTH
IN_COLS = COL_GATE + NSA_KV_HEADS * NSA_GROUP * N_BRANCH
LANES = 128
IN_COLS_PAD = -(-IN_COLS // LANES) * LANES

MASKED = -1e30
ATTN_TILE = 256
VMEM_LIMIT = 48 * 1024 * 1024


def _cparams(*sem):
    return pltpu.CompilerParams(dimension_semantics=sem, vmem_limit_bytes=VMEM_LIMIT)


def _rms(x, g):
    return x * lax.rsqrt(jnp.mean(x * x, axis=-1, keepdims=True) + EPS) * g


def _const_spec(shape):
    return pl.BlockSpec(shape, lambda *_: (0,) * len(shape))


def _in_proj_kernel(x_ref, g_ref, w_ref, z_ref):
    h = _rms(x_ref[...], g_ref[...]).astype(BF16)
    step = 512
    for c0 in range(0, IN_COLS_PAD, step):
        c1 = min(c0 + step, IN_COLS_PAD)
        z_ref[:, c0:c1] = jnp.dot(h, w_ref[:, c0:c1], preferred_element_type=F32)


def _in_proj(x2, g, w_pad):
    n = x2.shape[0]
    tm = 512
    return pl.pallas_call(
        _in_proj_kernel,
        out_shape=jax.ShapeDtypeStruct((n, IN_COLS_PAD), F32),
        grid=(n // tm,),
        in_specs=[pl.BlockSpec((tm, D_MODEL), lambda i: (i, 0)),
                  _const_spec((1, D_MODEL)),
                  _const_spec((D_MODEL, IN_COLS_PAD))],
        out_specs=pl.BlockSpec((tm, IN_COLS_PAD), lambda i: (i, 0)),
        compiler_params=_cparams("parallel"),
        name="in_proj",
    )(x2, g, w_pad)


def _sgu_kernel(u_ref, v_ref, lng_ref, lnb_ref, w_ref, bs_ref, og_ref, a_ref, *, chunks):
    row = lax.broadcasted_iota(jnp.int32, (SG_CHUNK, SG_CHUNK), 0)
    col = lax.broadcasted_iota(jnp.int32, (SG_CHUNK, SG_CHUNK), 1)
    causal = col <= row
    w = [jnp.where(causal, w_ref[g], 0.0).astype(BF16) for g in range(SG_GROUPS)]
    for c in range(chunks):
        rows = slice(c * SG_CHUNK, (c + 1) * SG_CHUNK)
        u = jax.nn.gelu(u_ref[rows, :])
        v = jax.nn.gelu(v_ref[rows, :])
        mu = jnp.mean(v, axis=-1, keepdims=True)
        vc = v - mu
        vn = vc * lax.rsqrt(jnp.mean(vc * vc, axis=-1, keepdims=True) + EPS) * lng_ref[...] + lnb_ref[...]
        vb = vn.astype(BF16)
        s = jnp.concatenate(
            [jnp.dot(w[g], vb[:, g * HEAD_DIM:(g + 1) * HEAD_DIM], preferred_element_type=F32)
             for g in range(SG_GROUPS)], axis=-1) + bs_ref[...]
        a_ref[rows, :] = _rms(u * s, og_ref[...]).astype(BF16)


def _sgu(z, ln_g, ln_b, w_s, bs_full, out_g):
    n = z.shape[0]
    chunks = 4
    tm = chunks * SG_CHUNK
    return pl.pallas_call(
        functools.partial(_sgu_kernel, chunks=chunks),
        out_shape=jax.ShapeDtypeStruct((n, SG_WIDTH), BF16),
        grid=(n // tm,),
        in_specs=[pl.BlockSpec((tm, SG_WIDTH), lambda i: (i, COL_U // SG_WIDTH)),
                  pl.BlockSpec((tm, SG_WIDTH), lambda i: (i, COL_V // SG_WIDTH)),
                  _const_spec((1, SG_WIDTH)), _const_spec((1, SG_WIDTH)),
                  _const_spec((SG_GROUPS, SG_CHUNK, SG_CHUNK)),
                  _const_spec((SG_CHUNK, SG_WIDTH)), _const_spec((1, SG_WIDTH))],
        out_specs=pl.BlockSpec((tm, SG_WIDTH), lambda i: (i, 0)),
        compiler_params=_cparams("parallel"),
        name="sgu",
    )(z, z, ln_g, ln_b, w_s, bs_full, out_g)


def _kv_prep_kernel(kv_ref, gs_ref, gw_ref, kx_ref, kw_ref, vs_ref, vw_ref, *, tm, seq):
    kv = kv_ref[...]
    t0 = (pl.program_id(0) * tm) % seq
    blk = (t0 + lax.broadcasted_iota(jnp.int32, (tm, SLC_BLOCK), 0)) // SLC_BLOCK
    onehot = jnp.where(blk == lax.broadcasted_iota(jnp.int32, (tm, SLC_BLOCK), 1), 1.0, 0.0).astype(BF16)
    for h in range(NSA_KV_HEADS):
        def part(idx):
            c0 = idx * KV_WIDTH + h * HEAD_DIM
            return kv[:, c0:c0 + HEAD_DIM]
        kx_ref[h, :, :HEAD_DIM] = _rms(part(2), gs_ref[...]).astype(BF16)
        kx_ref[h, :, HEAD_DIM:] = onehot
        vs_ref[h] = part(3).astype(BF16)
        kw_ref[h] = _rms(part(4), gw_ref[...]).astype(BF16)
        vw_ref[h] = part(5).astype(BF16)


def _kv_prep(z, g_slc, g_win, batch, seq):
    n = z.shape[0]
    tm = 512
    per_b = seq // tm
    hd = NSA_KV_HEADS

    def omap(i):
        return (i // per_b, 0, i % per_b, 0)

    def ospec(width):
        return pl.BlockSpec((None, hd, tm, width), omap)

    def oshape(width):
        return jax.ShapeDtypeStruct((batch, hd, seq, width), BF16)

    return pl.pallas_call(
        functools.partial(_kv_prep_kernel, tm=tm, seq=seq),
        out_shape=(oshape(2 * HEAD_DIM), oshape(HEAD_DIM), oshape(HEAD_DIM), oshape(HEAD_DIM)),
        grid=(n // tm,),
        in_specs=[pl.BlockSpec((tm, 6 * KV_WIDTH), lambda i: (i, COL_KV // (6 * KV_WIDTH))),
                  _const_spec((1, HEAD_DIM)), _const_spec((1, HEAD_DIM))],
        out_specs=(ospec(2 * HEAD_DIM), ospec(HEAD_DIM), ospec(HEAD_DIM), ospec(HEAD_DIM)),
        compiler_params=_cparams("parallel"),
        name="kv_prep",
    )(z, g_slc, g_win)


def _cmp_kernel(f_ref, pos_ref, w1_ref, b1_ref, w2_ref, b2_ref, g_ref, o_ref, *, norm):
    f = f_ref[...] + pos_ref[...]
    h1 = jax.nn.gelu(jnp.dot(f, w1_ref[...], preferred_element_type=F32,
                             precision=lax.Precision.HIGHEST) + b1_ref[...])
    y = jnp.dot(h1, w2_ref[...], preferred_element_type=F32,
                precision=lax.Precision.HIGHEST) + b2_ref[...]
    if norm:
        y = _rms(y, g_ref[...])
    o_ref[...] = y


def _compress(flat, pos, w1, b1, w2, b2, g, norm):
    bh, nb, width = flat.shape
    return pl.pallas_call(
        functools.partial(_cmp_kernel, norm=norm),
        out_shape=jax.ShapeDtypeStruct((bh, nb, HEAD_DIM), F32),
        grid=(bh,),
        in_specs=[pl.BlockSpec((None, nb, width), lambda i: (i, 0, 0)),
                  _const_spec((1, width)), _const_spec((width, CMP_HIDDEN)), _const_spec((1, CMP_HIDDEN)),
                  _const_spec((CMP_HIDDEN, HEAD_DIM)), _const_spec((1, HEAD_DIM)), _const_spec((1, HEAD_DIM))],
        out_specs=pl.BlockSpec((None, nb, HEAD_DIM), lambda i: (i, 0, 0)),
        compiler_params=_cparams("parallel"),
        name="compress_k" if norm else "compress_v",
    )(flat, pos, w1, b1, w2, b2, g)


def _attn_kernel(qT_ref, glT_ref, kc_ref, vcT_ref, ovT_ref, kx_ref, vsT_ref, kw_ref, vwT_ref, gq_ref,
                 o_ref, qx_sc, acc_sc, m_sc, l_sc, *, tq, n_cmp):
    i = pl.program_id(2)
    d = HEAD_DIM
    G = NSA_GROUP
    tk = tq

    qn = []
    for g in range(G):
        qg = qT_ref[g * d:(g + 1) * d, :]
        qg = qg * lax.rsqrt(jnp.mean(qg * qg, axis=0, keepdims=True) + EPS) * gq_ref[...]
        qn.append(qg.astype(BF16))

    gates = jax.nn.sigmoid(glT_ref[...])

    ncp = kc_ref.shape[0]
    t_idx = i * tq + lax.broadcasted_iota(jnp.int32, (ncp, tq), 1)
    n_idx = lax.broadcasted_iota(jnp.int32, (ncp, tq), 0)
    cvalid = (n_idx * CMP_STRIDE + (CMP_BLOCK - 1) <= t_idx) & (n_idx < n_cmp)
    kc = kc_ref[...].astype(BF16)
    vcT = vcT_ref[...].astype(BF16)
    psum = jnp.zeros((ncp, tq), F32)
    for g in range(G):
        s = jnp.dot(kc, qn[g], preferred_element_type=F32)
        s = jnp.where(cvalid, s, MASKED)
        m = jnp.max(s, axis=0, keepdims=True)
        e = jnp.where(cvalid, jnp.exp(s - m), 0.0)
        den = jnp.sum(e, axis=0, keepdims=True)
        p = e / jnp.where(den > 0, den, 1.0)
        psum = psum + p
        o_cmp = jnp.dot(vcT, p.astype(BF16), preferred_element_type=F32)
        o_ref[g * d:(g + 1) * d, :] = gates[g * 3:g * 3 + 1, :] * o_cmp

    p_hi = psum.astype(BF16)
    p_lo = (psum - p_hi.astype(F32)).astype(BF16)
    imp = (jnp.dot(ovT_ref[...], p_hi, preferred_element_type=F32)
           + jnp.dot(ovT_ref[...], p_lo, preferred_element_type=F32))
    n_slc = imp.shape[0]
    j_idx = lax.broadcasted_iota(jnp.int32, (n_slc, tq), 0)
    t_blk = (i * tq + lax.broadcasted_iota(jnp.int32, (n_slc, tq), 1)) // SLC_BLOCK
    forced = (j_idx == 0) | (j_idx == t_blk) | (j_idx == t_blk - 1)
    score = jnp.where(forced, FORCE_SCORE, jnp.where(j_idx <= t_blk, imp, -FORCE_SCORE))
    rank = jnp.zeros((n_slc, tq), F32)
    for k in range(n_slc):
        rk = score[k:k + 1, :]
        gt = jnp.where(rk > score, 1.0, 0.0)
        ge = jnp.where(rk >= score, 1.0, 0.0)
        rank = rank + jnp.where(j_idx > k, ge, gt)
    n_sel = min(SLC_TOPK, n_slc)
    bias = jnp.where(rank < n_sel, 0.0, MASKED).astype(BF16)
    if n_slc < d:
        bias = jnp.concatenate([bias, jnp.zeros((d - n_slc, tq), BF16)], axis=0)
    for g in range(G):
        qx_sc[0:d, g * tq:(g + 1) * tq] = qn[g]
        qx_sc[d:2 * d, g * tq:(g + 1) * tq] = bias

    r_idx = lax.broadcasted_iota(jnp.int32, (tk, tq), 0)
    c_idx = lax.broadcasted_iota(jnp.int32, (tk, tq), 1)

    def reset():
        m_sc[...] = jnp.full(m_sc.shape, -jnp.inf, F32)
        l_sc[...] = jnp.zeros(l_sc.shape, F32)
        acc_sc[...] = jnp.zeros(acc_sc.shape, F32)

    def flash_step(k_tile, vT_tile, q_rows, mask):
        for g in range(G):
            s = jnp.dot(k_tile, qx_sc[0:q_rows, g * tq:(g + 1) * tq], preferred_element_type=F32)
            if mask is not None:
                s = jnp.where(mask, s, MASKED)
            m_old = m_sc[g]
            m_new = jnp.maximum(m_old, jnp.max(s, axis=0, keepdims=True))
            alpha = jnp.exp(m_old - m_new)
            p = jnp.exp(s - m_new)
            l_sc[g] = alpha * l_sc[g] + jnp.sum(p, axis=0, keepdims=True)
            acc_sc[g] = alpha * acc_sc[g] + jnp.dot(vT_tile, p.astype(BF16), preferred_element_type=F32)
            m_sc[g] = m_new

    def finish(branch):
        for g in range(G):
            o = acc_sc[g] / l_sc[g]
            o_ref[g * d:(g + 1) * d, :] += gates[g * 3 + branch:g * 3 + branch + 1, :] * o

    reset()

    def slc_body(j, carry):
        flash_step(kx_ref[j], vsT_ref[j], 2 * d, None)
        return carry

    lax.fori_loop(0, i, slc_body, 0)
    flash_step(kx_ref[i], vsT_ref[i], 2 * d, r_idx <= c_idx)
    finish(1)

    reset()

    @pl.when(i >= 2)
    def _():
        flash_step(kw_ref[i - 2], vwT_ref[i - 2], d, r_idx > c_idx)

    @pl.when(i >= 1)
    def _():
        flash_step(kw_ref[i - 1], vwT_ref[i - 1], d, None)

    flash_step(kw_ref[i], vwT_ref[i], d, r_idx <= c_idx)
    finish(2)


def _attention(qT, glT, kc, vcT, ovT, kx, vsT, kw, vwT, gq, n_cmp):
    batch, _, seq = qT.shape
    tq = ATTN_TILE
    nq = seq // tq
    hd = NSA_KV_HEADS
    gw = NSA_GROUP * HEAD_DIM
    ncp = kc.shape[1]
    n_slc = ovT.shape[0]

    def per_head(shape):
        return pl.BlockSpec((None, None) + shape, lambda b, h, i: (b, h) + (0,) * len(shape))

    return pl.pallas_call(
        functools.partial(_attn_kernel, tq=tq, n_cmp=n_cmp),
        out_shape=jax.ShapeDtypeStruct((batch, NSA_WIDTH, seq), F32),
        grid=(batch, hd, nq),
        in_specs=[pl.BlockSpec((None, gw, tq), lambda b, h, i: (b, h, i)),
                  pl.BlockSpec((None, None, NSA_GROUP * N_BRANCH, tq), lambda b, h, i: (b, h, 0, i)),
                  pl.BlockSpec((None, ncp, HEAD_DIM), lambda b, h, i: (b * hd + h, 0, 0)),
                  pl.BlockSpec((None, HEAD_DIM, ncp), lambda b, h, i: (b * hd + h, 0, 0)),
                  _const_spec((n_slc, ncp)),
                  per_head((nq, tq, 2 * HEAD_DIM)), per_head((nq, HEAD_DIM, tq)),
                  per_head((nq, tq, HEAD_DIM)), per_head((nq, HEAD_DIM, tq)),
                  _const_spec((HEAD_DIM, 1))],
        out_specs=pl.BlockSpec((None, gw, tq), lambda b, h, i: (b, h, i)),
        scratch_shapes=[pltpu.VMEM((2 * HEAD_DIM, NSA_GROUP * tq), BF16),
                        pltpu.VMEM((NSA_GROUP, HEAD_DIM, tq), F32),
                        pltpu.VMEM((NSA_GROUP, 1, tq), F32),
                        pltpu.VMEM((NSA_GROUP, 1, tq), F32)],
        compiler_params=_cparams("parallel", "parallel", "arbitrary"),
        name="nsa_attention",
    )(qT, glT, kc, vcT, ovT, kx, vsT, kw, vwT, gq)


def _out_proj_kernel(a_ref, b_ref, x_ref, g_ref, w_ref, o_ref):
    bn = _rms(b_ref[...], g_ref[...]).astype(BF16)
    y = jnp.dot(a_ref[...], w_ref[:SG_WIDTH, :], preferred_element_type=F32)
    y = y + jnp.dot(bn, w_ref[SG_WIDTH:, :], preferred_element_type=F32)
    o_ref[...] = x_ref[...] + y


def _out_proj(a_n, b, x2, g_b, w_out):
    n = x2.shape[0]
    tm = 512
    return pl.pallas_call(
        _out_proj_kernel,
        out_shape=jax.ShapeDtypeStruct((n, D_MODEL), F32),
        grid=(n // tm,),
        in_specs=[pl.BlockSpec((tm, SG_WIDTH), lambda i: (i, 0)),
                  pl.BlockSpec((tm, NSA_WIDTH), lambda i: (i, 0)),
                  pl.BlockSpec((tm, D_MODEL), lambda i: (i, 0)),
                  _const_spec((1, NSA_WIDTH)), _const_spec((D_MODEL, D_MODEL))],
        out_specs=pl.BlockSpec((tm, D_MODEL), lambda i: (i, 0)),
        compiler_params=_cparams("parallel"),
        name="out_proj",
    )(a_n, b, x2, g_b, w_out)


def _mem_kv_kernel(mem_ref, g_ref, w_ref, kg_ref, k_ref, v_ref):
    mn = _rms(mem_ref[...], g_ref[...]).astype(BF16)
    kv = jnp.dot(mn, w_ref[...], preferred_element_type=F32)
    width = MEM_HEADS * MEM_HEAD_DIM
    for hh in range(MEM_HEADS):
        cols = slice(hh * MEM_HEAD_DIM, (hh + 1) * MEM_HEAD_DIM)
        k_ref[:, cols] = _rms(kv[:, cols], kg_ref[...]).astype(BF16)
    v_ref[...] = kv[:, width:].astype(BF16)


def _mem_kv(mem, g, w_mkv, k_g):
    batch, m_len, _ = mem.shape
    width = MEM_HEADS * MEM_HEAD_DIM
    spec = pl.BlockSpec((None, m_len, width), lambda b: (b, 0, 0))
    return pl.pallas_call(
        _mem_kv_kernel,
        out_shape=(jax.ShapeDtypeStruct((batch, m_len, width), BF16),) * 2,
        grid=(batch,),
        in_specs=[pl.BlockSpec((None, m_len, D_MODEL), lambda b: (b, 0, 0)),
                  _const_spec((1, D_MODEL)), _const_spec((D_MODEL, 2 * width)),
                  _const_spec((1, MEM_HEAD_DIM))],
        out_specs=(spec, spec),
        compiler_params=_cparams("parallel"),
        name="mem_kv",
    )(mem, g, w_mkv, k_g)


def _mem_attn_kernel(x_ref, g_ref, wq_ref, qg_ref, k_ref, v_ref, wo_ref, o_ref):
    x = x_ref[...]
    h = _rms(x, g_ref[...]).astype(BF16)
    q = jnp.dot(h, wq_ref[...], preferred_element_type=F32)
    outs = []
    for hh in range(MEM_HEADS):
        cols = slice(hh * MEM_HEAD_DIM, (hh + 1) * MEM_HEAD_DIM)
        qh = (_rms(q[:, cols], qg_ref[...]) * MEM_HEAD_DIM ** -0.5).astype(BF16)
        s = lax.dot_general(qh, k_ref[:, cols], (((1,), (1,)), ((), ())), preferred_element_type=F32)
        m = jnp.max(s, axis=-1, keepdims=True)
        e = jnp.exp(s - m)
        p = e / jnp.sum(e, axis=-1, keepdims=True)
        outs.append(jnp.dot(p.astype(BF16), v_ref[:, cols], preferred_element_type=F32))
    o = jnp.concatenate(outs, axis=-1).astype(BF16)
    o_ref[...] = x + jnp.dot(o, wo_ref[...], preferred_element_type=F32)


def _mem_attn(x2, g, w_mq, q_g, k, v, w_mo, seq):
    n = x2.shape[0]
    tm = 256
    per_b = seq // tm
    m_len = k.shape[1]
    width = MEM_HEADS * MEM_HEAD_DIM
    kv_spec = pl.BlockSpec((None, m_len, width), lambda i: (i // per_b, 0, 0))
    return pl.pallas_call(
        _mem_attn_kernel,
        out_shape=jax.ShapeDtypeStruct((n, D_MODEL), F32),
        grid=(n // tm,),
        in_specs=[pl.BlockSpec((tm, D_MODEL), lambda i: (i, 0)),
                  _const_spec((1, D_MODEL)), _const_spec((D_MODEL, width)), _const_spec((1, MEM_HEAD_DIM)),
                  kv_spec, kv_spec, _const_spec((width, D_MODEL))],
        out_specs=pl.BlockSpec((tm, D_MODEL), lambda i: (i, 0)),
        compiler_params=_cparams("parallel"),
        name="mem_attn",
    )(x2, g, w_mq, q_g, k, v, w_mo)


def _ffn_kernel(x_ref, g_ref, w1_ref, w2_ref, o_ref):
    x = x_ref[...]
    h = _rms(x, g_ref[...]).astype(BF16)
    step = 1024
    acc = x
    for c0 in range(0, D_FF, step):
        a = jnp.maximum(jnp.dot(h, w1_ref[:, c0:c0 + step], preferred_element_type=F32), 0.0)
        acc = acc + jnp.dot((a * a).astype(BF16), w2_ref[c0:c0 + step, :], preferred_element_type=F32)
    o_ref[...] = acc


def _ffn(x2, g, w1, w2):
    n = x2.shape[0]
    tm = 512
    return pl.pallas_call(
        _ffn_kernel,
        out_shape=jax.ShapeDtypeStruct((n, D_MODEL), F32),
        grid=(n // tm,),
        in_specs=[pl.BlockSpec((tm, D_MODEL), lambda i: (i, 0)),
                  _const_spec((1, D_MODEL)),
                  pl.BlockSpec((D_MODEL, D_FF), lambda i: (0, 0), pipeline_mode=pl.Buffered(1)),
                  pl.BlockSpec((D_FF, D_MODEL), lambda i: (0, 0), pipeline_mode=pl.Buffered(1))],
        out_specs=pl.BlockSpec((tm, D_MODEL), lambda i: (i, 0)),
        compiler_params=_cparams("parallel"),
        name="ffn",
    )(x2, g, w1, w2)


def _overlap_T(n_cmp_pad, n_cmp, n_slc):
    cs = jnp.arange(n_cmp_pad) * CMP_STRIDE
    ss = jnp.arange(n_slc) * SLC_BLOCK
    ov = jnp.clip(jnp.minimum(cs[None, :] + CMP_BLOCK, ss[:, None] + SLC_BLOCK)
                  - jnp.maximum(cs[None, :], ss[:, None]), 0, None).astype(F32) / CMP_BLOCK
    ov = jnp.where(jnp.arange(n_cmp_pad)[None, :] < n_cmp, ov, 0.0)
    return ov.astype(BF16)


def _compress_input(z, col, batch, seq):
    n_chunk = seq // CMP_STRIDE
    c = z[:, col:col + KV_WIDTH].reshape(batch, n_chunk, CMP_STRIDE, NSA_KV_HEADS, HEAD_DIM)
    c = c.transpose(0, 3, 1, 2, 4).reshape(batch * NSA_KV_HEADS, n_chunk, CMP_STRIDE * HEAD_DIM)
    flat = jnp.concatenate([c[:, :-1], c[:, 1:]], axis=-1)
    return jnp.pad(flat, ((0, 0), (0, 1), (0, 0)))


def kernel(x, mem, norm_mix_g, w_in, sg_ln_g, sg_ln_b, sg_w, sg_b, q_norm_g, k_norm_g, cmp_pos, cmp_w1, cmp_b1, cmp_w2, cmp_b2, mix_out_g, w_out, norm_mem_g, mem_kv_norm_g, w_mq, w_mkv, mem_q_norm_g, mem_k_norm_g, w_mo, norm_ffn_g, w_ff1, w_ff2):
    batch, seq, _ = x.shape
    depth = w_in.shape[0]
    n = batch * seq
    tq = ATTN_TILE
    assert WINDOW == 2 * tq and seq % 512 == 0
    nq = seq // tq
    n_cmp = seq // CMP_STRIDE - 1
    n_slc = seq // SLC_BLOCK
    assert n_slc <= HEAD_DIM
    ovT = _overlap_T(n_cmp + 1, n_cmp, n_slc)
    row = lambda a: a.reshape(1, -1)

    x2 = x.reshape(n, D_MODEL)
    for l in range(depth):
        w_in_pad = jnp.pad(w_in[l], ((0, 0), (0, IN_COLS_PAD - IN_COLS))).astype(BF16)
        z = _in_proj(x2, row(norm_mix_g[l]), w_in_pad)

        bs_full = jnp.repeat(sg_b[l].T, HEAD_DIM, axis=1)
        a_n = _sgu(z, row(sg_ln_g[l]), row(sg_ln_b[l]), sg_w[l], bs_full, row(mix_out_g[l, 0]))

        kx, kw, vs, vw = _kv_prep(z, row(k_norm_g[l, 1]), row(k_norm_g[l, 2]), batch, seq)
        tiles = lambda a: a.reshape(batch, NSA_KV_HEADS, nq, tq, a.shape[-1])
        kx, kw = tiles(kx), tiles(kw)
        vsT = tiles(vs).transpose(0, 1, 2, 4, 3)
        vwT = tiles(vw).transpose(0, 1, 2, 4, 3)

        kc = _compress(_compress_input(z, COL_KV, batch, seq), row(cmp_pos[l, 0]), cmp_w1[l, 0],
                       row(cmp_b1[l, 0]), cmp_w2[l, 0], row(cmp_b2[l, 0]), row(k_norm_g[l, 0]), True)
        vc = _compress(_compress_input(z, COL_KV + KV_WIDTH, batch, seq), row(cmp_pos[l, 1]), cmp_w1[l, 1],
                       row(cmp_b1[l, 1]), cmp_w2[l, 1], row(cmp_b2[l, 1]), row(k_norm_g[l, 0]), False)
        vcT = vc.transpose(0, 2, 1)

        qT = z[:, COL_Q:COL_Q + NSA_WIDTH].reshape(batch, seq, NSA_WIDTH).transpose(0, 2, 1)
        glT = z[:, COL_GATE:IN_COLS].reshape(batch, seq, NSA_KV_HEADS, NSA_GROUP * N_BRANCH).transpose(0, 2, 3, 1)
        gq = (q_norm_g[l] * HEAD_DIM ** -0.5).reshape(HEAD_DIM, 1)
        oT = _attention(qT, glT, kc, vcT, ovT, kx, vsT, kw, vwT, gq, n_cmp)
        b = oT.transpose(0, 2, 1).reshape(n, NSA_WIDTH)

        x2 = _out_proj(a_n, b, x2, row(mix_out_g[l, 1]), w_out[l].astype(BF16))

        mk, mv = _mem_kv(mem, row(mem_kv_norm_g[l]), w_mkv[l].astype(BF16), row(mem_k_norm_g[l]))
        x2 = _mem_attn(x2, row(norm_mem_g[l]), w_mq[l].astype(BF16), row(mem_q_norm_g[l]), mk, mv,
                       w_mo[l].astype(BF16), seq)

        x2 = _ffn(x2, row(norm_ffn_g[l]), w_ff1[l].astype(BF16), w_ff2[l].astype(BF16))
    return x2.reshape(batch, seq, D_MODEL)
```

```python
import functools

import jax
import jax.numpy as jnp
from jax import lax
from jax.experimental import pallas as pl
from jax.experimental.pallas import tpu as pltpu

F32 = jnp.float32
BF16 = jnp.bfloat16

D_MODEL = 1024
HEAD_DIM = 64
SG_WIDTH = 512
SG_GROUPS = 8
SG_CHUNK = 128
NSA_WIDTH = 512
NSA_KV_HEADS = 2
NSA_GROUP = 4
KV_WIDTH = NSA_KV_HEADS * HEAD_DIM
N_BRANCH = 3
CMP_BLOCK = 32
CMP_STRIDE = 16
CMP_HIDDEN = 256
SLC_BLOCK = 64
SLC_TOPK = 16
WINDOW = 512
FORCE_SCORE = 1e4
MEM_HEADS = 4
MEM_HEAD_DIM = 128
D_FF = 4 * D_MODEL
EPS = 1e-6

COL_U = 0
COL_V = SG_WIDTH
COL_Q = 2 * SG_WIDTH
COL_KV = COL_Q + NSA_WIDTH
COL_GATE = COL_KV + 6 * KV_WIDTH
IN_COLS = COL_GATE + NSA_KV_HEADS * NSA_GROUP * N_BRANCH
LANES = 128
IN_COLS_PAD = -(-IN_COLS // LANES) * LANES

MASKED = -1e30
LOG2E = 1.4426950408889634
BF16_SUBLANES = 16
ATTN_TILE = 256
VMEM_LIMIT = 48 * 1024 * 1024


def _cparams(*sem):
    return pltpu.CompilerParams(dimension_semantics=sem, vmem_limit_bytes=VMEM_LIMIT)


def _rms(x, g):
    return x * lax.rsqrt(jnp.mean(x * x, axis=-1, keepdims=True) + EPS) * g


def _const_spec(shape):
    return pl.BlockSpec(shape, lambda *_: (0,) * len(shape))


def _in_proj_kernel(x_ref, g_ref, w_ref, z_ref):
    h = _rms(x_ref[...], g_ref[...]).astype(BF16)
    step = 512
    for c0 in range(0, IN_COLS_PAD, step):
        c1 = min(c0 + step, IN_COLS_PAD)
        z_ref[:, c0:c1] = jnp.dot(h, w_ref[:, c0:c1], preferred_element_type=F32)


def _in_proj(x2, g, w_pad):
    n = x2.shape[0]
    tm = 512
    return pl.pallas_call(
        _in_proj_kernel,
        out_shape=jax.ShapeDtypeStruct((n, IN_COLS_PAD), F32),
        grid=(n // tm,),
        in_specs=[pl.BlockSpec((tm, D_MODEL), lambda i: (i, 0)),
                  _const_spec((1, D_MODEL)),
                  _const_spec((D_MODEL, IN_COLS_PAD))],
        out_specs=pl.BlockSpec((tm, IN_COLS_PAD), lambda i: (i, 0)),
        compiler_params=_cparams("parallel"),
        name="in_proj",
    )(x2, g, w_pad)


def _sgu_kernel(u_ref, v_ref, lng_ref, lnb_ref, w_ref, bs_ref, og_ref, a_ref, *, chunks):
    row = lax.broadcasted_iota(jnp.int32, (SG_CHUNK, SG_CHUNK), 0)
    col = lax.broadcasted_iota(jnp.int32, (SG_CHUNK, SG_CHUNK), 1)
    causal = col <= row
    w = [jnp.where(causal, w_ref[g], 0.0).astype(BF16) for g in range(SG_GROUPS)]
    for c in range(chunks):
        rows = slice(c * SG_CHUNK, (c + 1) * SG_CHUNK)
        u = jax.nn.gelu(u_ref[rows, :])
        v = jax.nn.gelu(v_ref[rows, :])
        mu = jnp.mean(v, axis=-1, keepdims=True)
        vc = v - mu
        vn = vc * lax.rsqrt(jnp.mean(vc * vc, axis=-1, keepdims=True) + EPS) * lng_ref[...] + lnb_ref[...]
        vb = vn.astype(BF16)
        s = jnp.concatenate(
            [jnp.dot(w[g], vb[:, g * HEAD_DIM:(g + 1) * HEAD_DIM], preferred_element_type=F32)
             for g in range(SG_GROUPS)], axis=-1) + bs_ref[...]
        a_ref[rows, :] = _rms(u * s, og_ref[...]).astype(BF16)


def _sgu(z, ln_g, ln_b, w_s, bs_full, out_g):
    n = z.shape[0]
    chunks = 4
    tm = chunks * SG_CHUNK
    return pl.pallas_call(
        functools.partial(_sgu_kernel, chunks=chunks),
        out_shape=jax.ShapeDtypeStruct((n, SG_WIDTH), BF16),
        grid=(n // tm,),
        in_specs=[pl.BlockSpec((tm, SG_WIDTH), lambda i: (i, COL_U // SG_WIDTH)),
                  pl.BlockSpec((tm, SG_WIDTH), lambda i: (i, COL_V // SG_WIDTH)),
                  _const_spec((1, SG_WIDTH)), _const_spec((1, SG_WIDTH)),
                  _const_spec((SG_GROUPS, SG_CHUNK, SG_CHUNK)),
                  _const_spec((SG_CHUNK, SG_WIDTH)), _const_spec((1, SG_WIDTH))],
        out_specs=pl.BlockSpec((tm, SG_WIDTH), lambda i: (i, 0)),
        compiler_params=_cparams("parallel"),
        name="sgu",
    )(z, z, ln_g, ln_b, w_s, bs_full, out_g)


def _kv_prep_kernel(kv_ref, gs_ref, gw_ref, kx_ref, v_ref, *, tm, seq):
    kv = kv_ref[...]
    t0 = (pl.program_id(0) * tm) % seq
    blk = (t0 + lax.broadcasted_iota(jnp.int32, (tm, SLC_BLOCK), 0)) // SLC_BLOCK
    onehot = jnp.where(blk == lax.broadcasted_iota(jnp.int32, (tm, SLC_BLOCK), 1), 1.0, 0.0).astype(BF16)
    for h in range(NSA_KV_HEADS):
        def part(idx):
            c0 = idx * KV_WIDTH + h * HEAD_DIM
            return kv[:, c0:c0 + HEAD_DIM]
        kx_ref[h, 0, :, :HEAD_DIM] = _rms(part(2), gs_ref[...]).astype(BF16)
        kx_ref[h, 0, :, HEAD_DIM:] = onehot
        kx_ref[h, 1, :, :HEAD_DIM] = _rms(part(4), gw_ref[...]).astype(BF16)
        kx_ref[h, 1, :, HEAD_DIM:] = jnp.zeros((tm, SLC_BLOCK), BF16)
        v_ref[h, 0] = part(3).astype(BF16)
        v_ref[h, 1] = part(5).astype(BF16)


def _kv_prep(z, g_slc, g_win, batch, seq):
    n = z.shape[0]
    tm = 512
    per_b = seq // tm
    hd = NSA_KV_HEADS

    def ospec(width):
        return pl.BlockSpec((None, hd, 2, tm, width), lambda i: (i // per_b, 0, 0, i % per_b, 0))

    def oshape(width):
        return jax.ShapeDtypeStruct((batch, hd, 2, seq, width), BF16)

    return pl.pallas_call(
        functools.partial(_kv_prep_kernel, tm=tm, seq=seq),
        out_shape=(oshape(2 * HEAD_DIM), oshape(HEAD_DIM)),
        grid=(n // tm,),
        in_specs=[pl.BlockSpec((tm, 6 * KV_WIDTH), lambda i: (i, COL_KV // (6 * KV_WIDTH))),
                  _const_spec((1, HEAD_DIM)), _const_spec((1, HEAD_DIM))],
        out_specs=(ospec(2 * HEAD_DIM), ospec(HEAD_DIM)),
        compiler_params=_cparams("parallel"),
        name="kv_prep",
    )(z, g_slc, g_win)


def _cmp_kernel(f_ref, pos_ref, w1_ref, b1_ref, w2_ref, b2_ref, g_ref, o_ref, *, norm):
    f = f_ref[...] + pos_ref[...]
    h1 = jax.nn.gelu(jnp.dot(f, w1_ref[...], preferred_element_type=F32,
                             precision=lax.Precision.HIGHEST) + b1_ref[...])
    y = jnp.dot(h1, w2_ref[...], preferred_element_type=F32,
                precision=lax.Precision.HIGHEST) + b2_ref[...]
    if norm:
        y = _rms(y, g_ref[...])
    o_ref[...] = y


def _compress(flat, pos, w1, b1, w2, b2, g, norm):
    bh, nb, width = flat.shape
    return pl.pallas_call(
        functools.partial(_cmp_kernel, norm=norm),
        out_shape=jax.ShapeDtypeStruct((bh, nb, HEAD_DIM), F32),
        grid=(bh,),
        in_specs=[pl.BlockSpec((None, nb, width), lambda i: (i, 0, 0)),
                  _const_spec((1, width)), _const_spec((width, CMP_HIDDEN)), _const_spec((1, CMP_HIDDEN)),
                  _const_spec((CMP_HIDDEN, HEAD_DIM)), _const_spec((1, HEAD_DIM)), _const_spec((1, HEAD_DIM))],
        out_specs=pl.BlockSpec((None, nb, HEAD_DIM), lambda i: (i, 0, 0)),
        compiler_params=_cparams("parallel"),
        name="compress_k" if norm else "compress_v",
    )(flat, pos, w1, b1, w2, b2, g)


def _attn_kernel(qT_ref, glT_ref, kc_ref, vcT_ref, ovT_ref, kx_ref, vT_ref, mask_ref, gq_ref,
                 o_ref, qx_sc, s_a, mo_a, mn_a, s_b, mo_b, mn_b, mrun_sc, acc_sc, *, tq, n_cmp, nq):
    i = pl.program_id(2)
    d = HEAD_DIM
    G = NSA_GROUP

    qn = []
    for g in range(G):
        qg = qT_ref[g * d:(g + 1) * d, :]
        qg = qg * lax.rsqrt(jnp.mean(qg * qg, axis=0, keepdims=True) + EPS) * gq_ref[...]
        qn.append(qg.astype(BF16))

    gates = jax.nn.sigmoid(glT_ref[...])

    ncp = kc_ref.shape[0]
    t_idx = i * tq + lax.broadcasted_iota(jnp.int32, (ncp, tq), 1)
    n_idx = lax.broadcasted_iota(jnp.int32, (ncp, tq), 0)
    cvalid = (n_idx * CMP_STRIDE + (CMP_BLOCK - 1) <= t_idx) & (n_idx < n_cmp)
    kc = kc_ref[...].astype(BF16)
    vcT = vcT_ref[...].astype(BF16)
    psum = jnp.zeros((ncp, tq), F32)
    for g in range(G):
        s = jnp.dot(kc, qn[g], preferred_element_type=F32)
        s = jnp.where(cvalid, s, MASKED)
        m = jnp.max(s, axis=0, keepdims=True)
        e = jnp.where(cvalid, jnp.exp2(s - m), 0.0)
        den = jnp.sum(e, axis=0, keepdims=True)
        p = e * (1.0 / jnp.where(den > 0, den, 1.0))
        psum = psum + p
        o_cmp = jnp.dot(vcT, p.astype(BF16), preferred_element_type=F32)
        o_ref[g * d:(g + 1) * d, :] = gates[g * 3:g * 3 + 1, :] * o_cmp

    p_hi = psum.astype(BF16)
    p_lo = (psum - p_hi.astype(F32)).astype(BF16)
    imp = (jnp.dot(ovT_ref[...], p_hi, preferred_element_type=F32)
           + jnp.dot(ovT_ref[...], p_lo, preferred_element_type=F32))
    n_slc = imp.shape[0]
    j_idx = lax.broadcasted_iota(jnp.int32, (n_slc, tq), 0)
    t_blk = (i * tq + lax.broadcasted_iota(jnp.int32, (n_slc, tq), 1)) // SLC_BLOCK
    forced = (j_idx == 0) | (j_idx == t_blk) | (j_idx == t_blk - 1)
    score = jnp.where(forced, FORCE_SCORE, jnp.where(j_idx <= t_blk, imp, -FORCE_SCORE))
    sub = 8
    groups = [score[r:r + sub, :] for r in range(0, n_slc, sub)]
    ranks = [jnp.zeros((sub, tq), F32) for _ in groups]
    row_in_group = lax.broadcasted_iota(jnp.int32, (sub, tq), 0)
    for k in range(n_slc):
        rk = jnp.broadcast_to(score[k:k + 1, :], (sub, tq))
        for r, sg in enumerate(groups):
            ge = jnp.where(rk >= sg, 1.0, 0.0)
            gt = jnp.where(rk > sg, 1.0, 0.0)
            if r * sub > k:
                before = ge
            elif r * sub + sub - 1 < k:
                before = gt
            else:
                before = jnp.where(row_in_group > k - r * sub, ge, gt)
            ranks[r] = ranks[r] + before
    rank = jnp.concatenate(ranks, axis=0)
    n_sel = min(SLC_TOPK, n_slc)
    bias = jnp.where(rank < n_sel, 0.0, MASKED).astype(BF16)
    if n_slc < d:
        bias = jnp.concatenate([bias, jnp.zeros((d - n_slc, tq), BF16)], axis=0)
    for g in range(G):
        qx_sc[0:d, g * tq:(g + 1) * tq] = qn[g]
        qx_sc[d:2 * d, g * tq:(g + 1) * tq] = bias

    n_win = jnp.minimum(i, 2) + 1
    n_jobs = i + 1 + n_win

    def job(k):
        is_win = k > i
        j_win = i - (n_win - 1) + (k - (i + 1))
        tile = jnp.where(is_win, j_win, k)
        mask_win = jnp.where(j_win == i, 1, jnp.where(j_win == i - 2, 2, 0))
        mask = jnp.where(is_win, mask_win, jnp.where(k == i, 1, 0))
        pad = k >= n_jobs
        tile = jnp.where(pad, i, tile)
        mask = jnp.where(pad, 3, mask)
        branch = is_win.astype(jnp.int32)
        return branch, branch * nq + tile, mask

    mrun_sc[...] = jnp.full(mrun_sc.shape, -jnp.inf, F32)
    acc_sc[...] = jnp.zeros(acc_sc.shape, F32)

    def scores(k, s_buf, mo_buf, mn_buf):
        branch, kv_idx, mask = job(k)
        k_tile = kx_ref[kv_idx]
        m_tile = mask_ref[mask]
        for g in range(G):
            s = jnp.dot(k_tile, qx_sc[:, g * tq:(g + 1) * tq], preferred_element_type=F32) + m_tile
            s_buf[g] = s
            m_old = mrun_sc[branch, g]
            m_new = jnp.maximum(m_old, jnp.max(s, axis=0, keepdims=True))
            mo_buf[g] = m_old
            mn_buf[g] = m_new
            mrun_sc[branch, g] = m_new

    def accumulate(k, s_buf, mo_buf, mn_buf):
        branch, kv_idx, _ = job(k)
        vT_tile = vT_ref[kv_idx]
        for g in range(G):
            m_new = mn_buf[g]
            alpha = jnp.exp2(mo_buf[g] - m_new)
            p = jnp.exp2(s_buf[g] - m_new).astype(BF16)
            acc_sc[branch, g] = alpha * acc_sc[branch, g] + jnp.dot(vT_tile, p, preferred_element_type=F32)

    scores(0, s_a, mo_a, mn_a)

    def pair(kk, carry):
        k0 = 2 * kk
        scores(k0 + 1, s_b, mo_b, mn_b)
        accumulate(k0, s_a, mo_a, mn_a)
        scores(k0 + 2, s_a, mo_a, mn_a)
        accumulate(k0 + 1, s_b, mo_b, mn_b)
        return carry

    lax.fori_loop(0, (n_jobs + 1) // 2, pair, 0)

    for branch in range(2):
        for g in range(G):
            acc = acc_sc[branch, g]
            o = acc[0:d, :] * (1.0 / acc[d:d + 1, :])
            gate = gates[g * 3 + branch + 1:g * 3 + branch + 2, :]
            o_ref[g * d:(g + 1) * d, :] += gate * o


def _attention(qT, glT, kc, vcT, ovT, kx, vT, masks, gq, n_cmp):
    batch, _, seq = qT.shape
    tq = ATTN_TILE
    nq = seq // tq
    hd = NSA_KV_HEADS
    gw = NSA_GROUP * HEAD_DIM
    ncp = kc.shape[1]
    n_slc = ovT.shape[0]
    v_rows = vT.shape[-2]

    def per_head(shape):
        return pl.BlockSpec((None, None) + shape, lambda b, h, i: (b, h) + (0,) * len(shape))

    stage = [pltpu.VMEM((NSA_GROUP, tq, tq), F32),
             pltpu.VMEM((NSA_GROUP, 1, tq), F32),
             pltpu.VMEM((NSA_GROUP, 1, tq), F32)]
    return pl.pallas_call(
        functools.partial(_attn_kernel, tq=tq, n_cmp=n_cmp, nq=nq),
        out_shape=jax.ShapeDtypeStruct((batch, NSA_WIDTH, seq), F32),
        grid=(batch, hd, nq),
        in_specs=[pl.BlockSpec((None, gw, tq), lambda b, h, i: (b, h, i)),
                  pl.BlockSpec((None, None, NSA_GROUP * N_BRANCH, tq), lambda b, h, i: (b, h, 0, i)),
                  pl.BlockSpec((None, ncp, HEAD_DIM), lambda b, h, i: (b * hd + h, 0, 0)),
                  pl.BlockSpec((None, HEAD_DIM, ncp), lambda b, h, i: (b * hd + h, 0, 0)),
                  _const_spec((n_slc, ncp)),
                  per_head((2 * nq, tq, 2 * HEAD_DIM)), per_head((2 * nq, v_rows, tq)),
                  _const_spec((4, tq, tq)),
                  _const_spec((HEAD_DIM, 1))],
        out_specs=pl.BlockSpec((None, gw, tq), lambda b, h, i: (b, h, i)),
        scratch_shapes=[pltpu.VMEM((2 * HEAD_DIM, NSA_GROUP * tq), BF16)] + stage + stage + [
            pltpu.VMEM((2, NSA_GROUP, 1, tq), F32),
            pltpu.VMEM((2, NSA_GROUP, v_rows, tq), F32)],
        compiler_params=_cparams("parallel", "parallel", "arbitrary"),
        name="nsa_attention",
    )(qT, glT, kc, vcT, ovT, kx, vT, masks, gq)


def _out_proj_kernel(a_ref, b_ref, x_ref, g_ref, w_ref, o_ref):
    bn = _rms(b_ref[...], g_ref[...]).astype(BF16)
    y = jnp.dot(a_ref[...], w_ref[:SG_WIDTH, :], preferred_element_type=F32)
    y = y + jnp.dot(bn, w_ref[SG_WIDTH:, :], preferred_element_type=F32)
    o_ref[...] = x_ref[...] + y


def _out_proj(a_n, b, x2, g_b, w_out):
    n = x2.shape[0]
    tm = 512
    return pl.pallas_call(
        _out_proj_kernel,
        out_shape=jax.ShapeDtypeStruct((n, D_MODEL), F32),
        grid=(n // tm,),
        in_specs=[pl.BlockSpec((tm, SG_WIDTH), lambda i: (i, 0)),
                  pl.BlockSpec((tm, NSA_WIDTH), lambda i: (i, 0)),
                  pl.BlockSpec((tm, D_MODEL), lambda i: (i, 0)),
                  _const_spec((1, NSA_WIDTH)), _const_spec((D_MODEL, D_MODEL))],
        out_specs=pl.BlockSpec((tm, D_MODEL), lambda i: (i, 0)),
        compiler_params=_cparams("parallel"),
        name="out_proj",
    )(a_n, b, x2, g_b, w_out)


def _mem_kv_kernel(mem_ref, g_ref, w_ref, kg_ref, k_ref, v_ref):
    mn = _rms(mem_ref[...], g_ref[...]).astype(BF16)
    kv = jnp.dot(mn, w_ref[...], preferred_element_type=F32)
    width = MEM_HEADS * MEM_HEAD_DIM
    for hh in range(MEM_HEADS):
        cols = slice(hh * MEM_HEAD_DIM, (hh + 1) * MEM_HEAD_DIM)
        k_ref[:, cols] = _rms(kv[:, cols], kg_ref[...]).astype(BF16)
    v_ref[...] = kv[:, width:].astype(BF16)


def _mem_kv(mem, g, w_mkv, k_g):
    batch, m_len, _ = mem.shape
    width = MEM_HEADS * MEM_HEAD_DIM
    spec = pl.BlockSpec((None, m_len, width), lambda b: (b, 0, 0))
    return pl.pallas_call(
        _mem_kv_kernel,
        out_shape=(jax.ShapeDtypeStruct((batch, m_len, width), BF16),) * 2,
        grid=(batch,),
        in_specs=[pl.BlockSpec((None, m_len, D_MODEL), lambda b: (b, 0, 0)),
                  _const_spec((1, D_MODEL)), _const_spec((D_MODEL, 2 * width)),
                  _const_spec((1, MEM_HEAD_DIM))],
        out_specs=(spec, spec),
        compiler_params=_cparams("parallel"),
        name="mem_kv",
    )(mem, g, w_mkv, k_g)


def _mem_attn_kernel(x_ref, g_ref, wq_ref, qg_ref, k_ref, v_ref, wo_ref, o_ref):
    x = x_ref[...]
    h = _rms(x, g_ref[...]).astype(BF16)
    q = jnp.dot(h, wq_ref[...], preferred_element_type=F32)
    outs = []
    for hh in range(MEM_HEADS):
        cols = slice(hh * MEM_HEAD_DIM, (hh + 1) * MEM_HEAD_DIM)
        qh = (_rms(q[:, cols], qg_ref[...]) * MEM_HEAD_DIM ** -0.5).astype(BF16)
        s = lax.dot_general(qh, k_ref[:, cols], (((1,), (1,)), ((), ())), preferred_element_type=F32)
        m = jnp.max(s, axis=-1, keepdims=True)
        e = jnp.exp(s - m)
        p = e / jnp.sum(e, axis=-1, keepdims=True)
        outs.append(jnp.dot(p.astype(BF16), v_ref[:, cols], preferred_element_type=F32))
    o = jnp.concatenate(outs, axis=-1).astype(BF16)
    o_ref[...] = x + jnp.dot(o, wo_ref[...], preferred_element_type=F32)


def _mem_attn(x2, g, w_mq, q_g, k, v, w_mo, seq):
    n = x2.shape[0]
    tm = 256
    per_b = seq // tm
    m_len = k.shape[1]
    width = MEM_HEADS * MEM_HEAD_DIM
    kv_spec = pl.BlockSpec((None, m_len, width), lambda i: (i // per_b, 0, 0))
    return pl.pallas_call(
        _mem_attn_kernel,
        out_shape=jax.ShapeDtypeStruct((n, D_MODEL), F32),
        grid=(n // tm,),
        in_specs=[pl.BlockSpec((tm, D_MODEL), lambda i: (i, 0)),
                  _const_spec((1, D_MODEL)), _const_spec((D_MODEL, width)), _const_spec((1, MEM_HEAD_DIM)),
                  kv_spec, kv_spec, _const_spec((width, D_MODEL))],
        out_specs=pl.BlockSpec((tm, D_MODEL), lambda i: (i, 0)),
        compiler_params=_cparams("parallel"),
        name="mem_attn",
    )(x2, g, w_mq, q_g, k, v, w_mo)


def _ffn_kernel(x_ref, g_ref, w1_ref, w2_ref, o_ref):
    x = x_ref[...]
    h = _rms(x, g_ref[...]).astype(BF16)
    step = 1024
    acc = x
    for c0 in range(0, D_FF, step):
        a = jnp.maximum(jnp.dot(h, w1_ref[:, c0:c0 + step], preferred_element_type=F32), 0.0)
        acc = acc + jnp.dot((a * a).astype(BF16), w2_ref[c0:c0 + step, :], preferred_element_type=F32)
    o_ref[...] = acc


def _ffn(x2, g, w1, w2):
    n = x2.shape[0]
    tm = 512
    return pl.pallas_call(
        _ffn_kernel,
        out_shape=jax.ShapeDtypeStruct((n, D_MODEL), F32),
        grid=(n // tm,),
        in_specs=[pl.BlockSpec((tm, D_MODEL), lambda i: (i, 0)),
                  _const_spec((1, D_MODEL)),
                  pl.BlockSpec((D_MODEL, D_FF), lambda i: (0, 0), pipeline_mode=pl.Buffered(1)),
                  pl.BlockSpec((D_FF, D_MODEL), lambda i: (0, 0), pipeline_mode=pl.Buffered(1))],
        out_specs=pl.BlockSpec((tm, D_MODEL), lambda i: (i, 0)),
        compiler_params=_cparams("parallel"),
        name="ffn",
    )(x2, g, w1, w2)


def _overlap_T(n_cmp_pad, n_cmp, n_slc):
    cs = jnp.arange(n_cmp_pad) * CMP_STRIDE
    ss = jnp.arange(n_slc) * SLC_BLOCK
    ov = jnp.clip(jnp.minimum(cs[None, :] + CMP_BLOCK, ss[:, None] + SLC_BLOCK)
                  - jnp.maximum(cs[None, :], ss[:, None]), 0, None).astype(F32) / CMP_BLOCK
    ov = jnp.where(jnp.arange(n_cmp_pad)[None, :] < n_cmp, ov, 0.0)
    return ov.astype(BF16)


def _compress_input(z, col, batch, seq):
    n_chunk = seq // CMP_STRIDE
    c = z[:, col:col + KV_WIDTH].reshape(batch, n_chunk, CMP_STRIDE, NSA_KV_HEADS, HEAD_DIM)
    c = c.transpose(0, 3, 1, 2, 4).reshape(batch * NSA_KV_HEADS, n_chunk, CMP_STRIDE * HEAD_DIM)
    flat = jnp.concatenate([c[:, :-1], c[:, 1:]], axis=-1)
    return jnp.pad(flat, ((0, 0), (0, 1), (0, 0)))


def kernel(x, mem, norm_mix_g, w_in, sg_ln_g, sg_ln_b, sg_w, sg_b, q_norm_g, k_norm_g, cmp_pos, cmp_w1, cmp_b1, cmp_w2, cmp_b2, mix_out_g, w_out, norm_mem_g, mem_kv_norm_g, w_mq, w_mkv, mem_q_norm_g, mem_k_norm_g, w_mo, norm_ffn_g, w_ff1, w_ff2):
    batch, seq, _ = x.shape
    depth = w_in.shape[0]
    n = batch * seq
    tq = ATTN_TILE
    assert WINDOW == 2 * tq and seq % 512 == 0
    nq = seq // tq
    n_cmp = seq // CMP_STRIDE - 1
    n_slc = seq // SLC_BLOCK
    assert n_slc <= HEAD_DIM
    ovT = _overlap_T(n_cmp + 1, n_cmp, n_slc)
    row = lambda a: a.reshape(1, -1)
    r_idx = lax.broadcasted_iota(jnp.int32, (tq, tq), 0)
    c_idx = lax.broadcasted_iota(jnp.int32, (tq, tq), 1)
    masks = jnp.stack([jnp.zeros((tq, tq), F32), jnp.where(r_idx <= c_idx, 0.0, MASKED),
                       jnp.where(r_idx > c_idx, 0.0, MASKED), jnp.full((tq, tq), MASKED, F32)])
    v_tail = jnp.zeros((batch, NSA_KV_HEADS, 2 * nq, BF16_SUBLANES, tq), BF16).at[:, :, :, 0, :].set(1.0)

    x2 = x.reshape(n, D_MODEL)
    for l in range(depth):
        w_in_pad = jnp.pad(w_in[l], ((0, 0), (0, IN_COLS_PAD - IN_COLS))).astype(BF16)
        z = _in_proj(x2, row(norm_mix_g[l]), w_in_pad)

        bs_full = jnp.repeat(sg_b[l].T, HEAD_DIM, axis=1)
        a_n = _sgu(z, row(sg_ln_g[l]), row(sg_ln_b[l]), sg_w[l], bs_full, row(mix_out_g[l, 0]))

        kx, v = _kv_prep(z, row(k_norm_g[l, 1]), row(k_norm_g[l, 2]), batch, seq)
        tiles = lambda a: a.reshape(batch, NSA_KV_HEADS, 2 * nq, tq, a.shape[-1])
        kx = tiles(kx)
        vT = jnp.concatenate([tiles(v).transpose(0, 1, 2, 4, 3), v_tail], axis=3)

        kc = _compress(_compress_input(z, COL_KV, batch, seq), row(cmp_pos[l, 0]), cmp_w1[l, 0],
                       row(cmp_b1[l, 0]), cmp_w2[l, 0], row(cmp_b2[l, 0]), row(k_norm_g[l, 0]), True)
        vc = _compress(_compress_input(z, COL_KV + KV_WIDTH, batch, seq), row(cmp_pos[l, 1]), cmp_w1[l, 1],
                       row(cmp_b1[l, 1]), cmp_w2[l, 1], row(cmp_b2[l, 1]), row(k_norm_g[l, 0]), False)
        vcT = vc.transpose(0, 2, 1)

        qT = z[:, COL_Q:COL_Q + NSA_WIDTH].reshape(batch, seq, NSA_WIDTH).transpose(0, 2, 1)
        glT = z[:, COL_GATE:IN_COLS].reshape(batch, seq, NSA_KV_HEADS, NSA_GROUP * N_BRANCH).transpose(0, 2, 3, 1)
        gq = (q_norm_g[l] * (HEAD_DIM ** -0.5 * LOG2E)).reshape(HEAD_DIM, 1)
        oT = _attention(qT, glT, kc, vcT, ovT, kx, vT, masks, gq, n_cmp)
        b = oT.transpose(0, 2, 1).reshape(n, NSA_WIDTH)

        x2 = _out_proj(a_n, b, x2, row(mix_out_g[l, 1]), w_out[l].astype(BF16))

        mk, mv = _mem_kv(mem, row(mem_kv_norm_g[l]), w_mkv[l].astype(BF16), row(mem_k_norm_g[l]))
        x2 = _mem_attn(x2, row(norm_mem_g[l]), w_mq[l].astype(BF16), row(mem_q_norm_g[l]), mk, mv,
                       w_mo[l].astype(BF16), seq)

        x2 = _ffn(x2, row(norm_ffn_g[l]), w_ff1[l].astype(BF16), w_ff2[l].astype(BF16))
    return x2.reshape(batch, seq, D_MODEL)
```

```python
import functools

import jax
import jax.numpy as jnp
from jax import lax
from jax.experimental import pallas as pl
from jax.experimental.pallas import tpu as pltpu

F32 = jnp.float32
BF16 = jnp.bfloat16

D_MODEL = 1024
HEAD_DIM = 64
SG_WIDTH = 512
SG_GROUPS = 8
SG_CHUNK = 128
NSA_WIDTH = 512
NSA_KV_HEADS = 2
NSA_GROUP = 4
KV_WIDTH = NSA_KV_HEADS * HEAD_DIM
N_BRANCH = 3
CMP_BLOCK = 32
CMP_STRIDE = 16
CMP_HIDDEN = 256
SLC_BLOCK = 64
SLC_TOPK = 16
WINDOW = 512
FORCE_SCORE = 1e4
MEM_HEADS = 4
MEM_HEAD_DIM = 128
D_FF = 4 * D_MODEL
EPS = 1e-6

COL_Q = 2 * SG_WIDTH
COL_KV = COL_Q + NSA_WIDTH
COL_GATE = COL_KV + 6 * KV_WIDTH
KV_COLS = 6 * KV_WIDTH
GATES_PER_HEAD = NSA_GROUP * N_BRANCH
GATE_ROWS = 16

MASKED = -1e30
LOG2E = 1.4426950408889634
BF16_SUBLANES = 16
V_ROWS = HEAD_DIM + BF16_SUBLANES
ATTN_TILE = 256
VMEM_LIMIT = 48 * 1024 * 1024

NT_DIMS = (((1,), (1,)), ((), ()))
TN_DIMS = (((0,), (0,)), ((), ()))


def _cparams(*sem):
    return pltpu.CompilerParams(dimension_semantics=sem, vmem_limit_bytes=VMEM_LIMIT)


def _rms(x, g):
    return x * lax.rsqrt(jnp.mean(x * x, axis=-1, keepdims=True) + EPS) * g


def _const_spec(shape):
    return pl.BlockSpec(shape, lambda *_: (0,) * len(shape))


def _transpose_bf16(x, eye):
    return lax.dot_general(eye, x, NT_DIMS, preferred_element_type=F32).astype(BF16)


def _in_proj_kernel(x_ref, g_ref, wuv_ref, wkv_ref, wqT_ref, wgT_ref, uv_ref, kv_ref, qT_ref, gT_ref):
    h = _rms(x_ref[...], g_ref[...]).astype(BF16)
    step = 512
    for c0 in range(0, 2 * SG_WIDTH, step):
        uv_ref[:, c0:c0 + step] = jnp.dot(h, wuv_ref[:, c0:c0 + step], preferred_element_type=F32)
    kv_ref[...] = jnp.dot(h, wkv_ref[...], preferred_element_type=F32)
    qT_ref[...] = lax.dot_general(wqT_ref[...], h, NT_DIMS, preferred_element_type=F32)
    gT_ref[...] = lax.dot_general(wgT_ref[...], h, NT_DIMS, preferred_element_type=F32)


def _in_proj(x2, g, w_uv, w_kv, w_qT, w_gT, batch, seq):
    n = x2.shape[0]
    tm = 512
    per_b = seq // tm
    g_rows = w_gT.shape[0]

    def rows(width):
        return pl.BlockSpec((tm, width), lambda i: (i, 0))

    def cols(height):
        return pl.BlockSpec((None, height, tm), lambda i: (i // per_b, 0, i % per_b))

    return pl.pallas_call(
        _in_proj_kernel,
        out_shape=(jax.ShapeDtypeStruct((n, 2 * SG_WIDTH), F32), jax.ShapeDtypeStruct((n, KV_COLS), F32),
                   jax.ShapeDtypeStruct((batch, NSA_WIDTH, seq), F32),
                   jax.ShapeDtypeStruct((batch, g_rows, seq), F32)),
        grid=(n // tm,),
        in_specs=[rows(D_MODEL), _const_spec((1, D_MODEL)),
                  _const_spec((D_MODEL, 2 * SG_WIDTH)), _const_spec((D_MODEL, KV_COLS)),
                  _const_spec((NSA_WIDTH, D_MODEL)), _const_spec((g_rows, D_MODEL))],
        out_specs=(rows(2 * SG_WIDTH), rows(KV_COLS), cols(NSA_WIDTH), cols(g_rows)),
        compiler_params=_cparams("parallel"),
        name="in_proj",
    )(x2, g, w_uv, w_kv, w_qT, w_gT)


def _sgu_kernel(u_ref, v_ref, lng_ref, lnb_ref, w_ref, bs_ref, og_ref, a_ref, *, chunks):
    row = lax.broadcasted_iota(jnp.int32, (SG_CHUNK, SG_CHUNK), 0)
    col = lax.broadcasted_iota(jnp.int32, (SG_CHUNK, SG_CHUNK), 1)
    causal = col <= row
    w = [jnp.where(causal, w_ref[g], 0.0).astype(BF16) for g in range(SG_GROUPS)]
    for c in range(chunks):
        rows = slice(c * SG_CHUNK, (c + 1) * SG_CHUNK)
        u = jax.nn.gelu(u_ref[rows, :])
        v = jax.nn.gelu(v_ref[rows, :])
        mu = jnp.mean(v, axis=-1, keepdims=True)
        vc = v - mu
        vn = vc * lax.rsqrt(jnp.mean(vc * vc, axis=-1, keepdims=True) + EPS) * lng_ref[...] + lnb_ref[...]
        vb = vn.astype(BF16)
        s = jnp.concatenate(
            [jnp.dot(w[g], vb[:, g * HEAD_DIM:(g + 1) * HEAD_DIM], preferred_element_type=F32)
             for g in range(SG_GROUPS)], axis=-1) + bs_ref[...]
        a_ref[rows, :] = _rms(u * s, og_ref[...]).astype(BF16)


def _sgu(z, ln_g, ln_b, w_s, bs_full, out_g):
    n = z.shape[0]
    chunks = 4
    tm = chunks * SG_CHUNK
    return pl.pallas_call(
        functools.partial(_sgu_kernel, chunks=chunks),
        out_shape=jax.ShapeDtypeStruct((n, SG_WIDTH), BF16),
        grid=(n // tm,),
        in_specs=[pl.BlockSpec((tm, SG_WIDTH), lambda i: (i, 0)),
                  pl.BlockSpec((tm, SG_WIDTH), lambda i: (i, 1)),
                  _const_spec((1, SG_WIDTH)), _const_spec((1, SG_WIDTH)),
                  _const_spec((SG_GROUPS, SG_CHUNK, SG_CHUNK)),
                  _const_spec((SG_CHUNK, SG_WIDTH)), _const_spec((1, SG_WIDTH))],
        out_specs=pl.BlockSpec((tm, SG_WIDTH), lambda i: (i, 0)),
        compiler_params=_cparams("parallel"),
        name="sgu",
    )(z, z, ln_g, ln_b, w_s, bs_full, out_g)


def _kv_prep_kernel(kv_ref, gs_ref, gw_ref, eye_ref, kx_ref, kw_ref, vT_ref, kc_ref, vc_ref, *, tm, seq):
    kv = kv_ref[...]
    t0 = (pl.program_id(0) * tm) % seq
    blk = (t0 + lax.broadcasted_iota(jnp.int32, (tm, SLC_BLOCK), 0)) // SLC_BLOCK
    onehot = jnp.where(blk == lax.broadcasted_iota(jnp.int32, (tm, SLC_BLOCK), 1), 1.0, 0.0).astype(BF16)
    tail = jnp.where(lax.broadcasted_iota(jnp.int32, (BF16_SUBLANES, tm), 0) == 0, 1.0, 0.0).astype(BF16)
    for h in range(NSA_KV_HEADS):
        def part(idx):
            c0 = idx * KV_WIDTH + h * HEAD_DIM
            return kv[:, c0:c0 + HEAD_DIM]
        kc_ref[h] = part(0)
        vc_ref[h] = part(1)
        kx_ref[h, :, :HEAD_DIM] = _rms(part(2), gs_ref[...]).astype(BF16)
        kx_ref[h, :, HEAD_DIM:] = onehot
        kw_ref[h] = _rms(part(4), gw_ref[...]).astype(BF16)
        for branch, idx in ((0, 3), (1, 5)):
            vT_ref[h, branch, :HEAD_DIM, :] = _transpose_bf16(part(idx).astype(BF16), eye_ref[...])
            vT_ref[h, branch, HEAD_DIM:, :] = tail


def _kv_prep(z, g_slc, g_win, eye, batch, seq):
    n = z.shape[0]
    tm = ATTN_TILE
    per_b = seq // tm
    hd = NSA_KV_HEADS

    def rows_spec(width):
        return pl.BlockSpec((None, hd, tm, width), lambda i: (i // per_b, 0, i % per_b, 0))

    def rows_shape(width, dtype):
        return jax.ShapeDtypeStruct((batch, hd, seq, width), dtype)

    return pl.pallas_call(
        functools.partial(_kv_prep_kernel, tm=tm, seq=seq),
        out_shape=(rows_shape(2 * HEAD_DIM, BF16), rows_shape(HEAD_DIM, BF16),
                   jax.ShapeDtypeStruct((batch, hd, 2, per_b, V_ROWS, tm), BF16),
                   rows_shape(HEAD_DIM, F32), rows_shape(HEAD_DIM, F32)),
        grid=(n // tm,),
        in_specs=[pl.BlockSpec((tm, KV_COLS), lambda i: (i, 0)),
                  _const_spec((1, HEAD_DIM)), _const_spec((1, HEAD_DIM)), _const_spec((HEAD_DIM, HEAD_DIM))],
        out_specs=(rows_spec(2 * HEAD_DIM), rows_spec(HEAD_DIM),
                   pl.BlockSpec((None, hd, 2, None, V_ROWS, tm), lambda i: (i // per_b, 0, 0, i % per_b, 0, 0)),
                   rows_spec(HEAD_DIM), rows_spec(HEAD_DIM)),
        compiler_params=_cparams("parallel"),
        name="kv_prep",
    )(z, g_slc, g_win, eye)


def _cmp_kernel(c_ref, pos_ref, w1_ref, b1_ref, w2_ref, b2_ref, g_ref, o_ref, *, norm):
    hi = lax.Precision.HIGHEST
    c = c_ref[...]
    nb, half = c.shape
    first = jnp.dot(c, w1_ref[:half, :], preferred_element_type=F32, precision=hi)
    second = jnp.dot(c, w1_ref[half:, :], preferred_element_type=F32, precision=hi)
    pos = jnp.dot(pos_ref[...], w1_ref[...], preferred_element_type=F32, precision=hi)[0:1, :]
    h1 = jax.nn.gelu(first + pltpu.roll(second, nb - 1, 0) + pos + b1_ref[...])
    y = jnp.dot(h1, w2_ref[...], preferred_element_type=F32, precision=hi) + b2_ref[...]
    if norm:
        y = _rms(y, g_ref[...])
    o_ref[...] = y


def _compress(chunks, pos8, w1, b1, w2, b2, g, norm):
    bh, nb, half = chunks.shape
    return pl.pallas_call(
        functools.partial(_cmp_kernel, norm=norm),
        out_shape=jax.ShapeDtypeStruct((bh, nb, HEAD_DIM), F32),
        grid=(bh,),
        in_specs=[pl.BlockSpec((None, nb, half), lambda i: (i, 0, 0)),
                  _const_spec((8, 2 * half)), _const_spec((2 * half, CMP_HIDDEN)), _const_spec((1, CMP_HIDDEN)),
                  _const_spec((CMP_HIDDEN, HEAD_DIM)), _const_spec((1, HEAD_DIM)), _const_spec((1, HEAD_DIM))],
        out_specs=pl.BlockSpec((None, nb, HEAD_DIM), lambda i: (i, 0, 0)),
        compiler_params=_cparams("parallel"),
        name="compress_k" if norm else "compress_v",
    )(chunks, pos8, w1, b1, w2, b2, g)


def _attn_kernel(qT_ref, glT_ref, kc_ref, vcT_ref, ovT_ref, kx_ref, kw_ref, vT_ref, mask_ref, gq_ref,
                 o_ref, qn_sc, qx_sc, s_a, mo_a, mn_a, s_b, mo_b, mn_b, s_c, score_sc, rank_sc, rows_sc, mrun_sc, acc_sc,
                 *, tq, n_cmp, nq):
    i = pl.program_id(2)
    d = HEAD_DIM
    G = NSA_GROUP

    def lanes(g):
        return slice(g * tq, (g + 1) * tq)

    for g in range(G):
        qg = qT_ref[g * d:(g + 1) * d, :]
        qg = qg * lax.rsqrt(jnp.mean(qg * qg, axis=0, keepdims=True) + EPS) * gq_ref[...]
        qn_sc[:, lanes(g)] = qg.astype(BF16)

    gates = jax.nn.sigmoid(glT_ref[...])

    mrun_sc[...] = jnp.full(mrun_sc.shape, -jnp.inf, F32)
    acc_sc[...] = jnp.zeros(acc_sc.shape, F32)

    def scores(branch, k_ref, k_idx, q_sc, mask, s_buf, mo_buf, mn_buf):
        for g in range(G):
            s = jnp.dot(k_ref[k_idx], q_sc[:, lanes(g)], preferred_element_type=F32)
            if mask is not None:
                s = s + mask_ref[mask]
            s_buf[g] = s
            m_old = mrun_sc[branch, g]
            m_new = jnp.maximum(m_old, jnp.max(s, axis=0, keepdims=True))
            mo_buf[g] = m_old
            mn_buf[g] = m_new
            mrun_sc[branch, g] = m_new

    def accumulate(branch, v_idx, s_buf, mo_buf, mn_buf):
        for g in range(G):
            m_new = mn_buf[g]
            alpha = jnp.exp2(mo_buf[g] - m_new)
            p = jnp.exp2(s_buf[g] - m_new).astype(BF16)
            acc_sc[branch, g] = alpha * acc_sc[branch, g] + jnp.dot(vT_ref[branch, v_idx], p,
                                                                    preferred_element_type=F32)

    buf_a = (s_a, mo_a, mn_a)
    buf_b = (s_b, mo_b, mn_b)

    w_tiles = [jnp.maximum(i - 2, 0), jnp.maximum(i - 1, 0), i]
    w_masks = [jnp.where(i >= 2, 2, 3), jnp.where(i >= 1, 0, 3), 1]
    scores(1, kw_ref, w_tiles[0], qn_sc, w_masks[0], *buf_a)
    scores(1, kw_ref, w_tiles[1], qn_sc, w_masks[1], *buf_b)
    accumulate(1, w_tiles[0], *buf_a)
    scores(1, kw_ref, w_tiles[2], qn_sc, w_masks[2], *buf_a)
    accumulate(1, w_tiles[1], *buf_b)
    accumulate(1, w_tiles[2], *buf_a)

    ncp = kc_ref.shape[0]
    t_idx = i * tq + lax.broadcasted_iota(jnp.int32, (ncp, tq), 1)
    n_idx = lax.broadcasted_iota(jnp.int32, (ncp, tq), 0)
    c_mask = jnp.where((n_idx * CMP_STRIDE + (CMP_BLOCK - 1) <= t_idx) & (n_idx < n_cmp), 0.0, MASKED)
    sees_any = jnp.where(t_idx[0:1, :] >= CMP_BLOCK - 1, 1.0, 0.0)
    kc = kc_ref[...].astype(BF16)
    c_max = []
    for g in range(G):
        s = jnp.dot(kc, qn_sc[:, lanes(g)], preferred_element_type=F32) + c_mask
        s_c[g] = s
        c_max.append(jnp.max(s, axis=0, keepdims=True))
    vcT = vcT_ref[...].astype(BF16)
    psum = jnp.zeros((ncp, tq), F32)
    for g in range(G):
        e = jnp.exp2(s_c[g] - c_max[g])
        p = e * (sees_any / jnp.sum(e, axis=0, keepdims=True))
        psum = psum + p
        o_cmp = jnp.dot(vcT, p.astype(BF16), preferred_element_type=F32)
        o_ref[g * d:(g + 1) * d, :] = gates[g * 3:g * 3 + 1, :] * o_cmp

    p_hi = psum.astype(BF16)
    p_lo = (psum - p_hi.astype(F32)).astype(BF16)
    imp = (jnp.dot(ovT_ref[...], p_hi, preferred_element_type=F32)
           + jnp.dot(ovT_ref[...], p_lo, preferred_element_type=F32))
    n_slc = imp.shape[0]
    j_idx = lax.broadcasted_iota(jnp.int32, (n_slc, tq), 0)
    t_blk = (i * tq + lax.broadcasted_iota(jnp.int32, (n_slc, tq), 1)) // SLC_BLOCK
    forced = (j_idx == 0) | (j_idx == t_blk) | (j_idx == t_blk - 1)
    score = jnp.where(forced, FORCE_SCORE, jnp.where(j_idx <= t_blk, imp, -FORCE_SCORE))
    for k in range(n_slc):
        rows_sc[k] = score[k:k + 1, :]
    score_sc[...] = score
    rank_sc[...] = jnp.zeros(rank_sc.shape, F32)
    sub, k_chunk = 8, 16
    blocks_per_tile = tq // SLC_BLOCK
    row_in_group = lax.broadcasted_iota(jnp.int32, (sub, tq), 0)
    for k0 in range(0, n_slc, k_chunk):
        @pl.when(i * blocks_per_tile + blocks_per_tile > k0)
        def _(k0=k0):
            rks = [jnp.broadcast_to(rows_sc[k], (sub, tq)) for k in range(k0, k0 + k_chunk)]
            for r0 in range(0, n_slc, sub):
                sg = score_sc[r0:r0 + sub, :]
                acc = rank_sc[r0:r0 + sub, :]
                for k in range(k0, k0 + k_chunk):
                    rk = rks[k - k0]
                    ge = jnp.where(rk >= sg, 1.0, 0.0)
                    gt = jnp.where(rk > sg, 1.0, 0.0)
                    if r0 > k:
                        before = ge
                    elif r0 + sub - 1 < k:
                        before = gt
                    else:
                        before = jnp.where(row_in_group > k - r0, ge, gt)
                    acc = acc + before
                rank_sc[r0:r0 + sub, :] = acc
    n_sel = min(SLC_TOPK, n_slc)
    bias = jnp.where(rank_sc[...] < n_sel, 0.0, MASKED).astype(BF16)
    if n_slc < d:
        bias = jnp.concatenate([bias, jnp.zeros((d - n_slc, tq), BF16)], axis=0)
    qx_sc[0:d, :] = qn_sc[...]
    for g in range(G):
        qx_sc[d:2 * d, lanes(g)] = bias

    def slc_mask(j):
        return jnp.where(j == i, 1, jnp.where(j > i, 3, 0))

    def slc_tile(j):
        return jnp.minimum(j, i)

    scores(0, kx_ref, 0, qx_sc, slc_mask(0), *buf_a)
    n_loop = jnp.maximum(i - 1, 0) // 2

    def pair(kk, carry):
        j0 = 2 * kk
        scores(0, kx_ref, j0 + 1, qx_sc, None, *buf_b)
        accumulate(0, j0, *buf_a)
        scores(0, kx_ref, j0 + 2, qx_sc, None, *buf_a)
        accumulate(0, j0 + 1, *buf_b)
        return carry

    lax.fori_loop(0, n_loop, pair, 0)
    j0 = 2 * n_loop
    scores(0, kx_ref, slc_tile(j0 + 1), qx_sc, slc_mask(j0 + 1), *buf_b)
    accumulate(0, j0, *buf_a)
    scores(0, kx_ref, slc_tile(j0 + 2), qx_sc, slc_mask(j0 + 2), *buf_a)
    accumulate(0, slc_tile(j0 + 1), *buf_b)
    accumulate(0, slc_tile(j0 + 2), *buf_a)

    for branch in range(2):
        for g in range(G):
            acc = acc_sc[branch, g]
            o = acc[0:d, :] * (1.0 / acc[d:d + 1, :])
            gate = gates[g * 3 + branch + 1:g * 3 + branch + 2, :]
            o_ref[g * d:(g + 1) * d, :] += gate * o


def _attention(qT, glT, kc, vcT, ovT, kx, kw, vT, masks, gq, n_cmp):
    batch, _, seq = qT.shape
    tq = ATTN_TILE
    nq = seq // tq
    hd = NSA_KV_HEADS
    gw = NSA_GROUP * HEAD_DIM
    ncp = kc.shape[1]
    n_slc = ovT.shape[0]

    def per_head(shape):
        return pl.BlockSpec((None, None) + shape, lambda b, h, i: (b, h) + (0,) * len(shape))

    stage = [pltpu.VMEM((NSA_GROUP, tq, tq), F32),
             pltpu.VMEM((NSA_GROUP, 1, tq), F32),
             pltpu.VMEM((NSA_GROUP, 1, tq), F32)]
    return pl.pallas_call(
        functools.partial(_attn_kernel, tq=tq, n_cmp=n_cmp, nq=nq),
        out_shape=jax.ShapeDtypeStruct((batch, NSA_WIDTH, seq), F32),
        grid=(batch, hd, nq),
        in_specs=[pl.BlockSpec((None, gw, tq), lambda b, h, i: (b, h, i)),
                  pl.BlockSpec((None, GATE_ROWS, tq), lambda b, h, i: (b, h, i)),
                  pl.BlockSpec((None, ncp, HEAD_DIM), lambda b, h, i: (b * hd + h, 0, 0)),
                  pl.BlockSpec((None, HEAD_DIM, ncp), lambda b, h, i: (b * hd + h, 0, 0)),
                  _const_spec((n_slc, ncp)),
                  per_head((nq, tq, 2 * HEAD_DIM)), per_head((nq, tq, HEAD_DIM)),
                  per_head((2, nq, V_ROWS, tq)),
                  _const_spec((4, tq, tq)),
                  _const_spec((HEAD_DIM, 1))],
        out_specs=pl.BlockSpec((None, gw, tq), lambda b, h, i: (b, h, i)),
        scratch_shapes=[pltpu.VMEM((HEAD_DIM, NSA_GROUP * tq), BF16),
                        pltpu.VMEM((2 * HEAD_DIM, NSA_GROUP * tq), BF16)] + stage + stage + [
            pltpu.VMEM((NSA_GROUP, ncp, tq), F32),
            pltpu.VMEM((n_slc, tq), F32), pltpu.VMEM((n_slc, tq), F32), pltpu.VMEM((n_slc, 1, tq), F32),
            pltpu.VMEM((2, NSA_GROUP, 1, tq), F32),
            pltpu.VMEM((2, NSA_GROUP, V_ROWS, tq), F32)],
        compiler_params=_cparams("parallel", "parallel", "arbitrary"),
        name="nsa_attention",
    )(qT, glT, kc, vcT, ovT, kx, kw, vT, masks, gq)


def _out_proj_kernel(a_ref, bT_ref, x_ref, g_ref, w_ref, o_ref):
    bT = bT_ref[...]
    bn = (bT * lax.rsqrt(jnp.mean(bT * bT, axis=0, keepdims=True) + EPS) * g_ref[...]).astype(BF16)
    y = jnp.dot(a_ref[...], w_ref[:SG_WIDTH, :], preferred_element_type=F32)
    y = y + lax.dot_general(bn, w_ref[SG_WIDTH:, :], TN_DIMS, preferred_element_type=F32)
    o_ref[...] = x_ref[...] + y


def _out_proj(a_n, bT, x2, g_b, w_out, seq):
    n = x2.shape[0]
    tm = 512
    per_b = seq // tm
    return pl.pallas_call(
        _out_proj_kernel,
        out_shape=jax.ShapeDtypeStruct((n, D_MODEL), F32),
        grid=(n // tm,),
        in_specs=[pl.BlockSpec((tm, SG_WIDTH), lambda i: (i, 0)),
                  pl.BlockSpec((None, NSA_WIDTH, tm), lambda i: (i // per_b, 0, i % per_b)),
                  pl.BlockSpec((tm, D_MODEL), lambda i: (i, 0)),
                  _const_spec((NSA_WIDTH, 1)), _const_spec((D_MODEL, D_MODEL))],
        out_specs=pl.BlockSpec((tm, D_MODEL), lambda i: (i, 0)),
        compiler_params=_cparams("parallel"),
        name="out_proj",
    )(a_n, bT, x2, g_b, w_out)


def _mem_kv_kernel(mem_ref, g_ref, w_ref, kg_ref, k_ref, v_ref):
    mn = _rms(mem_ref[...], g_ref[...]).astype(BF16)
    kv = jnp.dot(mn, w_ref[...], preferred_element_type=F32)
    width = MEM_HEADS * MEM_HEAD_DIM
    for hh in range(MEM_HEADS):
        cols = slice(hh * MEM_HEAD_DIM, (hh + 1) * MEM_HEAD_DIM)
        k_ref[:, cols] = _rms(kv[:, cols], kg_ref[...]).astype(BF16)
    v_ref[...] = kv[:, width:].astype(BF16)


def _mem_kv(mem, g, w_mkv, k_g):
    batch, m_len, _ = mem.shape
    width = MEM_HEADS * MEM_HEAD_DIM
    spec = pl.BlockSpec((None, m_len, width), lambda b: (b, 0, 0))
    return pl.pallas_call(
        _mem_kv_kernel,
        out_shape=(jax.ShapeDtypeStruct((batch, m_len, width), BF16),) * 2,
        grid=(batch,),
        in_specs=[pl.BlockSpec((None, m_len, D_MODEL), lambda b: (b, 0, 0)),
                  _const_spec((1, D_MODEL)), _const_spec((D_MODEL, 2 * width)),
                  _const_spec((1, MEM_HEAD_DIM))],
        out_specs=(spec, spec),
        compiler_params=_cparams("parallel"),
        name="mem_kv",
    )(mem, g, w_mkv, k_g)


def _mem_attn_kernel(x_ref, g_ref, wq_ref, qg_ref, k_ref, v_ref, wo_ref, o_ref):
    x = x_ref[...]
    h = _rms(x, g_ref[...]).astype(BF16)
    q = jnp.dot(h, wq_ref[...], preferred_element_type=F32)
    outs = []
    for hh in range(MEM_HEADS):
        cols = slice(hh * MEM_HEAD_DIM, (hh + 1) * MEM_HEAD_DIM)
        qh = (_rms(q[:, cols], qg_ref[...]) * MEM_HEAD_DIM ** -0.5).astype(BF16)
        s = lax.dot_general(qh, k_ref[:, cols], NT_DIMS, preferred_element_type=F32)
        m = jnp.max(s, axis=-1, keepdims=True)
        e = jnp.exp(s - m)
        p = e / jnp.sum(e, axis=-1, keepdims=True)
        outs.append(jnp.dot(p.astype(BF16), v_ref[:, cols], preferred_element_type=F32))
    o = jnp.concatenate(outs, axis=-1).astype(BF16)
    o_ref[...] = x + jnp.dot(o, wo_ref[...], preferred_element_type=F32)


def _mem_attn(x2, g, w_mq, q_g, k, v, w_mo, seq):
    n = x2.shape[0]
    tm = 256
    per_b = seq // tm
    m_len = k.shape[1]
    width = MEM_HEADS * MEM_HEAD_DIM
    kv_spec = pl.BlockSpec((None, m_len, width), lambda i: (i // per_b, 0, 0))
    return pl.pallas_call(
        _mem_attn_kernel,
        out_shape=jax.ShapeDtypeStruct((n, D_MODEL), F32),
        grid=(n // tm,),
        in_specs=[pl.BlockSpec((tm, D_MODEL), lambda i: (i, 0)),
                  _const_spec((1, D_MODEL)), _const_spec((D_MODEL, width)), _const_spec((1, MEM_HEAD_DIM)),
                  kv_spec, kv_spec, _const_spec((width, D_MODEL))],
        out_specs=pl.BlockSpec((tm, D_MODEL), lambda i: (i, 0)),
        compiler_params=_cparams("parallel"),
        name="mem_attn",
    )(x2, g, w_mq, q_g, k, v, w_mo)


def _ffn_kernel(x_ref, g_ref, w1_ref, w2_ref, o_ref):
    x = x_ref[...]
    h = _rms(x, g_ref[...]).astype(BF16)
    step = 1024
    acc = x
    for c0 in range(0, D_FF, step):
        a = jnp.maximum(jnp.dot(h, w1_ref[:, c0:c0 + step], preferred_element_type=F32), 0.0)
        acc = acc + jnp.dot((a * a).astype(BF16), w2_ref[c0:c0 + step, :], preferred_element_type=F32)
    o_ref[...] = acc


def _ffn(x2, g, w1, w2):
    n = x2.shape[0]
    tm = 512
    return pl.pallas_call(
        _ffn_kernel,
        out_shape=jax.ShapeDtypeStruct((n, D_MODEL), F32),
        grid=(n // tm,),
        in_specs=[pl.BlockSpec((tm, D_MODEL), lambda i: (i, 0)),
                  _const_spec((1, D_MODEL)),
                  pl.BlockSpec((D_MODEL, D_FF), lambda i: (0, 0), pipeline_mode=pl.Buffered(1)),
                  pl.BlockSpec((D_FF, D_MODEL), lambda i: (0, 0), pipeline_mode=pl.Buffered(1))],
        out_specs=pl.BlockSpec((tm, D_MODEL), lambda i: (i, 0)),
        compiler_params=_cparams("parallel"),
        name="ffn",
    )(x2, g, w1, w2)


def _overlap_T(n_cmp_pad, n_cmp, n_slc):
    cs = jnp.arange(n_cmp_pad) * CMP_STRIDE
    ss = jnp.arange(n_slc) * SLC_BLOCK
    ov = jnp.clip(jnp.minimum(cs[None, :] + CMP_BLOCK, ss[:, None] + SLC_BLOCK)
                  - jnp.maximum(cs[None, :], ss[:, None]), 0, None).astype(F32) / CMP_BLOCK
    ov = jnp.where(jnp.arange(n_cmp_pad)[None, :] < n_cmp, ov, 0.0)
    return ov.astype(BF16)


def kernel(x, mem, norm_mix_g, w_in, sg_ln_g, sg_ln_b, sg_w, sg_b, q_norm_g, k_norm_g, cmp_pos, cmp_w1, cmp_b1, cmp_w2, cmp_b2, mix_out_g, w_out, norm_mem_g, mem_kv_norm_g, w_mq, w_mkv, mem_q_norm_g, mem_k_norm_g, w_mo, norm_ffn_g, w_ff1, w_ff2):
    batch, seq, _ = x.shape
    depth = w_in.shape[0]
    n = batch * seq
    tq = ATTN_TILE
    assert WINDOW == 2 * tq and seq % 512 == 0
    n_chunk = seq // CMP_STRIDE
    n_cmp = n_chunk - 1
    n_slc = seq // SLC_BLOCK
    assert n_slc <= HEAD_DIM
    ovT = _overlap_T(n_chunk, n_cmp, n_slc)
    eye = jnp.eye(HEAD_DIM, dtype=BF16)
    row = lambda a: a.reshape(1, -1)
    r_idx = lax.broadcasted_iota(jnp.int32, (tq, tq), 0)
    c_idx = lax.broadcasted_iota(jnp.int32, (tq, tq), 1)
    masks = jnp.stack([jnp.zeros((tq, tq), F32), jnp.where(r_idx <= c_idx, 0.0, MASKED),
                       jnp.where(r_idx > c_idx, 0.0, MASKED), jnp.full((tq, tq), MASKED, F32)])

    x2 = x.reshape(n, D_MODEL)
    for l in range(depth):
        w = w_in[l].astype(BF16)
        w_gT = jnp.pad(w[:, COL_GATE:].T.reshape(NSA_KV_HEADS, GATES_PER_HEAD, D_MODEL),
                       ((0, 0), (0, GATE_ROWS - GATES_PER_HEAD), (0, 0))).reshape(NSA_KV_HEADS * GATE_ROWS, D_MODEL)
        z_uv, z_kv, qT, glT = _in_proj(x2, row(norm_mix_g[l]), w[:, :COL_Q], w[:, COL_KV:COL_GATE],
                                       w[:, COL_Q:COL_KV].T, w_gT, batch, seq)

        bs_full = jnp.repeat(sg_b[l].T, HEAD_DIM, axis=1)
        a_n = _sgu(z_uv, row(sg_ln_g[l]), row(sg_ln_b[l]), sg_w[l], bs_full, row(mix_out_g[l, 0]))

        kx, kw, vT, k_cmp, v_cmp = _kv_prep(z_kv, row(k_norm_g[l, 1]), row(k_norm_g[l, 2]), eye, batch, seq)
        tiles = lambda a: a.reshape(batch, NSA_KV_HEADS, seq // tq, tq, a.shape[-1])
        chunks = lambda a: a.reshape(batch * NSA_KV_HEADS, n_chunk, CMP_STRIDE * HEAD_DIM)
        pos8 = lambda p: jnp.broadcast_to(p.reshape(1, -1), (8, CMP_BLOCK * HEAD_DIM))
        kc = _compress(chunks(k_cmp), pos8(cmp_pos[l, 0]), cmp_w1[l, 0], row(cmp_b1[l, 0]), cmp_w2[l, 0],
                       row(cmp_b2[l, 0]), row(k_norm_g[l, 0]), True)
        vc = _compress(chunks(v_cmp), pos8(cmp_pos[l, 1]), cmp_w1[l, 1], row(cmp_b1[l, 1]), cmp_w2[l, 1],
                       row(cmp_b2[l, 1]), row(k_norm_g[l, 0]), False)
        vcT = vc.transpose(0, 2, 1)

        gq = (q_norm_g[l] * (HEAD_DIM ** -0.5 * LOG2E)).reshape(HEAD_DIM, 1)
        oT = _attention(qT, glT, kc, vcT, ovT, tiles(kx), tiles(kw), vT, masks, gq, n_cmp)

        x2 = _out_proj(a_n, oT, x2, mix_out_g[l, 1].reshape(NSA_WIDTH, 1), w_out[l].astype(BF16), seq)

        mk, mv = _mem_kv(mem, row(mem_kv_norm_g[l]), w_mkv[l].astype(BF16), row(mem_k_norm_g[l]))
        x2 = _mem_attn(x2, row(norm_mem_g[l]), w_mq[l].astype(BF16), row(mem_q_norm_g[l]), mk, mv,
                       w_mo[l].astype(BF16), seq)

        x2 = _ffn(x2, row(norm_ffn_g[l]), w_ff1[l].astype(BF16), w_ff2[l].astype(BF16))
    return x2.reshape(batch, seq, D_MODEL)
```

```python
import functools

import jax
import jax.numpy as jnp
from jax import lax
from jax.experimental import pallas as pl
from jax.experimental.pallas import tpu as pltpu

F32 = jnp.float32
BF16 = jnp.bfloat16

D_MODEL = 1024
HEAD_DIM = 64
SG_WIDTH = 512
SG_GROUPS = 8
SG_CHUNK = 128
NSA_WIDTH = 512
NSA_KV_HEADS = 2
NSA_GROUP = 4
KV_WIDTH = NSA_KV_HEADS * HEAD_DIM
N_BRANCH = 3
CMP_BLOCK = 32
CMP_STRIDE = 16
CMP_HIDDEN = 256
SLC_BLOCK = 64
SLC_TOPK = 16
WINDOW = 512
FORCE_SCORE = 1e4
MEM_HEADS = 4
MEM_HEAD_DIM = 128
D_FF = 4 * D_MODEL
EPS = 1e-6

COL_Q = 2 * SG_WIDTH
COL_KV = COL_Q + NSA_WIDTH
COL_GATE = COL_KV + 6 * KV_WIDTH
KV_COLS = 6 * KV_WIDTH
GATES_PER_HEAD = NSA_GROUP * N_BRANCH
GATE_ROWS = 16

MASKED = -1e30
LOG2E = 1.4426950408889634
MAX_SCORE_SPAN = 100.0
BF16_SUBLANES = 16
V_ROWS = HEAD_DIM + BF16_SUBLANES
ATTN_TILE = 256
VMEM_LIMIT = 48 * 1024 * 1024

NT_DIMS = (((1,), (1,)), ((), ()))
TN_DIMS = (((0,), (0,)), ((), ()))


def _cparams(*sem):
    return pltpu.CompilerParams(dimension_semantics=sem, vmem_limit_bytes=VMEM_LIMIT)


def _rms(x, g):
    return x * lax.rsqrt(jnp.mean(x * x, axis=-1, keepdims=True) + EPS) * g


def _const_spec(shape):
    return pl.BlockSpec(shape, lambda *_: (0,) * len(shape))


def _transpose_bf16(x, eye):
    return lax.dot_general(eye, x, NT_DIMS, preferred_element_type=F32).astype(BF16)


def _in_proj_kernel(x_ref, g_ref, wuv_ref, wkv_ref, wqT_ref, wgT_ref, uv_ref, kv_ref, qT_ref, gT_ref):
    h = _rms(x_ref[...], g_ref[...]).astype(BF16)
    step = 512
    for c0 in range(0, 2 * SG_WIDTH, step):
        uv_ref[:, c0:c0 + step] = jnp.dot(h, wuv_ref[:, c0:c0 + step], preferred_element_type=F32)
    kv_ref[...] = jnp.dot(h, wkv_ref[...], preferred_element_type=F32)
    qT_ref[...] = lax.dot_general(wqT_ref[...], h, NT_DIMS, preferred_element_type=F32)
    gT_ref[...] = lax.dot_general(wgT_ref[...], h, NT_DIMS, preferred_element_type=F32)


def _in_proj(x2, g, w_uv, w_kv, w_qT, w_gT, batch, seq):
    n = x2.shape[0]
    tm = 512
    per_b = seq // tm
    g_rows = w_gT.shape[0]

    def rows(width):
        return pl.BlockSpec((tm, width), lambda i: (i, 0))

    def cols(height):
        return pl.BlockSpec((None, height, tm), lambda i: (i // per_b, 0, i % per_b))

    return pl.pallas_call(
        _in_proj_kernel,
        out_shape=(jax.ShapeDtypeStruct((n, 2 * SG_WIDTH), F32), jax.ShapeDtypeStruct((n, KV_COLS), F32),
                   jax.ShapeDtypeStruct((batch, NSA_WIDTH, seq), F32),
                   jax.ShapeDtypeStruct((batch, g_rows, seq), F32)),
        grid=(n // tm,),
        in_specs=[rows(D_MODEL), _const_spec((1, D_MODEL)),
                  _const_spec((D_MODEL, 2 * SG_WIDTH)), _const_spec((D_MODEL, KV_COLS)),
                  _const_spec((NSA_WIDTH, D_MODEL)), _const_spec((g_rows, D_MODEL))],
        out_specs=(rows(2 * SG_WIDTH), rows(KV_COLS), cols(NSA_WIDTH), cols(g_rows)),
        compiler_params=_cparams("parallel"),
        name="in_proj",
    )(x2, g, w_uv, w_kv, w_qT, w_gT)


def _sgu_kernel(u_ref, v_ref, lng_ref, lnb_ref, w_ref, bs_ref, og_ref, a_ref, *, chunks):
    row = lax.broadcasted_iota(jnp.int32, (SG_CHUNK, SG_CHUNK), 0)
    col = lax.broadcasted_iota(jnp.int32, (SG_CHUNK, SG_CHUNK), 1)
    causal = col <= row
    w = [jnp.where(causal, w_ref[g], 0.0).astype(BF16) for g in range(SG_GROUPS)]
    for c in range(chunks):
        rows = slice(c * SG_CHUNK, (c + 1) * SG_CHUNK)
        u = jax.nn.gelu(u_ref[rows, :])
        v = jax.nn.gelu(v_ref[rows, :])
        mu = jnp.mean(v, axis=-1, keepdims=True)
        vc = v - mu
        vn = vc * lax.rsqrt(jnp.mean(vc * vc, axis=-1, keepdims=True) + EPS) * lng_ref[...] + lnb_ref[...]
        vb = vn.astype(BF16)
        s = jnp.concatenate(
            [jnp.dot(w[g], vb[:, g * HEAD_DIM:(g + 1) * HEAD_DIM], preferred_element_type=F32)
             for g in range(SG_GROUPS)], axis=-1) + bs_ref[...]
        a_ref[rows, :] = _rms(u * s, og_ref[...]).astype(BF16)


def _sgu(z, ln_g, ln_b, w_s, bs_full, out_g):
    n = z.shape[0]
    chunks = 4
    tm = chunks * SG_CHUNK
    return pl.pallas_call(
        functools.partial(_sgu_kernel, chunks=chunks),
        out_shape=jax.ShapeDtypeStruct((n, SG_WIDTH), BF16),
        grid=(n // tm,),
        in_specs=[pl.BlockSpec((tm, SG_WIDTH), lambda i: (i, 0)),
                  pl.BlockSpec((tm, SG_WIDTH), lambda i: (i, 1)),
                  _const_spec((1, SG_WIDTH)), _const_spec((1, SG_WIDTH)),
                  _const_spec((SG_GROUPS, SG_CHUNK, SG_CHUNK)),
                  _const_spec((SG_CHUNK, SG_WIDTH)), _const_spec((1, SG_WIDTH))],
        out_specs=pl.BlockSpec((tm, SG_WIDTH), lambda i: (i, 0)),
        compiler_params=_cparams("parallel"),
        name="sgu",
    )(z, z, ln_g, ln_b, w_s, bs_full, out_g)


def _kv_prep_kernel(kv_ref, gs_ref, gw_ref, eye_ref, kx_ref, kw_ref, vT_ref, kc_ref, vc_ref, *, tm, seq):
    kv = kv_ref[...]
    t0 = (pl.program_id(0) * tm) % seq
    blk = (t0 + lax.broadcasted_iota(jnp.int32, (tm, SLC_BLOCK), 0)) // SLC_BLOCK
    onehot = jnp.where(blk == lax.broadcasted_iota(jnp.int32, (tm, SLC_BLOCK), 1), 1.0, 0.0).astype(BF16)
    tail = jnp.where(lax.broadcasted_iota(jnp.int32, (BF16_SUBLANES, tm), 0) == 0, 1.0, 0.0).astype(BF16)
    for h in range(NSA_KV_HEADS):
        def part(idx):
            c0 = idx * KV_WIDTH + h * HEAD_DIM
            return kv[:, c0:c0 + HEAD_DIM]
        kc_ref[h] = part(0)
        vc_ref[h] = part(1)
        kx_ref[h, :, :HEAD_DIM] = _rms(part(2), gs_ref[...]).astype(BF16)
        kx_ref[h, :, HEAD_DIM:] = onehot
        kw_ref[h, :, :HEAD_DIM] = _rms(part(4), gw_ref[...]).astype(BF16)
        kw_ref[h, :, HEAD_DIM:] = onehot
        for branch, idx in ((0, 3), (1, 5)):
            vT_ref[h, branch, :HEAD_DIM, :] = _transpose_bf16(part(idx).astype(BF16), eye_ref[...])
            vT_ref[h, branch, HEAD_DIM:, :] = tail


def _kv_prep(z, g_slc, g_win, eye, batch, seq):
    n = z.shape[0]
    tm = ATTN_TILE
    per_b = seq // tm
    hd = NSA_KV_HEADS

    def rows_spec(width):
        return pl.BlockSpec((None, hd, tm, width), lambda i: (i // per_b, 0, i % per_b, 0))

    def rows_shape(width, dtype):
        return jax.ShapeDtypeStruct((batch, hd, seq, width), dtype)

    return pl.pallas_call(
        functools.partial(_kv_prep_kernel, tm=tm, seq=seq),
        out_shape=(rows_shape(2 * HEAD_DIM, BF16), rows_shape(2 * HEAD_DIM, BF16),
                   jax.ShapeDtypeStruct((batch, hd, 2, per_b, V_ROWS, tm), BF16),
                   rows_shape(HEAD_DIM, F32), rows_shape(HEAD_DIM, F32)),
        grid=(n // tm,),
        in_specs=[pl.BlockSpec((tm, KV_COLS), lambda i: (i, 0)),
                  _const_spec((1, HEAD_DIM)), _const_spec((1, HEAD_DIM)), _const_spec((HEAD_DIM, HEAD_DIM))],
        out_specs=(rows_spec(2 * HEAD_DIM), rows_spec(2 * HEAD_DIM),
                   pl.BlockSpec((None, hd, 2, None, V_ROWS, tm), lambda i: (i // per_b, 0, 0, i % per_b, 0, 0)),
                   rows_spec(HEAD_DIM), rows_spec(HEAD_DIM)),
        compiler_params=_cparams("parallel"),
        name="kv_prep",
    )(z, g_slc, g_win, eye)


def _cmp_kernel(c_ref, pos_ref, w1_ref, b1_ref, w2_ref, b2_ref, g_ref, o_ref, *, norm):
    hi = lax.Precision.HIGHEST
    c = c_ref[...]
    nb, half = c.shape
    first = jnp.dot(c, w1_ref[:half, :], preferred_element_type=F32, precision=hi)
    second = jnp.dot(c, w1_ref[half:, :], preferred_element_type=F32, precision=hi)
    pos = jnp.dot(pos_ref[...], w1_ref[...], preferred_element_type=F32, precision=hi)[0:1, :]
    h1 = jax.nn.gelu(first + pltpu.roll(second, nb - 1, 0) + pos + b1_ref[...])
    y = jnp.dot(h1, w2_ref[...], preferred_element_type=F32, precision=hi) + b2_ref[...]
    if norm:
        y = _rms(y, g_ref[...])
    o_ref[...] = y


def _compress(chunks, pos8, w1, b1, w2, b2, g, norm):
    bh, nb, half = chunks.shape
    return pl.pallas_call(
        functools.partial(_cmp_kernel, norm=norm),
        out_shape=jax.ShapeDtypeStruct((bh, nb, HEAD_DIM), F32),
        grid=(bh,),
        in_specs=[pl.BlockSpec((None, nb, half), lambda i: (i, 0, 0)),
                  _const_spec((8, 2 * half)), _const_spec((2 * half, CMP_HIDDEN)), _const_spec((1, CMP_HIDDEN)),
                  _const_spec((CMP_HIDDEN, HEAD_DIM)), _const_spec((1, HEAD_DIM)), _const_spec((1, HEAD_DIM))],
        out_specs=pl.BlockSpec((None, nb, HEAD_DIM), lambda i: (i, 0, 0)),
        compiler_params=_cparams("parallel"),
        name="compress_k" if norm else "compress_v",
    )(chunks, pos8, w1, b1, w2, b2, g)


def _stable_rank(score, i, score_sc, rank_sc, rows_sc, tq):
    n_slc = score.shape[0]
    for k in range(n_slc):
        rows_sc[k] = score[k:k + 1, :]
    score_sc[...] = score
    rank_sc[...] = jnp.zeros(rank_sc.shape, F32)
    sub, k_chunk = 8, 16
    blocks_per_tile = tq // SLC_BLOCK
    row_in_group = lax.broadcasted_iota(jnp.int32, (sub, tq), 0)
    for k0 in range(0, n_slc, k_chunk):
        @pl.when(i * blocks_per_tile + blocks_per_tile > k0)
        def _(k0=k0):
            rks = [jnp.broadcast_to(rows_sc[k], (sub, tq)) for k in range(k0, k0 + k_chunk)]
            for r0 in range(0, n_slc, sub):
                sg = score_sc[r0:r0 + sub, :]
                acc = rank_sc[r0:r0 + sub, :]
                for k in range(k0, k0 + k_chunk):
                    rk = rks[k - k0]
                    ge = jnp.where(rk >= sg, 1.0, 0.0)
                    gt = jnp.where(rk > sg, 1.0, 0.0)
                    if r0 > k:
                        before = ge
                    elif r0 + sub - 1 < k:
                        before = gt
                    else:
                        before = jnp.where(row_in_group > k - r0, ge, gt)
                    acc = acc + before
                rank_sc[r0:r0 + sub, :] = acc


def _attn_kernel(qT_ref, glT_ref, kc_ref, vcT_ref, ovT_ref, kx_ref, kw_ref, vT_ref, mask_ref, gq_ref, shift_ref,
                 o_ref, qw_sc, qx_sc, s_a, mo_a, mn_a, s_b, mo_b, mn_b, s_c, score_sc, rank_sc, rows_sc, mrun_sc, acc_sc,
                 *, tq, n_cmp, bounded):
    i = pl.program_id(2)
    d = HEAD_DIM
    G = NSA_GROUP
    shift = shift_ref[0]

    def lanes(g):
        return slice(g * tq, (g + 1) * tq)

    for g in range(G):
        qg = qT_ref[g * d:(g + 1) * d, :]
        qg = qg * lax.rsqrt(jnp.mean(qg * qg, axis=0, keepdims=True) + EPS) * gq_ref[...]
        qw_sc[0:d, lanes(g)] = qg.astype(BF16)
    qw_sc[d:2 * d, :] = jnp.full((d, G * tq), shift, F32).astype(BF16)

    gates = jax.nn.sigmoid(glT_ref[...])

    if not bounded:
        mrun_sc[...] = jnp.full(mrun_sc.shape, -jnp.inf, F32)
    acc_sc[...] = jnp.zeros(acc_sc.shape, F32)

    def scores(branch, k_ref, k_idx, q_sc, mask, s_buf, mo_buf, mn_buf):
        for g in range(G):
            s = jnp.dot(k_ref[k_idx], q_sc[:, lanes(g)], preferred_element_type=F32)
            if mask is not None:
                s = s + mask_ref[mask]
            s_buf[g] = s
            if not bounded:
                m_old = mrun_sc[branch, g]
                m_new = jnp.maximum(m_old, jnp.max(s, axis=0, keepdims=True))
                mo_buf[g] = m_old
                mn_buf[g] = m_new
                mrun_sc[branch, g] = m_new

    def accumulate(branch, v_idx, s_buf, mo_buf, mn_buf):
        for g in range(G):
            if bounded:
                p = jnp.exp2(s_buf[g]).astype(BF16)
                acc_sc[branch, g] += jnp.dot(vT_ref[branch, v_idx], p, preferred_element_type=F32)
            else:
                m_new = mn_buf[g]
                alpha = jnp.exp2(mo_buf[g] - m_new)
                p = jnp.exp2(s_buf[g] - m_new).astype(BF16)
                acc_sc[branch, g] = alpha * acc_sc[branch, g] + jnp.dot(vT_ref[branch, v_idx], p,
                                                                        preferred_element_type=F32)

    buf_a = (s_a, mo_a, mn_a)
    buf_b = (s_b, mo_b, mn_b)

    w_tiles = [jnp.maximum(i - 2, 0), jnp.maximum(i - 1, 0), i]
    w_masks = [jnp.where(i >= 2, 2, 3), jnp.where(i >= 1, 0, 3), 1]
    scores(1, kw_ref, w_tiles[0], qw_sc, w_masks[0], *buf_a)
    scores(1, kw_ref, w_tiles[1], qw_sc, w_masks[1], *buf_b)
    accumulate(1, w_tiles[0], *buf_a)
    scores(1, kw_ref, w_tiles[2], qw_sc, w_masks[2], *buf_a)
    accumulate(1, w_tiles[1], *buf_b)
    accumulate(1, w_tiles[2], *buf_a)

    ncp = kc_ref.shape[0]
    t_idx = i * tq + lax.broadcasted_iota(jnp.int32, (ncp, tq), 1)
    n_idx = lax.broadcasted_iota(jnp.int32, (ncp, tq), 0)
    c_mask = jnp.where((n_idx * CMP_STRIDE + (CMP_BLOCK - 1) <= t_idx) & (n_idx < n_cmp), 0.0, MASKED)
    sees_any = jnp.where(t_idx[0:1, :] >= CMP_BLOCK - 1, 1.0, 0.0)
    c_max = []
    for g in range(G):
        s = jnp.dot(kc_ref[...], qw_sc[:, lanes(g)], preferred_element_type=F32) + c_mask
        s_c[g] = s
        if not bounded:
            c_max.append(jnp.max(s, axis=0, keepdims=True))
    psum = jnp.zeros((ncp, tq), F32)
    for g in range(G):
        e = jnp.exp2(s_c[g]) if bounded else jnp.exp2(s_c[g] - c_max[g])
        oc = jnp.dot(vcT_ref[...], e.astype(BF16), preferred_element_type=F32)
        den = oc[d:d + 1, :]
        r = sees_any / jnp.where(den > 0, den, 1.0)
        psum = psum + e * r
        o_ref[g * d:(g + 1) * d, :] = (gates[g * 3:g * 3 + 1, :] * r) * oc[0:d, :]

    p_hi = psum.astype(BF16)
    p_lo = (psum - p_hi.astype(F32)).astype(BF16)
    imp = (jnp.dot(ovT_ref[...], p_hi, preferred_element_type=F32)
           + jnp.dot(ovT_ref[...], p_lo, preferred_element_type=F32))
    n_slc = imp.shape[0]
    j_idx = lax.broadcasted_iota(jnp.int32, (n_slc, tq), 0)
    t_blk = (i * tq + lax.broadcasted_iota(jnp.int32, (n_slc, tq), 1)) // SLC_BLOCK
    forced = (j_idx == 0) | (j_idx == t_blk) | (j_idx == t_blk - 1)
    score = jnp.where(forced, FORCE_SCORE, jnp.where(j_idx <= t_blk, imp, -FORCE_SCORE))
    _stable_rank(score, i, score_sc, rank_sc, rows_sc, tq)
    n_sel = min(SLC_TOPK, n_slc)
    bias = jnp.where(rank_sc[...] < n_sel, shift, MASKED).astype(BF16)
    if n_slc < d:
        bias = jnp.concatenate([bias, jnp.zeros((d - n_slc, tq), BF16)], axis=0)
    qx_sc[0:d, :] = qw_sc[0:d, :]
    for g in range(G):
        qx_sc[d:2 * d, lanes(g)] = bias

    def slc_mask(j):
        return jnp.where(j == i, 1, jnp.where(j > i, 3, 0))

    def slc_tile(j):
        return jnp.minimum(j, i)

    scores(0, kx_ref, 0, qx_sc, slc_mask(0), *buf_a)
    n_pairs = jnp.maximum(i - 1, 0) // 2

    def pair(j0):
        scores(0, kx_ref, j0 + 1, qx_sc, None, *buf_b)
        accumulate(0, j0, *buf_a)
        scores(0, kx_ref, j0 + 2, qx_sc, None, *buf_a)
        accumulate(0, j0 + 1, *buf_b)

    def two_pairs(kk, carry):
        pair(4 * kk)
        pair(4 * kk + 2)
        return carry

    def one_pair(kk, carry):
        pair(4 * (n_pairs // 2))
        return carry

    lax.fori_loop(0, n_pairs // 2, two_pairs, 0)
    lax.fori_loop(0, n_pairs % 2, one_pair, 0)
    j0 = 2 * n_pairs
    scores(0, kx_ref, slc_tile(j0 + 1), qx_sc, slc_mask(j0 + 1), *buf_b)
    accumulate(0, j0, *buf_a)
    scores(0, kx_ref, slc_tile(j0 + 2), qx_sc, slc_mask(j0 + 2), *buf_a)
    accumulate(0, slc_tile(j0 + 1), *buf_b)
    accumulate(0, slc_tile(j0 + 2), *buf_a)

    for branch in range(2):
        for g in range(G):
            acc = acc_sc[branch, g]
            o = acc[0:d, :] * (1.0 / acc[d:d + 1, :])
            gate = gates[g * 3 + branch + 1:g * 3 + branch + 2, :]
            o_ref[g * d:(g + 1) * d, :] += gate * o


def _attention(qT, glT, kc_x, vcT_x, ovT, kx, kw, vT, masks, gq, shift, n_cmp, bounded):
    batch, _, seq = qT.shape
    tq = ATTN_TILE
    nq = seq // tq
    hd = NSA_KV_HEADS
    gw = NSA_GROUP * HEAD_DIM
    ncp = kc_x.shape[1]
    n_slc = ovT.shape[0]

    def per_head(shape):
        return pl.BlockSpec((None, None) + shape, lambda b, h, i: (b, h) + (0,) * len(shape))

    stage = [pltpu.VMEM((NSA_GROUP, tq, tq), F32),
             pltpu.VMEM((NSA_GROUP, 1, tq), F32),
             pltpu.VMEM((NSA_GROUP, 1, tq), F32)]
    return pl.pallas_call(
        functools.partial(_attn_kernel, tq=tq, n_cmp=n_cmp, bounded=bounded),
        out_shape=jax.ShapeDtypeStruct((batch, NSA_WIDTH, seq), F32),
        grid=(batch, hd, nq),
        in_specs=[pl.BlockSpec((None, gw, tq), lambda b, h, i: (b, h, i)),
                  pl.BlockSpec((None, GATE_ROWS, tq), lambda b, h, i: (b, h, i)),
                  pl.BlockSpec((None, ncp, 2 * HEAD_DIM), lambda b, h, i: (b * hd + h, 0, 0)),
                  pl.BlockSpec((None, V_ROWS, ncp), lambda b, h, i: (b * hd + h, 0, 0)),
                  _const_spec((n_slc, ncp)),
                  per_head((nq, tq, 2 * HEAD_DIM)), per_head((nq, tq, 2 * HEAD_DIM)),
                  per_head((2, nq, V_ROWS, tq)),
                  _const_spec((4, tq, tq)),
                  _const_spec((HEAD_DIM, 1)),
                  pl.BlockSpec(memory_space=pltpu.SMEM)],
        out_specs=pl.BlockSpec((None, gw, tq), lambda b, h, i: (b, h, i)),
        scratch_shapes=[pltpu.VMEM((2 * HEAD_DIM, NSA_GROUP * tq), BF16),
                        pltpu.VMEM((2 * HEAD_DIM, NSA_GROUP * tq), BF16)] + stage + stage + [
            pltpu.VMEM((NSA_GROUP, ncp, tq), F32),
            pltpu.VMEM((n_slc, tq), F32), pltpu.VMEM((n_slc, tq), F32), pltpu.VMEM((n_slc, 1, tq), F32),
            pltpu.VMEM((2, NSA_GROUP, 1, tq), F32),
            pltpu.VMEM((2, NSA_GROUP, V_ROWS, tq), F32)],
        compiler_params=_cparams("parallel", "parallel", "arbitrary"),
        name="nsa_attention_bounded" if bounded else "nsa_attention",
    )(qT, glT, kc_x, vcT_x, ovT, kx, kw, vT, masks, gq, shift)


def _out_proj_kernel(a_ref, bT_ref, x_ref, g_ref, w_ref, o_ref):
    bT = bT_ref[...]
    bn = (bT * lax.rsqrt(jnp.mean(bT * bT, axis=0, keepdims=True) + EPS) * g_ref[...]).astype(BF16)
    y = jnp.dot(a_ref[...], w_ref[:SG_WIDTH, :], preferred_element_type=F32)
    y = y + lax.dot_general(bn, w_ref[SG_WIDTH:, :], TN_DIMS, preferred_element_type=F32)
    o_ref[...] = x_ref[...] + y


def _out_proj(a_n, bT, x2, g_b, w_out, seq):
    n = x2.shape[0]
    tm = 512
    per_b = seq // tm
    return pl.pallas_call(
        _out_proj_kernel,
        out_shape=jax.ShapeDtypeStruct((n, D_MODEL), F32),
        grid=(n // tm,),
        in_specs=[pl.BlockSpec((tm, SG_WIDTH), lambda i: (i, 0)),
                  pl.BlockSpec((None, NSA_WIDTH, tm), lambda i: (i // per_b, 0, i % per_b)),
                  pl.BlockSpec((tm, D_MODEL), lambda i: (i, 0)),
                  _const_spec((NSA_WIDTH, 1)), _const_spec((D_MODEL, D_MODEL))],
        out_specs=pl.BlockSpec((tm, D_MODEL), lambda i: (i, 0)),
        compiler_params=_cparams("parallel"),
        name="out_proj",
    )(a_n, bT, x2, g_b, w_out)


def _mem_kv_kernel(mem_ref, g_ref, w_ref, kg_ref, k_ref, v_ref):
    mn = _rms(mem_ref[...], g_ref[...]).astype(BF16)
    kv = jnp.dot(mn, w_ref[...], preferred_element_type=F32)
    width = MEM_HEADS * MEM_HEAD_DIM
    for hh in range(MEM_HEADS):
        cols = slice(hh * MEM_HEAD_DIM, (hh + 1) * MEM_HEAD_DIM)
        k_ref[:, cols] = _rms(kv[:, cols], kg_ref[...]).astype(BF16)
    v_ref[...] = kv[:, width:].astype(BF16)


def _mem_kv(mem, g, w_mkv, k_g):
    batch, m_len, _ = mem.shape
    width = MEM_HEADS * MEM_HEAD_DIM
    spec = pl.BlockSpec((None, m_len, width), lambda b: (b, 0, 0))
    return pl.pallas_call(
        _mem_kv_kernel,
        out_shape=(jax.ShapeDtypeStruct((batch, m_len, width), BF16),) * 2,
        grid=(batch,),
        in_specs=[pl.BlockSpec((None, m_len, D_MODEL), lambda b: (b, 0, 0)),
                  _const_spec((1, D_MODEL)), _const_spec((D_MODEL, 2 * width)),
                  _const_spec((1, MEM_HEAD_DIM))],
        out_specs=(spec, spec),
        compiler_params=_cparams("parallel"),
        name="mem_kv",
    )(mem, g, w_mkv, k_g)


def _mem_attn_kernel(x_ref, g_ref, wq_ref, qg_ref, k_ref, v_ref, wo_ref, o_ref):
    x = x_ref[...]
    h = _rms(x, g_ref[...]).astype(BF16)
    q = jnp.dot(h, wq_ref[...], preferred_element_type=F32)
    outs = []
    for hh in range(MEM_HEADS):
        cols = slice(hh * MEM_HEAD_DIM, (hh + 1) * MEM_HEAD_DIM)
        qh = (_rms(q[:, cols], qg_ref[...]) * MEM_HEAD_DIM ** -0.5).astype(BF16)
        s = lax.dot_general(qh, k_ref[:, cols], NT_DIMS, preferred_element_type=F32)
        m = jnp.max(s, axis=-1, keepdims=True)
        e = jnp.exp(s - m)
        p = e / jnp.sum(e, axis=-1, keepdims=True)
        outs.append(jnp.dot(p.astype(BF16), v_ref[:, cols], preferred_element_type=F32))
    o = jnp.concatenate(outs, axis=-1).astype(BF16)
    o_ref[...] = x + jnp.dot(o, wo_ref[...], preferred_element_type=F32)


def _mem_attn(x2, g, w_mq, q_g, k, v, w_mo, seq):
    n = x2.shape[0]
    tm = 256
    per_b = seq // tm
    m_len = k.shape[1]
    width = MEM_HEADS * MEM_HEAD_DIM
    kv_spec = pl.BlockSpec((None, m_len, width), lambda i: (i // per_b, 0, 0))
    return pl.pallas_call(
        _mem_attn_kernel,
        out_shape=jax.ShapeDtypeStruct((n, D_MODEL), F32),
        grid=(n // tm,),
        in_specs=[pl.BlockSpec((tm, D_MODEL), lambda i: (i, 0)),
                  _const_spec((1, D_MODEL)), _const_spec((D_MODEL, width)), _const_spec((1, MEM_HEAD_DIM)),
                  kv_spec, kv_spec, _const_spec((width, D_MODEL))],
        out_specs=pl.BlockSpec((tm, D_MODEL), lambda i: (i, 0)),
        compiler_params=_cparams("parallel"),
        name="mem_attn",
    )(x2, g, w_mq, q_g, k, v, w_mo)


def _ffn_kernel(x_ref, g_ref, w1_ref, w2_ref, o_ref):
    x = x_ref[...]
    h = _rms(x, g_ref[...]).astype(BF16)
    step = 1024
    acc = x
    for c0 in range(0, D_FF, step):
        a = jnp.maximum(jnp.dot(h, w1_ref[:, c0:c0 + step], preferred_element_type=F32), 0.0)
        acc = acc + jnp.dot((a * a).astype(BF16), w2_ref[c0:c0 + step, :], preferred_element_type=F32)
    o_ref[...] = acc


def _ffn(x2, g, w1, w2):
    n = x2.shape[0]
    tm = 512
    return pl.pallas_call(
        _ffn_kernel,
        out_shape=jax.ShapeDtypeStruct((n, D_MODEL), F32),
        grid=(n // tm,),
        in_specs=[pl.BlockSpec((tm, D_MODEL), lambda i: (i, 0)),
                  _const_spec((1, D_MODEL)),
                  pl.BlockSpec((D_MODEL, D_FF), lambda i: (0, 0), pipeline_mode=pl.Buffered(1)),
                  pl.BlockSpec((D_FF, D_MODEL), lambda i: (0, 0), pipeline_mode=pl.Buffered(1))],
        out_specs=pl.BlockSpec((tm, D_MODEL), lambda i: (i, 0)),
        compiler_params=_cparams("parallel"),
        name="ffn",
    )(x2, g, w1, w2)


def _overlap_T(n_cmp_pad, n_cmp, n_slc):
    cs = jnp.arange(n_cmp_pad) * CMP_STRIDE
    ss = jnp.arange(n_slc) * SLC_BLOCK
    ov = jnp.clip(jnp.minimum(cs[None, :] + CMP_BLOCK, ss[:, None] + SLC_BLOCK)
                  - jnp.maximum(cs[None, :], ss[:, None]), 0, None).astype(F32) / CMP_BLOCK
    ov = jnp.where(jnp.arange(n_cmp_pad)[None, :] < n_cmp, ov, 0.0)
    return ov.astype(BF16)


def kernel(x, mem, norm_mix_g, w_in, sg_ln_g, sg_ln_b, sg_w, sg_b, q_norm_g, k_norm_g, cmp_pos, cmp_w1, cmp_b1, cmp_w2, cmp_b2, mix_out_g, w_out, norm_mem_g, mem_kv_norm_g, w_mq, w_mkv, mem_q_norm_g, mem_k_norm_g, w_mo, norm_ffn_g, w_ff1, w_ff2):
    batch, seq, _ = x.shape
    depth = w_in.shape[0]
    n = batch * seq
    tq = ATTN_TILE
    assert WINDOW == 2 * tq and seq % 512 == 0
    n_chunk = seq // CMP_STRIDE
    n_cmp = n_chunk - 1
    n_slc = seq // SLC_BLOCK
    assert n_slc <= HEAD_DIM
    ovT = _overlap_T(n_chunk, n_cmp, n_slc)
    eye = jnp.eye(HEAD_DIM, dtype=BF16)
    row = lambda a: a.reshape(1, -1)
    r_idx = lax.broadcasted_iota(jnp.int32, (tq, tq), 0)
    c_idx = lax.broadcasted_iota(jnp.int32, (tq, tq), 1)
    masks = jnp.stack([jnp.zeros((tq, tq), F32), jnp.where(r_idx <= c_idx, 0.0, MASKED),
                       jnp.where(r_idx > c_idx, 0.0, MASKED), jnp.full((tq, tq), MASKED, F32)])

    x2 = x.reshape(n, D_MODEL)
    for l in range(depth):
        w = w_in[l].astype(BF16)
        w_gT = jnp.pad(w[:, COL_GATE:].T.reshape(NSA_KV_HEADS, GATES_PER_HEAD, D_MODEL),
                       ((0, 0), (0, GATE_ROWS - GATES_PER_HEAD), (0, 0))).reshape(NSA_KV_HEADS * GATE_ROWS, D_MODEL)
        z_uv, z_kv, qT, glT = _in_proj(x2, row(norm_mix_g[l]), w[:, :COL_Q], w[:, COL_KV:COL_GATE],
                                       w[:, COL_Q:COL_KV].T, w_gT, batch, seq)

        bs_full = jnp.repeat(sg_b[l].T, HEAD_DIM, axis=1)
        a_n = _sgu(z_uv, row(sg_ln_g[l]), row(sg_ln_b[l]), sg_w[l], bs_full, row(mix_out_g[l, 0]))

        kx, kw, vT, k_cmp, v_cmp = _kv_prep(z_kv, row(k_norm_g[l, 1]), row(k_norm_g[l, 2]), eye, batch, seq)
        tiles = lambda a: a.reshape(batch, NSA_KV_HEADS, seq // tq, tq, a.shape[-1])
        chunks = lambda a: a.reshape(batch * NSA_KV_HEADS, n_chunk, CMP_STRIDE * HEAD_DIM)
        pos8 = lambda p: jnp.broadcast_to(p.reshape(1, -1), (8, CMP_BLOCK * HEAD_DIM))
        kc = _compress(chunks(k_cmp), pos8(cmp_pos[l, 0]), cmp_w1[l, 0], row(cmp_b1[l, 0]), cmp_w2[l, 0],
                       row(cmp_b2[l, 0]), row(k_norm_g[l, 0]), True)
        vc = _compress(chunks(v_cmp), pos8(cmp_pos[l, 1]), cmp_w1[l, 1], row(cmp_b1[l, 1]), cmp_w2[l, 1],
                       row(cmp_b2[l, 1]), row(k_norm_g[l, 0]), False)
        vcT = vc.transpose(0, 2, 1)

        gq = (q_norm_g[l] * (HEAD_DIM ** -0.5 * LOG2E)).reshape(HEAD_DIM, 1)
        kx, kw = tiles(kx), tiles(kw)
        peak = lambda g: jnp.max(jnp.abs(g))
        bound = HEAD_DIM ** 0.5 * LOG2E * peak(q_norm_g[l]) * peak(k_norm_g[l])

        kc_x = jnp.concatenate([kc, jnp.ones_like(kc[..., :1]), jnp.zeros_like(kc[..., 1:])], axis=-1).astype(BF16)
        vcT_x = jnp.concatenate([vcT, jnp.ones_like(vcT[:, :1]),
                                 jnp.zeros_like(vcT[:, :BF16_SUBLANES - 1])], axis=1).astype(BF16)
        ops = (qT, glT, kc_x, vcT_x, ovT, kx, kw, vT, masks, gq)
        oT = lax.cond(2.0 * bound <= MAX_SCORE_SPAN,
                      lambda ops, bound: _attention(*ops, (-bound).reshape(1), n_cmp, True),
                      lambda ops, bound: _attention(*ops, jnp.zeros((1,), F32), n_cmp, False),
                      ops, bound)

        x2 = _out_proj(a_n, oT, x2, mix_out_g[l, 1].reshape(NSA_WIDTH, 1), w_out[l].astype(BF16), seq)

        mk, mv = _mem_kv(mem, row(mem_kv_norm_g[l]), w_mkv[l].astype(BF16), row(mem_k_norm_g[l]))
        x2 = _mem_attn(x2, row(norm_mem_g[l]), w_mq[l].astype(BF16), row(mem_q_norm_g[l]), mk, mv,
                       w_mo[l].astype(BF16), seq)

        x2 = _ffn(x2, row(norm_ffn_g[l]), w_ff1[l].astype(BF16), w_ff2[l].astype(BF16))
    return x2.reshape(batch, seq, D_MODEL)
```

```python
import functools

import jax
import jax.numpy as jnp
from jax import lax
from jax.experimental import pallas as pl
from jax.experimental.pallas import tpu as pltpu

F32 = jnp.float32
BF16 = jnp.bfloat16

D_MODEL = 1024
HEAD_DIM = 64
SG_WIDTH = 512
SG_GROUPS = 8
SG_CHUNK = 128
NSA_WIDTH = 512
NSA_KV_HEADS = 2
NSA_GROUP = 4
KV_WIDTH = NSA_KV_HEADS * HEAD_DIM
N_BRANCH = 3
CMP_BLOCK = 32
CMP_STRIDE = 16
CMP_HIDDEN = 256
SLC_BLOCK = 64
SLC_TOPK = 16
WINDOW = 512
FORCE_SCORE = 1e4
MEM_HEADS = 4
MEM_HEAD_DIM = 128
D_FF = 4 * D_MODEL
EPS = 1e-6

COL_Q = 2 * SG_WIDTH
COL_KV = COL_Q + NSA_WIDTH
COL_GATE = COL_KV + 6 * KV_WIDTH
KV_COLS = 6 * KV_WIDTH
GATES_PER_HEAD = NSA_GROUP * N_BRANCH
GATE_ROWS = 16

MASKED = -1e30
LOG2E = 1.4426950408889634
MAX_SCORE_SPAN = 100.0
BF16_SUBLANES = 16
V_ROWS = HEAD_DIM + BF16_SUBLANES
ATTN_TILE = 256
VMEM_LIMIT = 48 * 1024 * 1024
TAIL_VMEM_LIMIT = 56 * 1024 * 1024

NT_DIMS = (((1,), (1,)), ((), ()))
TN_DIMS = (((0,), (0,)), ((), ()))


def _cparams(*sem):
    return pltpu.CompilerParams(dimension_semantics=sem, vmem_limit_bytes=VMEM_LIMIT)


def _rms(x, g):
    return x * lax.rsqrt(jnp.mean(x * x, axis=-1, keepdims=True) + EPS) * g


def _const_spec(shape):
    return pl.BlockSpec(shape, lambda *_: (0,) * len(shape))


def _transpose_bf16(x, eye):
    return lax.dot_general(eye, x, NT_DIMS, preferred_element_type=F32).astype(BF16)


def _in_proj_kernel(x_ref, g_ref, wuv_ref, wkv_ref, wqT_ref, wgT_ref, uv_ref, kv_ref, qT_ref, gT_ref):
    h = _rms(x_ref[...], g_ref[...]).astype(BF16)
    step = 512
    for c0 in range(0, 2 * SG_WIDTH, step):
        uv_ref[:, c0:c0 + step] = jnp.dot(h, wuv_ref[:, c0:c0 + step], preferred_element_type=F32)
    kv_ref[...] = jnp.dot(h, wkv_ref[...], preferred_element_type=F32)
    qT_ref[...] = lax.dot_general(wqT_ref[...], h, NT_DIMS, preferred_element_type=F32)
    gT_ref[...] = lax.dot_general(wgT_ref[...], h, NT_DIMS, preferred_element_type=F32)


def _in_proj(x2, g, w_uv, w_kv, w_qT, w_gT, batch, seq):
    n = x2.shape[0]
    tm = 512
    per_b = seq // tm
    g_rows = w_gT.shape[0]

    def rows(width):
        return pl.BlockSpec((tm, width), lambda i: (i, 0))

    def cols(height):
        return pl.BlockSpec((None, height, tm), lambda i: (i // per_b, 0, i % per_b))

    return pl.pallas_call(
        _in_proj_kernel,
        out_shape=(jax.ShapeDtypeStruct((n, 2 * SG_WIDTH), F32), jax.ShapeDtypeStruct((n, KV_COLS), F32),
                   jax.ShapeDtypeStruct((batch, NSA_WIDTH, seq), F32),
                   jax.ShapeDtypeStruct((batch, g_rows, seq), F32)),
        grid=(n // tm,),
        in_specs=[rows(D_MODEL), _const_spec((1, D_MODEL)),
                  _const_spec((D_MODEL, 2 * SG_WIDTH)), _const_spec((D_MODEL, KV_COLS)),
                  _const_spec((NSA_WIDTH, D_MODEL)), _const_spec((g_rows, D_MODEL))],
        out_specs=(rows(2 * SG_WIDTH), rows(KV_COLS), cols(NSA_WIDTH), cols(g_rows)),
        compiler_params=_cparams("parallel"),
        name="in_proj",
    )(x2, g, w_uv, w_kv, w_qT, w_gT)


def _sgu_kernel(u_ref, v_ref, lng_ref, lnb_ref, w_ref, bs_ref, og_ref, a_ref, *, chunks):
    row = lax.broadcasted_iota(jnp.int32, (SG_CHUNK, SG_CHUNK), 0)
    col = lax.broadcasted_iota(jnp.int32, (SG_CHUNK, SG_CHUNK), 1)
    causal = col <= row
    w = [jnp.where(causal, w_ref[g], 0.0).astype(BF16) for g in range(SG_GROUPS)]
    for c in range(chunks):
        rows = slice(c * SG_CHUNK, (c + 1) * SG_CHUNK)
        u = jax.nn.gelu(u_ref[rows, :])
        v = jax.nn.gelu(v_ref[rows, :])
        mu = jnp.mean(v, axis=-1, keepdims=True)
        vc = v - mu
        vn = vc * lax.rsqrt(jnp.mean(vc * vc, axis=-1, keepdims=True) + EPS) * lng_ref[...] + lnb_ref[...]
        vb = vn.astype(BF16)
        s = jnp.concatenate(
            [jnp.dot(w[g], vb[:, g * HEAD_DIM:(g + 1) * HEAD_DIM], preferred_element_type=F32)
             for g in range(SG_GROUPS)], axis=-1) + bs_ref[...]
        a_ref[rows, :] = _rms(u * s, og_ref[...]).astype(BF16)


def _sgu(z, ln_g, ln_b, w_s, bs_full, out_g):
    n = z.shape[0]
    chunks = 4
    tm = chunks * SG_CHUNK
    return pl.pallas_call(
        functools.partial(_sgu_kernel, chunks=chunks),
        out_shape=jax.ShapeDtypeStruct((n, SG_WIDTH), BF16),
        grid=(n // tm,),
        in_specs=[pl.BlockSpec((tm, SG_WIDTH), lambda i: (i, 0)),
                  pl.BlockSpec((tm, SG_WIDTH), lambda i: (i, 1)),
                  _const_spec((1, SG_WIDTH)), _const_spec((1, SG_WIDTH)),
                  _const_spec((SG_GROUPS, SG_CHUNK, SG_CHUNK)),
                  _const_spec((SG_CHUNK, SG_WIDTH)), _const_spec((1, SG_WIDTH))],
        out_specs=pl.BlockSpec((tm, SG_WIDTH), lambda i: (i, 0)),
        compiler_params=_cparams("parallel"),
        name="sgu",
    )(z, z, ln_g, ln_b, w_s, bs_full, out_g)


def _kv_prep_kernel(kv_ref, gs_ref, gw_ref, eye_ref, kx_ref, kw_ref, vT_ref, kc_ref, vc_ref, *, tm, seq):
    kv = kv_ref[...]
    t0 = (pl.program_id(0) * tm) % seq
    blk = (t0 + lax.broadcasted_iota(jnp.int32, (tm, SLC_BLOCK), 0)) // SLC_BLOCK
    onehot = jnp.where(blk == lax.broadcasted_iota(jnp.int32, (tm, SLC_BLOCK), 1), 1.0, 0.0).astype(BF16)
    tail = jnp.where(lax.broadcasted_iota(jnp.int32, (BF16_SUBLANES, tm), 0) == 0, 1.0, 0.0).astype(BF16)
    for h in range(NSA_KV_HEADS):
        def part(idx):
            c0 = idx * KV_WIDTH + h * HEAD_DIM
            return kv[:, c0:c0 + HEAD_DIM]
        kc_ref[h] = part(0)
        vc_ref[h] = part(1)
        kx_ref[h, :, :HEAD_DIM] = _rms(part(2), gs_ref[...]).astype(BF16)
        kx_ref[h, :, HEAD_DIM:] = onehot
        kw_ref[h, :, :HEAD_DIM] = _rms(part(4), gw_ref[...]).astype(BF16)
        kw_ref[h, :, HEAD_DIM:] = onehot
        for branch, idx in ((0, 3), (1, 5)):
            vT_ref[h, branch, :HEAD_DIM, :] = _transpose_bf16(part(idx).astype(BF16), eye_ref[...])
            vT_ref[h, branch, HEAD_DIM:, :] = tail


def _kv_prep(z, g_slc, g_win, eye, batch, seq):
    n = z.shape[0]
    tm = ATTN_TILE
    per_b = seq // tm
    hd = NSA_KV_HEADS

    def rows_spec(width):
        return pl.BlockSpec((None, hd, tm, width), lambda i: (i // per_b, 0, i % per_b, 0))

    def rows_shape(width, dtype):
        return jax.ShapeDtypeStruct((batch, hd, seq, width), dtype)

    return pl.pallas_call(
        functools.partial(_kv_prep_kernel, tm=tm, seq=seq),
        out_shape=(rows_shape(2 * HEAD_DIM, BF16), rows_shape(2 * HEAD_DIM, BF16),
                   jax.ShapeDtypeStruct((batch, hd, 2, per_b, V_ROWS, tm), BF16),
                   rows_shape(HEAD_DIM, F32), rows_shape(HEAD_DIM, F32)),
        grid=(n // tm,),
        in_specs=[pl.BlockSpec((tm, KV_COLS), lambda i: (i, 0)),
                  _const_spec((1, HEAD_DIM)), _const_spec((1, HEAD_DIM)), _const_spec((HEAD_DIM, HEAD_DIM))],
        out_specs=(rows_spec(2 * HEAD_DIM), rows_spec(2 * HEAD_DIM),
                   pl.BlockSpec((None, hd, 2, None, V_ROWS, tm), lambda i: (i // per_b, 0, 0, i % per_b, 0, 0)),
                   rows_spec(HEAD_DIM), rows_spec(HEAD_DIM)),
        compiler_params=_cparams("parallel"),
        name="kv_prep",
    )(z, g_slc, g_win, eye)


def _cmp_kernel(c_ref, pos_ref, w1_ref, b1_ref, w2_ref, b2_ref, g_ref, o_ref, *, norm):
    hi = lax.Precision.HIGHEST
    c = c_ref[...]
    nb, half = c.shape
    first = jnp.dot(c, w1_ref[:half, :], preferred_element_type=F32, precision=hi)
    second = jnp.dot(c, w1_ref[half:, :], preferred_element_type=F32, precision=hi)
    pos = jnp.dot(pos_ref[...], w1_ref[...], preferred_element_type=F32, precision=hi)[0:1, :]
    h1 = jax.nn.gelu(first + pltpu.roll(second, nb - 1, 0) + pos + b1_ref[...])
    y = jnp.dot(h1, w2_ref[...], preferred_element_type=F32, precision=hi) + b2_ref[...]
    if norm:
        y = _rms(y, g_ref[...])
    o_ref[...] = y


def _compress(chunks, pos8, w1, b1, w2, b2, g, norm):
    bh, nb, half = chunks.shape
    return pl.pallas_call(
        functools.partial(_cmp_kernel, norm=norm),
        out_shape=jax.ShapeDtypeStruct((bh, nb, HEAD_DIM), F32),
        grid=(bh,),
        in_specs=[pl.BlockSpec((None, nb, half), lambda i: (i, 0, 0)),
                  _const_spec((8, 2 * half)), _const_spec((2 * half, CMP_HIDDEN)), _const_spec((1, CMP_HIDDEN)),
                  _const_spec((CMP_HIDDEN, HEAD_DIM)), _const_spec((1, HEAD_DIM)), _const_spec((1, HEAD_DIM))],
        out_specs=pl.BlockSpec((None, nb, HEAD_DIM), lambda i: (i, 0, 0)),
        compiler_params=_cparams("parallel"),
        name="compress_k" if norm else "compress_v",
    )(chunks, pos8, w1, b1, w2, b2, g)


def _stable_rank(score, score_sc, rank_sc, rows_sc, tq, interleave):
    interleave = list(interleave)
    n_slc = score.shape[0]
    for k in range(n_slc):
        rows_sc[k] = score[k:k + 1, :]
    score_sc[...] = score
    sub, k_chunk = 8, 16
    row_in_group = lax.broadcasted_iota(jnp.int32, (sub, tq), 0)
    for k0 in range(0, n_slc, k_chunk):
        if interleave:
            interleave.pop(0)()
        rks = [jnp.broadcast_to(rows_sc[k], (sub, tq)) for k in range(k0, k0 + k_chunk)]
        for r0 in range(0, n_slc, sub):
            sg = score_sc[r0:r0 + sub, :]
            acc = rank_sc[r0:r0 + sub, :] if k0 else jnp.zeros((sub, tq), F32)
            for k in range(k0, k0 + k_chunk):
                rk = rks[k - k0]
                ge = jnp.where(rk >= sg, 1.0, 0.0)
                gt = jnp.where(rk > sg, 1.0, 0.0)
                if r0 > k:
                    before = ge
                elif r0 + sub - 1 < k:
                    before = gt
                else:
                    before = jnp.where(row_in_group > k - r0, ge, gt)
                acc = acc + before
            rank_sc[r0:r0 + sub, :] = acc
    for rest in interleave:
        rest()


def _attn_kernel(qT_ref, glT_ref, kc_ref, vcT_ref, ovT_ref, kx_ref, kw_ref, vT_ref, mask_ref, gq_ref, shift_ref,
                 o_ref, qw_sc, qx_sc, s_a, mo_a, mn_a, s_b, mo_b, mn_b, s_c, score_sc, rank_sc, rows_sc, mrun_sc, acc_sc,
                 *, tq, n_cmp, bounded):
    i = pl.program_id(2)
    d = HEAD_DIM
    G = NSA_GROUP
    shift = shift_ref[0]

    def lanes(g):
        return slice(g * tq, (g + 1) * tq)

    for g in range(G):
        qg = qT_ref[g * d:(g + 1) * d, :]
        qg = qg * lax.rsqrt(jnp.mean(qg * qg, axis=0, keepdims=True) + EPS) * gq_ref[...]
        qw_sc[0:d, lanes(g)] = qg.astype(BF16)
    qw_sc[d:2 * d, :] = jnp.full((d, G * tq), shift, F32).astype(BF16)

    gates = jax.nn.sigmoid(glT_ref[...])

    if not bounded:
        mrun_sc[...] = jnp.full(mrun_sc.shape, -jnp.inf, F32)
    acc_sc[...] = jnp.zeros(acc_sc.shape, F32)

    def scores(branch, k_ref, k_idx, q_sc, mask, s_buf, mo_buf, mn_buf):
        for g in range(G):
            s = jnp.dot(k_ref[k_idx], q_sc[:, lanes(g)], preferred_element_type=F32)
            if mask is not None:
                s = s + mask_ref[mask]
            s_buf[g] = s
            if not bounded:
                m_old = mrun_sc[branch, g]
                m_new = jnp.maximum(m_old, jnp.max(s, axis=0, keepdims=True))
                mo_buf[g] = m_old
                mn_buf[g] = m_new
                mrun_sc[branch, g] = m_new

    def accumulate(branch, v_idx, s_buf, mo_buf, mn_buf):
        for g in range(G):
            if bounded:
                p = jnp.exp2(s_buf[g]).astype(BF16)
                acc_sc[branch, g] += jnp.dot(vT_ref[branch, v_idx], p, preferred_element_type=F32)
            else:
                m_new = mn_buf[g]
                alpha = jnp.exp2(mo_buf[g] - m_new)
                p = jnp.exp2(s_buf[g] - m_new).astype(BF16)
                acc_sc[branch, g] = alpha * acc_sc[branch, g] + jnp.dot(vT_ref[branch, v_idx], p,
                                                                        preferred_element_type=F32)

    buf_a = (s_a, mo_a, mn_a)
    buf_b = (s_b, mo_b, mn_b)

    w_tiles = [jnp.maximum(i - 2, 0), jnp.maximum(i - 1, 0), i]
    w_masks = [jnp.where(i >= 2, 2, 3), jnp.where(i >= 1, 0, 3), 1]

    def window_head():
        scores(1, kw_ref, w_tiles[0], qw_sc, w_masks[0], *buf_a)
        scores(1, kw_ref, w_tiles[1], qw_sc, w_masks[1], *buf_b)

    def window_mid():
        accumulate(1, w_tiles[0], *buf_a)
        scores(1, kw_ref, w_tiles[2], qw_sc, w_masks[2], *buf_a)

    window_stages = [window_head, window_mid,
                     lambda: accumulate(1, w_tiles[1], *buf_b),
                     lambda: accumulate(1, w_tiles[2], *buf_a)]

    ncp = kc_ref.shape[0]
    t_idx = i * tq + lax.broadcasted_iota(jnp.int32, (ncp, tq), 1)
    n_idx = lax.broadcasted_iota(jnp.int32, (ncp, tq), 0)
    c_mask = jnp.where((n_idx * CMP_STRIDE + (CMP_BLOCK - 1) <= t_idx) & (n_idx < n_cmp), 0.0, MASKED)
    sees_any = jnp.where(t_idx[0:1, :] >= CMP_BLOCK - 1, 1.0, 0.0)
    c_max = []
    for g in range(G):
        s = jnp.dot(kc_ref[...], qw_sc[:, lanes(g)], preferred_element_type=F32) + c_mask
        s_c[g] = s
        if not bounded:
            c_max.append(jnp.max(s, axis=0, keepdims=True))
    oc = []
    for g in range(G):
        e = jnp.exp2(s_c[g]) if bounded else jnp.exp2(s_c[g] - c_max[g])
        s_c[g] = e
        oc.append(jnp.dot(vcT_ref[...], e.astype(BF16), preferred_element_type=F32))
    psum = jnp.zeros((ncp, tq), F32)
    for g in range(G):
        den = oc[g][d:d + 1, :]
        r = sees_any / jnp.where(den > 0, den, 1.0)
        psum = psum + s_c[g] * r
        o_ref[g * d:(g + 1) * d, :] = (gates[g * 3:g * 3 + 1, :] * r) * oc[g][0:d, :]

    p_hi = psum.astype(BF16)
    p_lo = (psum - p_hi.astype(F32)).astype(BF16)
    imp = (jnp.dot(ovT_ref[...], p_hi, preferred_element_type=F32)
           + jnp.dot(ovT_ref[...], p_lo, preferred_element_type=F32))
    n_slc = imp.shape[0]
    j_idx = lax.broadcasted_iota(jnp.int32, (n_slc, tq), 0)
    t_blk = (i * tq + lax.broadcasted_iota(jnp.int32, (n_slc, tq), 1)) // SLC_BLOCK
    forced = (j_idx == 0) | (j_idx == t_blk) | (j_idx == t_blk - 1)
    score = jnp.where(forced, FORCE_SCORE, jnp.where(j_idx <= t_blk, imp, -FORCE_SCORE))
    _stable_rank(score, score_sc, rank_sc, rows_sc, tq, window_stages)
    n_sel = min(SLC_TOPK, n_slc)
    bias = jnp.where(rank_sc[...] < n_sel, shift, MASKED).astype(BF16)
    if n_slc < d:
        bias = jnp.concatenate([bias, jnp.zeros((d - n_slc, tq), BF16)], axis=0)
    qx_sc[0:d, :] = qw_sc[0:d, :]
    for g in range(G):
        qx_sc[d:2 * d, lanes(g)] = bias

    def slc_mask(j):
        return jnp.where(j == i, 1, jnp.where(j > i, 3, 0))

    def slc_tile(j):
        return jnp.minimum(j, i)

    scores(0, kx_ref, 0, qx_sc, slc_mask(0), *buf_a)
    n_pairs = jnp.maximum(i - 1, 0) // 2

    def pair(j0):
        scores(0, kx_ref, j0 + 1, qx_sc, None, *buf_b)
        accumulate(0, j0, *buf_a)
        scores(0, kx_ref, j0 + 2, qx_sc, None, *buf_a)
        accumulate(0, j0 + 1, *buf_b)

    def two_pairs(kk, carry):
        pair(4 * kk)
        pair(4 * kk + 2)
        return carry

    def one_pair(kk, carry):
        pair(4 * (n_pairs // 2))
        return carry

    lax.fori_loop(0, n_pairs // 2, two_pairs, 0)
    lax.fori_loop(0, n_pairs % 2, one_pair, 0)
    j0 = 2 * n_pairs
    scores(0, kx_ref, slc_tile(j0 + 1), qx_sc, slc_mask(j0 + 1), *buf_b)
    accumulate(0, j0, *buf_a)
    scores(0, kx_ref, slc_tile(j0 + 2), qx_sc, slc_mask(j0 + 2), *buf_a)
    accumulate(0, slc_tile(j0 + 1), *buf_b)
    accumulate(0, slc_tile(j0 + 2), *buf_a)

    for branch in range(2):
        for g in range(G):
            acc = acc_sc[branch, g]
            o = acc[0:d, :] * (1.0 / acc[d:d + 1, :])
            gate = gates[g * 3 + branch + 1:g * 3 + branch + 2, :]
            o_ref[g * d:(g + 1) * d, :] += gate * o


def _attention(qT, glT, kc_x, vcT_x, ovT, kx, kw, vT, masks, gq, shift, n_cmp, bounded):
    batch, _, seq = qT.shape
    tq = ATTN_TILE
    nq = seq // tq
    hd = NSA_KV_HEADS
    gw = NSA_GROUP * HEAD_DIM
    ncp = kc_x.shape[1]
    n_slc = ovT.shape[0]

    def per_head(shape):
        return pl.BlockSpec((None, None) + shape, lambda b, h, i: (b, h) + (0,) * len(shape))

    stage = [pltpu.VMEM((NSA_GROUP, tq, tq), F32),
             pltpu.VMEM((NSA_GROUP, 1, tq), F32),
             pltpu.VMEM((NSA_GROUP, 1, tq), F32)]
    return pl.pallas_call(
        functools.partial(_attn_kernel, tq=tq, n_cmp=n_cmp, bounded=bounded),
        out_shape=jax.ShapeDtypeStruct((batch, NSA_WIDTH, seq), F32),
        grid=(batch, hd, nq),
        in_specs=[pl.BlockSpec((None, gw, tq), lambda b, h, i: (b, h, i)),
                  pl.BlockSpec((None, GATE_ROWS, tq), lambda b, h, i: (b, h, i)),
                  pl.BlockSpec((None, ncp, 2 * HEAD_DIM), lambda b, h, i: (b * hd + h, 0, 0)),
                  pl.BlockSpec((None, V_ROWS, ncp), lambda b, h, i: (b * hd + h, 0, 0)),
                  _const_spec((n_slc, ncp)),
                  per_head((nq, tq, 2 * HEAD_DIM)), per_head((nq, tq, 2 * HEAD_DIM)),
                  per_head((2, nq, V_ROWS, tq)),
                  _const_spec((4, tq, tq)),
                  _const_spec((HEAD_DIM, 1)),
                  pl.BlockSpec(memory_space=pltpu.SMEM)],
        out_specs=pl.BlockSpec((None, gw, tq), lambda b, h, i: (b, h, i)),
        scratch_shapes=[pltpu.VMEM((2 * HEAD_DIM, NSA_GROUP * tq), BF16),
                        pltpu.VMEM((2 * HEAD_DIM, NSA_GROUP * tq), BF16)] + stage + stage + [
            pltpu.VMEM((NSA_GROUP, ncp, tq), F32),
            pltpu.VMEM((n_slc, tq), F32), pltpu.VMEM((n_slc, tq), F32), pltpu.VMEM((n_slc, 1, tq), F32),
            pltpu.VMEM((2, NSA_GROUP, 1, tq), F32),
            pltpu.VMEM((2, NSA_GROUP, V_ROWS, tq), F32)],
        compiler_params=_cparams("parallel", "parallel", "arbitrary"),
        name="nsa_attention_bounded" if bounded else "nsa_attention",
    )(qT, glT, kc_x, vcT_x, ovT, kx, kw, vT, masks, gq, shift)


def _mem_kv_kernel(mem_ref, g_ref, w_ref, kg_ref, k_ref, v_ref):
    mn = _rms(mem_ref[...], g_ref[...]).astype(BF16)
    kv = jnp.dot(mn, w_ref[...], preferred_element_type=F32)
    width = MEM_HEADS * MEM_HEAD_DIM
    for hh in range(MEM_HEADS):
        cols = slice(hh * MEM_HEAD_DIM, (hh + 1) * MEM_HEAD_DIM)
        k_ref[:, cols] = _rms(kv[:, cols], kg_ref[...]).astype(BF16)
    v_ref[...] = kv[:, width:].astype(BF16)


def _mem_kv(mem, g, w_mkv, k_g):
    batch, m_len, _ = mem.shape
    width = MEM_HEADS * MEM_HEAD_DIM
    spec = pl.BlockSpec((None, m_len, width), lambda b: (b, 0, 0))
    return pl.pallas_call(
        _mem_kv_kernel,
        out_shape=(jax.ShapeDtypeStruct((batch, m_len, width), BF16),) * 2,
        grid=(batch,),
        in_specs=[pl.BlockSpec((None, m_len, D_MODEL), lambda b: (b, 0, 0)),
                  _const_spec((1, D_MODEL)), _const_spec((D_MODEL, 2 * width)),
                  _const_spec((1, MEM_HEAD_DIM))],
        out_specs=(spec, spec),
        compiler_params=_cparams("parallel"),
        name="mem_kv",
    )(mem, g, w_mkv, k_g)


def _tail_kernel(a_ref, bT_ref, x_ref, gb_ref, wout_ref, gm_ref, wq_ref, qg_ref, k_ref, v_ref, wo_ref,
                 gf_ref, w1_ref, w2_ref, o_ref):
    bT = bT_ref[...]
    bn = (bT * lax.rsqrt(jnp.mean(bT * bT, axis=0, keepdims=True) + EPS) * gb_ref[...]).astype(BF16)
    x = x_ref[...] + jnp.dot(a_ref[...], wout_ref[:SG_WIDTH, :], preferred_element_type=F32)
    x = x + lax.dot_general(bn, wout_ref[SG_WIDTH:, :], TN_DIMS, preferred_element_type=F32)

    h = _rms(x, gm_ref[...]).astype(BF16)
    q = jnp.dot(h, wq_ref[...], preferred_element_type=F32)
    outs = []
    for hh in range(MEM_HEADS):
        cols = slice(hh * MEM_HEAD_DIM, (hh + 1) * MEM_HEAD_DIM)
        qh = (_rms(q[:, cols], qg_ref[...]) * MEM_HEAD_DIM ** -0.5).astype(BF16)
        s = lax.dot_general(qh, k_ref[:, cols], NT_DIMS, preferred_element_type=F32)
        e = jnp.exp(s - jnp.max(s, axis=-1, keepdims=True))
        p = e * (1.0 / jnp.sum(e, axis=-1, keepdims=True))
        outs.append(jnp.dot(p.astype(BF16), v_ref[:, cols], preferred_element_type=F32))
    o = jnp.concatenate(outs, axis=-1).astype(BF16)
    x = x + jnp.dot(o, wo_ref[...], preferred_element_type=F32)

    h = _rms(x, gf_ref[...]).astype(BF16)
    step = 1024
    for c0 in range(0, D_FF, step):
        a = jnp.maximum(jnp.dot(h, w1_ref[:, c0:c0 + step], preferred_element_type=F32), 0.0)
        x = x + jnp.dot((a * a).astype(BF16), w2_ref[c0:c0 + step, :], preferred_element_type=F32)
    o_ref[...] = x


def _tail(a_n, bT, x2, g_b, w_out, g_mem, w_mq, q_g, mk, mv, w_mo, g_ffn, w1, w2, seq):
    n = x2.shape[0]
    tm = 512
    per_b = seq // tm
    m_len = mk.shape[1]
    width = MEM_HEADS * MEM_HEAD_DIM

    def weight(shape):
        return pl.BlockSpec(shape, lambda i: (0,) * len(shape), pipeline_mode=pl.Buffered(1))

    kv_spec = pl.BlockSpec((None, m_len, width), lambda i: (i // per_b, 0, 0))
    return pl.pallas_call(
        _tail_kernel,
        out_shape=jax.ShapeDtypeStruct((n, D_MODEL), F32),
        grid=(n // tm,),
        in_specs=[pl.BlockSpec((tm, SG_WIDTH), lambda i: (i, 0)),
                  pl.BlockSpec((None, NSA_WIDTH, tm), lambda i: (i // per_b, 0, i % per_b)),
                  pl.BlockSpec((tm, D_MODEL), lambda i: (i, 0)),
                  _const_spec((NSA_WIDTH, 1)), weight((D_MODEL, D_MODEL)),
                  _const_spec((1, D_MODEL)), weight((D_MODEL, width)), _const_spec((1, MEM_HEAD_DIM)),
                  kv_spec, kv_spec, weight((width, D_MODEL)),
                  _const_spec((1, D_MODEL)), weight((D_MODEL, D_FF)), weight((D_FF, D_MODEL))],
        out_specs=pl.BlockSpec((tm, D_MODEL), lambda i: (i, 0)),
        compiler_params=pltpu.CompilerParams(dimension_semantics=("parallel",), vmem_limit_bytes=TAIL_VMEM_LIMIT),
        name="tail",
    )(a_n, bT, x2, g_b, w_out, g_mem, w_mq, q_g, mk, mv, w_mo, g_ffn, w1, w2)


def _overlap_T(n_cmp_pad, n_cmp, n_slc):
    cs = jnp.arange(n_cmp_pad) * CMP_STRIDE
    ss = jnp.arange(n_slc) * SLC_BLOCK
    ov = jnp.clip(jnp.minimum(cs[None, :] + CMP_BLOCK, ss[:, None] + SLC_BLOCK)
                  - jnp.maximum(cs[None, :], ss[:, None]), 0, None).astype(F32) / CMP_BLOCK
    ov = jnp.where(jnp.arange(n_cmp_pad)[None, :] < n_cmp, ov, 0.0)
    return ov.astype(BF16)


def kernel(x, mem, norm_mix_g, w_in, sg_ln_g, sg_ln_b, sg_w, sg_b, q_norm_g, k_norm_g, cmp_pos, cmp_w1, cmp_b1, cmp_w2, cmp_b2, mix_out_g, w_out, norm_mem_g, mem_kv_norm_g, w_mq, w_mkv, mem_q_norm_g, mem_k_norm_g, w_mo, norm_ffn_g, w_ff1, w_ff2):
    batch, seq, _ = x.shape
    depth = w_in.shape[0]
    n = batch * seq
    tq = ATTN_TILE
    assert WINDOW == 2 * tq and seq % 512 == 0
    n_chunk = seq // CMP_STRIDE
    n_cmp = n_chunk - 1
    n_slc = seq // SLC_BLOCK
    assert n_slc <= HEAD_DIM
    ovT = _overlap_T(n_chunk, n_cmp, n_slc)
    eye = jnp.eye(HEAD_DIM, dtype=BF16)
    row = lambda a: a.reshape(1, -1)
    r_idx = lax.broadcasted_iota(jnp.int32, (tq, tq), 0)
    c_idx = lax.broadcasted_iota(jnp.int32, (tq, tq), 1)
    masks = jnp.stack([jnp.zeros((tq, tq), F32), jnp.where(r_idx <= c_idx, 0.0, MASKED),
                       jnp.where(r_idx > c_idx, 0.0, MASKED), jnp.full((tq, tq), MASKED, F32)])

    x2 = x.reshape(n, D_MODEL)
    for l in range(depth):
        w = w_in[l].astype(BF16)
        w_gT = jnp.pad(w[:, COL_GATE:].T.reshape(NSA_KV_HEADS, GATES_PER_HEAD, D_MODEL),
                       ((0, 0), (0, GATE_ROWS - GATES_PER_HEAD), (0, 0))).reshape(NSA_KV_HEADS * GATE_ROWS, D_MODEL)
        z_uv, z_kv, qT, glT = _in_proj(x2, row(norm_mix_g[l]), w[:, :COL_Q], w[:, COL_KV:COL_GATE],
                                       w[:, COL_Q:COL_KV].T, w_gT, batch, seq)

        bs_full = jnp.repeat(sg_b[l].T, HEAD_DIM, axis=1)
        a_n = _sgu(z_uv, row(sg_ln_g[l]), row(sg_ln_b[l]), sg_w[l], bs_full, row(mix_out_g[l, 0]))

        kx, kw, vT, k_cmp, v_cmp = _kv_prep(z_kv, row(k_norm_g[l, 1]), row(k_norm_g[l, 2]), eye, batch, seq)
        tiles = lambda a: a.reshape(batch, NSA_KV_HEADS, seq // tq, tq, a.shape[-1])
        chunks = lambda a: a.reshape(batch * NSA_KV_HEADS, n_chunk, CMP_STRIDE * HEAD_DIM)
        pos8 = lambda p: jnp.broadcast_to(p.reshape(1, -1), (8, CMP_BLOCK * HEAD_DIM))
        kc = _compress(chunks(k_cmp), pos8(cmp_pos[l, 0]), cmp_w1[l, 0], row(cmp_b1[l, 0]), cmp_w2[l, 0],
                       row(cmp_b2[l, 0]), row(k_norm_g[l, 0]), True)
        vc = _compress(chunks(v_cmp), pos8(cmp_pos[l, 1]), cmp_w1[l, 1], row(cmp_b1[l, 1]), cmp_w2[l, 1],
                       row(cmp_b2[l, 1]), row(k_norm_g[l, 0]), False)
        vcT = vc.transpose(0, 2, 1)

        gq = (q_norm_g[l] * (HEAD_DIM ** -0.5 * LOG2E)).reshape(HEAD_DIM, 1)
        kx, kw = tiles(kx), tiles(kw)
        peak = lambda g: jnp.max(jnp.abs(g))
        bound = HEAD_DIM ** 0.5 * LOG2E * peak(q_norm_g[l]) * peak(k_norm_g[l])

        kc_x = jnp.concatenate([kc, jnp.ones_like(kc[..., :1]), jnp.zeros_like(kc[..., 1:])], axis=-1).astype(BF16)
        vcT_x = jnp.concatenate([vcT, jnp.ones_like(vcT[:, :1]),
                                 jnp.zeros_like(vcT[:, :BF16_SUBLANES - 1])], axis=1).astype(BF16)
        ops = (qT, glT, kc_x, vcT_x, ovT, kx, kw, vT, masks, gq)
        oT = lax.cond(2.0 * bound <= MAX_SCORE_SPAN,
                      lambda ops, bound: _attention(*ops, (-bound).reshape(1), n_cmp, True),
                      lambda ops, bound: _attention(*ops, jnp.zeros((1,), F32), n_cmp, False),
                      ops, bound)

        mk, mv = _mem_kv(mem, row(mem_kv_norm_g[l]), w_mkv[l].astype(BF16), row(mem_k_norm_g[l]))
        x2 = _tail(a_n, oT, x2, mix_out_g[l, 1].reshape(NSA_WIDTH, 1), w_out[l].astype(BF16),
                   row(norm_mem_g[l]), w_mq[l].astype(BF16), row(mem_q_norm_g[l]), mk, mv, w_mo[l].astype(BF16),
                   row(norm_ffn_g[l]), w_ff1[l].astype(BF16), w_ff2[l].astype(BF16), seq)
    return x2.reshape(batch, seq, D_MODEL)
```

```python
import functools

import jax
import jax.numpy as jnp
from jax import lax
from jax.experimental import pallas as pl
from jax.experimental.pallas import tpu as pltpu

F32 = jnp.float32
BF16 = jnp.bfloat16

D_MODEL = 1024
HEAD_DIM = 64
SG_WIDTH = 512
SG_GROUPS = 8
SG_CHUNK = 128
NSA_WIDTH = 512
NSA_KV_HEADS = 2
NSA_GROUP = 4
KV_WIDTH = NSA_KV_HEADS * HEAD_DIM
N_BRANCH = 3
CMP_BLOCK = 32
CMP_STRIDE = 16
CMP_HIDDEN = 256
SLC_BLOCK = 64
SLC_TOPK = 16
WINDOW = 512
FORCE_SCORE = 1e4
MEM_HEADS = 4
MEM_HEAD_DIM = 128
D_FF = 4 * D_MODEL
EPS = 1e-6

COL_Q = 2 * SG_WIDTH
COL_KV = COL_Q + NSA_WIDTH
COL_GATE = COL_KV + 6 * KV_WIDTH
GATES_PER_HEAD = NSA_GROUP * N_BRANCH
GATE_ROWS = 16

MASKED = -1e30
LOG2E = 1.4426950408889634
MAX_SCORE_SPAN = 100.0
BF16_SUBLANES = 16
V_ROWS = HEAD_DIM + BF16_SUBLANES
ATTN_TILE = 256
VMEM_LIMIT = 48 * 1024 * 1024
TAIL_VMEM_LIMIT = 56 * 1024 * 1024

NT_DIMS = (((1,), (1,)), ((), ()))
TN_DIMS = (((0,), (0,)), ((), ()))


def _cparams(*sem):
    return pltpu.CompilerParams(dimension_semantics=sem, vmem_limit_bytes=VMEM_LIMIT)


def _rms(x, g):
    return x * lax.rsqrt(jnp.mean(x * x, axis=-1, keepdims=True) + EPS) * g


def _const_spec(shape):
    return pl.BlockSpec(shape, lambda *_: (0,) * len(shape))


def _dot_3pass(a, b):
    a_hi = a.astype(BF16)
    b_hi = b.astype(BF16)
    a_lo = (a - a_hi.astype(F32)).astype(BF16)
    b_lo = (b - b_hi.astype(F32)).astype(BF16)
    dot = functools.partial(jnp.dot, preferred_element_type=F32)
    return dot(a_hi, b_hi) + (dot(a_hi, b_lo) + dot(a_lo, b_hi))


def _front_kernel(x_ref, g_ref, wuv_ref, wk_ref, wvT_ref, wqT_ref, wgT_ref,
                  lng_ref, lnb_ref, ws_ref, bs_ref, og_ref, gs_ref, gw_ref,
                  a_ref, kx_ref, kw_ref, vT_ref, kc_ref, vc_ref, qT_ref, gT_ref, h_sc, *, tm, seq):
    h_sc[...] = _rms(x_ref[...], g_ref[...]).astype(BF16)
    tq = vT_ref.shape[-1]

    def queries():
        qT_ref[...] = lax.dot_general(wqT_ref[...], h_sc[...], NT_DIMS, preferred_element_type=F32)
        gT_ref[...] = lax.dot_general(wgT_ref[...], h_sc[...], NT_DIMS, preferred_element_type=F32)

    def values():
        vT = lax.dot_general(wvT_ref[...], h_sc[...], NT_DIMS, preferred_element_type=F32).astype(BF16)
        tail = jnp.where(lax.broadcasted_iota(jnp.int32, (BF16_SUBLANES, tq), 0) == 0, 1.0, 0.0).astype(BF16)
        for branch in range(2):
            for hd in range(NSA_KV_HEADS):
                r0 = (branch * NSA_KV_HEADS + hd) * HEAD_DIM
                for t in range(tm // tq):
                    vT_ref[hd, branch, t, :HEAD_DIM, :] = vT[r0:r0 + HEAD_DIM, t * tq:(t + 1) * tq]
                    vT_ref[hd, branch, t, HEAD_DIM:, :] = tail

    def keys(half):
        rows = slice(half * (tm // 2), (half + 1) * (tm // 2))
        kk = jnp.dot(h_sc[rows, :], wk_ref[...], preferred_element_type=F32)
        t0 = (pl.program_id(0) * tm) % seq + half * (tm // 2)
        blk = (t0 + lax.broadcasted_iota(jnp.int32, (tm // 2, SLC_BLOCK), 0)) // SLC_BLOCK
        onehot = jnp.where(blk == lax.broadcasted_iota(jnp.int32, (tm // 2, SLC_BLOCK), 1), 1.0, 0.0).astype(BF16)
        for hd in range(NSA_KV_HEADS):
            def part(idx):
                c0 = idx * KV_WIDTH + hd * HEAD_DIM
                return kk[:, c0:c0 + HEAD_DIM]
            kc_ref[hd, rows, :] = part(0)
            vc_ref[hd, rows, :] = part(1)
            kx_ref[hd, rows, :HEAD_DIM] = _rms(part(2), gs_ref[...]).astype(BF16)
            kx_ref[hd, rows, HEAD_DIM:] = onehot
            kw_ref[hd, rows, :HEAD_DIM] = _rms(part(3), gw_ref[...]).astype(BF16)
            kw_ref[hd, rows, HEAD_DIM:] = onehot

    row = lax.broadcasted_iota(jnp.int32, (SG_CHUNK, SG_CHUNK), 0)
    col = lax.broadcasted_iota(jnp.int32, (SG_CHUNK, SG_CHUNK), 1)
    causal = col <= row
    w = [jnp.where(causal, ws_ref[g], 0.0).astype(BF16) for g in range(SG_GROUPS)]
    between = [queries, values, lambda: keys(0), lambda: keys(1)]
    for c in range(tm // SG_CHUNK):
        rows = slice(c * SG_CHUNK, (c + 1) * SG_CHUNK)
        uv = jnp.dot(h_sc[rows, :], wuv_ref[...], preferred_element_type=F32)
        between[c]()
        u = jax.nn.gelu(uv[:, :SG_WIDTH])
        v = jax.nn.gelu(uv[:, SG_WIDTH:])
        mu = jnp.mean(v, axis=-1, keepdims=True)
        vc = v - mu
        vn = vc * lax.rsqrt(jnp.mean(vc * vc, axis=-1, keepdims=True) + EPS) * lng_ref[...] + lnb_ref[...]
        vb = vn.astype(BF16)
        s = jnp.concatenate(
            [jnp.dot(w[g], vb[:, g * HEAD_DIM:(g + 1) * HEAD_DIM], preferred_element_type=F32)
             for g in range(SG_GROUPS)], axis=-1) + bs_ref[...]
        a_ref[rows, :] = _rms(u * s, og_ref[...]).astype(BF16)


def _front(x2, g, w_uv, w_k, w_vT, w_qT, w_gT, ln_g, ln_b, w_s, bs_full, out_g, g_slc, g_win, batch, seq):
    n = x2.shape[0]
    tm = 512
    tq = ATTN_TILE
    per_b = seq // tm
    hd = NSA_KV_HEADS
    g_rows = w_gT.shape[0]

    def per_head(width):
        return pl.BlockSpec((None, hd, tm, width), lambda i: (i // per_b, 0, i % per_b, 0))

    def per_head_shape(width, dtype):
        return jax.ShapeDtypeStruct((batch, hd, seq, width), dtype)

    def cols(height):
        return pl.BlockSpec((None, height, tm), lambda i: (i // per_b, 0, i % per_b))

    return pl.pallas_call(
        functools.partial(_front_kernel, tm=tm, seq=seq),
        out_shape=(jax.ShapeDtypeStruct((n, SG_WIDTH), BF16),
                   per_head_shape(2 * HEAD_DIM, BF16), per_head_shape(2 * HEAD_DIM, BF16),
                   jax.ShapeDtypeStruct((batch, hd, 2, seq // tq, V_ROWS, tq), BF16),
                   per_head_shape(HEAD_DIM, F32), per_head_shape(HEAD_DIM, F32),
                   jax.ShapeDtypeStruct((batch, NSA_WIDTH, seq), F32),
                   jax.ShapeDtypeStruct((batch, g_rows, seq), F32)),
        grid=(n // tm,),
        in_specs=[pl.BlockSpec((tm, D_MODEL), lambda i: (i, 0)), _const_spec((1, D_MODEL)),
                  _const_spec((D_MODEL, 2 * SG_WIDTH)), _const_spec((D_MODEL, 4 * KV_WIDTH)),
                  _const_spec((2 * KV_WIDTH, D_MODEL)), _const_spec((NSA_WIDTH, D_MODEL)),
                  _const_spec((g_rows, D_MODEL)),
                  _const_spec((1, SG_WIDTH)), _const_spec((1, SG_WIDTH)),
                  _const_spec((SG_GROUPS, SG_CHUNK, SG_CHUNK)),
                  _const_spec((SG_CHUNK, SG_WIDTH)), _const_spec((1, SG_WIDTH)),
                  _const_spec((1, HEAD_DIM)), _const_spec((1, HEAD_DIM))],
        out_specs=(pl.BlockSpec((tm, SG_WIDTH), lambda i: (i, 0)),
                   per_head(2 * HEAD_DIM), per_head(2 * HEAD_DIM),
                   pl.BlockSpec((None, hd, 2, tm // tq, V_ROWS, tq), lambda i: (i // per_b, 0, 0, i % per_b, 0, 0)),
                   per_head(HEAD_DIM), per_head(HEAD_DIM), cols(NSA_WIDTH), cols(g_rows)),
        scratch_shapes=[pltpu.VMEM((tm, D_MODEL), BF16)],
        compiler_params=_cparams("parallel"),
        name="front",
    )(x2, g, w_uv, w_k, w_vT, w_qT, w_gT, ln_g, ln_b, w_s, bs_full, out_g, g_slc, g_win)


def _cmp_kernel(c_ref, pos_ref, w1_ref, b1_ref, w2_ref, b2_ref, g_ref, o_ref, *, norm):
    c = c_ref[...]
    nb, half = c.shape
    first = _dot_3pass(c, w1_ref[:half, :])
    second = _dot_3pass(c, w1_ref[half:, :])
    pos = _dot_3pass(pos_ref[...], w1_ref[...])[0:1, :]
    h1 = jax.nn.gelu(first + pltpu.roll(second, nb - 1, 0) + pos + b1_ref[...])
    y = _dot_3pass(h1, w2_ref[...]) + b2_ref[...]
    if norm:
        y = _rms(y, g_ref[...])
    o_ref[...] = y


def _compress(chunks, pos8, w1, b1, w2, b2, g, norm):
    bh, nb, half = chunks.shape
    return pl.pallas_call(
        functools.partial(_cmp_kernel, norm=norm),
        out_shape=jax.ShapeDtypeStruct((bh, nb, HEAD_DIM), F32),
        grid=(bh,),
        in_specs=[pl.BlockSpec((None, nb, half), lambda i: (i, 0, 0)),
                  _const_spec((8, 2 * half)), _const_spec((2 * half, CMP_HIDDEN)), _const_spec((1, CMP_HIDDEN)),
                  _const_spec((CMP_HIDDEN, HEAD_DIM)), _const_spec((1, HEAD_DIM)), _const_spec((1, HEAD_DIM))],
        out_specs=pl.BlockSpec((None, nb, HEAD_DIM), lambda i: (i, 0, 0)),
        compiler_params=_cparams("parallel"),
        name="compress_k" if norm else "compress_v",
    )(chunks, pos8, w1, b1, w2, b2, g)


def _stable_rank(score, score_sc, rank_sc, rows_sc, tq, interleave):
    interleave = list(interleave)
    n_slc = score.shape[0]
    for k in range(n_slc):
        rows_sc[k] = score[k:k + 1, :]
    score_sc[...] = score
    sub, k_chunk = 8, 16
    row_in_group = lax.broadcasted_iota(jnp.int32, (sub, tq), 0)
    for k0 in range(0, n_slc, k_chunk):
        if interleave:
            interleave.pop(0)()
        rks = [jnp.broadcast_to(rows_sc[k], (sub, tq)) for k in range(k0, k0 + k_chunk)]
        for r0 in range(0, n_slc, sub):
            sg = score_sc[r0:r0 + sub, :]
            acc = rank_sc[r0:r0 + sub, :] if k0 else jnp.zeros((sub, tq), F32)
            for k in range(k0, k0 + k_chunk):
                rk = rks[k - k0]
                ge = jnp.where(rk >= sg, 1.0, 0.0)
                gt = jnp.where(rk > sg, 1.0, 0.0)
                if r0 > k:
                    before = ge
                elif r0 + sub - 1 < k:
                    before = gt
                else:
                    before = jnp.where(row_in_group > k - r0, ge, gt)
                acc = acc + before
            rank_sc[r0:r0 + sub, :] = acc
    for rest in interleave:
        rest()


def _attn_kernel(qT_ref, glT_ref, kc_ref, vcT_ref, ovT_ref, kx_ref, kw_ref, vT_ref, mask_ref, gq_ref, shift_ref,
                 o_ref, qw_sc, qx_sc, s_a, mo_a, mn_a, s_b, mo_b, mn_b, s_c, score_sc, rank_sc, rows_sc, mrun_sc, acc_sc,
                 *, tq, n_cmp, bounded):
    i = pl.program_id(2)
    d = HEAD_DIM
    G = NSA_GROUP
    shift = shift_ref[0]

    def lanes(g):
        return slice(g * tq, (g + 1) * tq)

    for g in range(G):
        qg = qT_ref[g * d:(g + 1) * d, :]
        qg = qg * lax.rsqrt(jnp.mean(qg * qg, axis=0, keepdims=True) + EPS) * gq_ref[...]
        qw_sc[0:d, lanes(g)] = qg.astype(BF16)
    qw_sc[d:2 * d, :] = jnp.full((d, G * tq), shift, F32).astype(BF16)

    gates = jax.nn.sigmoid(glT_ref[...])

    if not bounded:
        mrun_sc[...] = jnp.full(mrun_sc.shape, -jnp.inf, F32)
    acc_sc[...] = jnp.zeros(acc_sc.shape, F32)

    def scores(branch, k_ref, k_idx, q_sc, mask, s_buf, mo_buf, mn_buf):
        for g in range(G):
            s = jnp.dot(k_ref[k_idx], q_sc[:, lanes(g)], preferred_element_type=F32)
            if mask is not None:
                s = s + mask_ref[mask]
            s_buf[g] = s
            if not bounded:
                m_old = mrun_sc[branch, g]
                m_new = jnp.maximum(m_old, jnp.max(s, axis=0, keepdims=True))
                mo_buf[g] = m_old
                mn_buf[g] = m_new
                mrun_sc[branch, g] = m_new

    def accumulate(branch, v_idx, s_buf, mo_buf, mn_buf):
        for g in range(G):
            if bounded:
                p = jnp.exp2(s_buf[g]).astype(BF16)
                acc_sc[branch, g] += jnp.dot(vT_ref[branch, v_idx], p, preferred_element_type=F32)
            else:
                m_new = mn_buf[g]
                alpha = jnp.exp2(mo_buf[g] - m_new)
                p = jnp.exp2(s_buf[g] - m_new).astype(BF16)
                acc_sc[branch, g] = alpha * acc_sc[branch, g] + jnp.dot(vT_ref[branch, v_idx], p,
                                                                        preferred_element_type=F32)

    buf_a = (s_a, mo_a, mn_a)
    buf_b = (s_b, mo_b, mn_b)

    w_tiles = [jnp.maximum(i - 2, 0), jnp.maximum(i - 1, 0), i]
    w_masks = [jnp.where(i >= 2, 2, 3), jnp.where(i >= 1, 0, 3), 1]

    def window_head():
        scores(1, kw_ref, w_tiles[0], qw_sc, w_masks[0], *buf_a)
        scores(1, kw_ref, w_tiles[1], qw_sc, w_masks[1], *buf_b)

    def window_mid():
        accumulate(1, w_tiles[0], *buf_a)
        scores(1, kw_ref, w_tiles[2], qw_sc, w_masks[2], *buf_a)

    window_stages = [window_head, window_mid,
                     lambda: accumulate(1, w_tiles[1], *buf_b),
                     lambda: accumulate(1, w_tiles[2], *buf_a)]

    ncp = kc_ref.shape[0]
    t_idx = i * tq + lax.broadcasted_iota(jnp.int32, (ncp, tq), 1)
    n_idx = lax.broadcasted_iota(jnp.int32, (ncp, tq), 0)
    c_mask = jnp.where((n_idx * CMP_STRIDE + (CMP_BLOCK - 1) <= t_idx) & (n_idx < n_cmp), 0.0, MASKED)
    sees_any = jnp.where(t_idx[0:1, :] >= CMP_BLOCK - 1, 1.0, 0.0)
    c_max = []
    for g in range(G):
        s = jnp.dot(kc_ref[...], qw_sc[:, lanes(g)], preferred_element_type=F32) + c_mask
        s_c[g] = s
        if not bounded:
            c_max.append(jnp.max(s, axis=0, keepdims=True))
    oc = []
    for g in range(G):
        e = jnp.exp2(s_c[g]) if bounded else jnp.exp2(s_c[g] - c_max[g])
        s_c[g] = e
        oc.append(jnp.dot(vcT_ref[...], e.astype(BF16), preferred_element_type=F32))
    psum = jnp.zeros((ncp, tq), F32)
    for g in range(G):
        den = oc[g][d:d + 1, :]
        r = sees_any / jnp.where(den > 0, den, 1.0)
        psum = psum + s_c[g] * r
        o_ref[g * d:(g + 1) * d, :] = (gates[g * 3:g * 3 + 1, :] * r) * oc[g][0:d, :]

    p_hi = psum.astype(BF16)
    p_lo = (psum - p_hi.astype(F32)).astype(BF16)
    imp = (jnp.dot(ovT_ref[...], p_hi, preferred_element_type=F32)
           + jnp.dot(ovT_ref[...], p_lo, preferred_element_type=F32))
    n_slc = imp.shape[0]
    j_idx = lax.broadcasted_iota(jnp.int32, (n_slc, tq), 0)
    t_blk = (i * tq + lax.broadcasted_iota(jnp.int32, (n_slc, tq), 1)) // SLC_BLOCK
    forced = (j_idx == 0) | (j_idx == t_blk) | (j_idx == t_blk - 1)
    score = jnp.where(forced, FORCE_SCORE, jnp.where(j_idx <= t_blk, imp, -FORCE_SCORE))
    _stable_rank(score, score_sc, rank_sc, rows_sc, tq, window_stages)
    n_sel = min(SLC_TOPK, n_slc)
    bias = jnp.where(rank_sc[...] < n_sel, shift, MASKED).astype(BF16)
    if n_slc < d:
        bias = jnp.concatenate([bias, jnp.zeros((d - n_slc, tq), BF16)], axis=0)
    qx_sc[0:d, :] = qw_sc[0:d, :]
    for g in range(G):
        qx_sc[d:2 * d, lanes(g)] = bias

    def slc_mask(j):
        return jnp.where(j == i, 1, jnp.where(j > i, 3, 0))

    def slc_tile(j):
        return jnp.minimum(j, i)

    scores(0, kx_ref, 0, qx_sc, slc_mask(0), *buf_a)
    n_pairs = jnp.maximum(i - 1, 0) // 2

    def pair(j0):
        scores(0, kx_ref, j0 + 1, qx_sc, None, *buf_b)
        accumulate(0, j0, *buf_a)
        scores(0, kx_ref, j0 + 2, qx_sc, None, *buf_a)
        accumulate(0, j0 + 1, *buf_b)

    def two_pairs(kk, carry):
        pair(4 * kk)
        pair(4 * kk + 2)
        return carry

    def one_pair(kk, carry):
        pair(4 * (n_pairs // 2))
        return carry

    lax.fori_loop(0, n_pairs // 2, two_pairs, 0)
    lax.fori_loop(0, n_pairs % 2, one_pair, 0)
    j0 = 2 * n_pairs
    scores(0, kx_ref, slc_tile(j0 + 1), qx_sc, slc_mask(j0 + 1), *buf_b)
    accumulate(0, j0, *buf_a)
    scores(0, kx_ref, slc_tile(j0 + 2), qx_sc, slc_mask(j0 + 2), *buf_a)
    accumulate(0, slc_tile(j0 + 1), *buf_b)
    accumulate(0, slc_tile(j0 + 2), *buf_a)

    for branch in range(2):
        for g in range(G):
            acc = acc_sc[branch, g]
            o = acc[0:d, :] * (1.0 / acc[d:d + 1, :])
            gate = gates[g * 3 + branch + 1:g * 3 + branch + 2, :]
            o_ref[g * d:(g + 1) * d, :] += gate * o


def _attention(qT, glT, kc_x, vcT_x, ovT, kx, kw, vT, masks, gq, shift, n_cmp, bounded):
    batch, _, seq = qT.shape
    tq = ATTN_TILE
    nq = seq // tq
    hd = NSA_KV_HEADS
    gw = NSA_GROUP * HEAD_DIM
    ncp = kc_x.shape[1]
    n_slc = ovT.shape[0]

    def per_head(shape):
        return pl.BlockSpec((None, None) + shape, lambda b, h, i: (b, h) + (0,) * len(shape))

    stage = [pltpu.VMEM((NSA_GROUP, tq, tq), F32),
             pltpu.VMEM((NSA_GROUP, 1, tq), F32),
             pltpu.VMEM((NSA_GROUP, 1, tq), F32)]
    return pl.pallas_call(
        functools.partial(_attn_kernel, tq=tq, n_cmp=n_cmp, bounded=bounded),
        out_shape=jax.ShapeDtypeStruct((batch, NSA_WIDTH, seq), F32),
        grid=(batch, hd, nq),
        in_specs=[pl.BlockSpec((None, gw, tq), lambda b, h, i: (b, h, i)),
                  pl.BlockSpec((None, GATE_ROWS, tq), lambda b, h, i: (b, h, i)),
                  pl.BlockSpec((None, ncp, 2 * HEAD_DIM), lambda b, h, i: (b * hd + h, 0, 0)),
                  pl.BlockSpec((None, V_ROWS, ncp), lambda b, h, i: (b * hd + h, 0, 0)),
                  _const_spec((n_slc, ncp)),
                  per_head((nq, tq, 2 * HEAD_DIM)), per_head((nq, tq, 2 * HEAD_DIM)),
                  per_head((2, nq, V_ROWS, tq)),
                  _const_spec((4, tq, tq)),
                  _const_spec((HEAD_DIM, 1)),
                  pl.BlockSpec(memory_space=pltpu.SMEM)],
        out_specs=pl.BlockSpec((None, gw, tq), lambda b, h, i: (b, h, i)),
        scratch_shapes=[pltpu.VMEM((2 * HEAD_DIM, NSA_GROUP * tq), BF16),
                        pltpu.VMEM((2 * HEAD_DIM, NSA_GROUP * tq), BF16)] + stage + stage + [
            pltpu.VMEM((NSA_GROUP, ncp, tq), F32),
            pltpu.VMEM((n_slc, tq), F32), pltpu.VMEM((n_slc, tq), F32), pltpu.VMEM((n_slc, 1, tq), F32),
            pltpu.VMEM((2, NSA_GROUP, 1, tq), F32),
            pltpu.VMEM((2, NSA_GROUP, V_ROWS, tq), F32)],
        compiler_params=_cparams("parallel", "parallel", "arbitrary"),
        name="nsa_attention_bounded" if bounded else "nsa_attention",
    )(qT, glT, kc_x, vcT_x, ovT, kx, kw, vT, masks, gq, shift)


def _mem_kv_kernel(mem_ref, g_ref, w_ref, kg_ref, k_ref, v_ref):
    mn = _rms(mem_ref[...], g_ref[...]).astype(BF16)
    kv = jnp.dot(mn, w_ref[...], preferred_element_type=F32)
    width = MEM_HEADS * MEM_HEAD_DIM
    for hh in range(MEM_HEADS):
        cols = slice(hh * MEM_HEAD_DIM, (hh + 1) * MEM_HEAD_DIM)
        k_ref[:, cols] = _rms(kv[:, cols], kg_ref[...]).astype(BF16)
    v_ref[...] = kv[:, width:].astype(BF16)


def _mem_kv(mem, g, w_mkv, k_g):
    batch, m_len, _ = mem.shape
    width = MEM_HEADS * MEM_HEAD_DIM
    spec = pl.BlockSpec((None, m_len, width), lambda b: (b, 0, 0))
    return pl.pallas_call(
        _mem_kv_kernel,
        out_shape=(jax.ShapeDtypeStruct((batch, m_len, width), BF16),) * 2,
        grid=(batch,),
        in_specs=[pl.BlockSpec((None, m_len, D_MODEL), lambda b: (b, 0, 0)),
                  _const_spec((1, D_MODEL)), _const_spec((D_MODEL, 2 * width)),
                  _const_spec((1, MEM_HEAD_DIM))],
        out_specs=(spec, spec),
        compiler_params=_cparams("parallel"),
        name="mem_kv",
    )(mem, g, w_mkv, k_g)


def _tail_kernel(a_ref, bT_ref, x_ref, gb_ref, wout_ref, gm_ref, wq_ref, qg_ref, k_ref, v_ref, wo_ref,
                 gf_ref, w1_ref, w2_ref, o_ref):
    bT = bT_ref[...]
    bn = (bT * lax.rsqrt(jnp.mean(bT * bT, axis=0, keepdims=True) + EPS) * gb_ref[...]).astype(BF16)
    x = x_ref[...] + jnp.dot(a_ref[...], wout_ref[:SG_WIDTH, :], preferred_element_type=F32)
    x = x + lax.dot_general(bn, wout_ref[SG_WIDTH:, :], TN_DIMS, preferred_element_type=F32)

    h = _rms(x, gm_ref[...]).astype(BF16)
    q = jnp.dot(h, wq_ref[...], preferred_element_type=F32)
    outs = []
    for hh in range(MEM_HEADS):
        cols = slice(hh * MEM_HEAD_DIM, (hh + 1) * MEM_HEAD_DIM)
        qh = (_rms(q[:, cols], qg_ref[...]) * MEM_HEAD_DIM ** -0.5).astype(BF16)
        s = lax.dot_general(qh, k_ref[:, cols], NT_DIMS, preferred_element_type=F32)
        e = jnp.exp(s - jnp.max(s, axis=-1, keepdims=True))
        p = e * (1.0 / jnp.sum(e, axis=-1, keepdims=True))
        outs.append(jnp.dot(p.astype(BF16), v_ref[:, cols], preferred_element_type=F32))
    o = jnp.concatenate(outs, axis=-1).astype(BF16)
    x = x + jnp.dot(o, wo_ref[...], preferred_element_type=F32)

    h = _rms(x, gf_ref[...]).astype(BF16)
    step = 1024
    for c0 in range(0, D_FF, step):
        a = jnp.maximum(jnp.dot(h, w1_ref[:, c0:c0 + step], preferred_element_type=F32), 0.0)
        x = x + jnp.dot((a * a).astype(BF16), w2_ref[c0:c0 + step, :], preferred_element_type=F32)
    o_ref[...] = x


def _tail(a_n, bT, x2, g_b, w_out, g_mem, w_mq, q_g, mk, mv, w_mo, g_ffn, w1, w2, seq):
    n = x2.shape[0]
    tm = 512
    per_b = seq // tm
    m_len = mk.shape[1]
    width = MEM_HEADS * MEM_HEAD_DIM

    def weight(shape):
        return pl.BlockSpec(shape, lambda i: (0,) * len(shape), pipeline_mode=pl.Buffered(1))

    kv_spec = pl.BlockSpec((None, m_len, width), lambda i: (i // per_b, 0, 0))
    return pl.pallas_call(
        _tail_kernel,
        out_shape=jax.ShapeDtypeStruct((n, D_MODEL), F32),
        grid=(n // tm,),
        in_specs=[pl.BlockSpec((tm, SG_WIDTH), lambda i: (i, 0)),
                  pl.BlockSpec((None, NSA_WIDTH, tm), lambda i: (i // per_b, 0, i % per_b)),
                  pl.BlockSpec((tm, D_MODEL), lambda i: (i, 0)),
                  _const_spec((NSA_WIDTH, 1)), weight((D_MODEL, D_MODEL)),
                  _const_spec((1, D_MODEL)), weight((D_MODEL, width)), _const_spec((1, MEM_HEAD_DIM)),
                  kv_spec, kv_spec, weight((width, D_MODEL)),
                  _const_spec((1, D_MODEL)), weight((D_MODEL, D_FF)), weight((D_FF, D_MODEL))],
        out_specs=pl.BlockSpec((tm, D_MODEL), lambda i: (i, 0)),
        compiler_params=pltpu.CompilerParams(dimension_semantics=("parallel",), vmem_limit_bytes=TAIL_VMEM_LIMIT),
        name="tail",
    )(a_n, bT, x2, g_b, w_out, g_mem, w_mq, q_g, mk, mv, w_mo, g_ffn, w1, w2)


def _overlap_T(n_cmp_pad, n_cmp, n_slc):
    cs = jnp.arange(n_cmp_pad) * CMP_STRIDE
    ss = jnp.arange(n_slc) * SLC_BLOCK
    ov = jnp.clip(jnp.minimum(cs[None, :] + CMP_BLOCK, ss[:, None] + SLC_BLOCK)
                  - jnp.maximum(cs[None, :], ss[:, None]), 0, None).astype(F32) / CMP_BLOCK
    ov = jnp.where(jnp.arange(n_cmp_pad)[None, :] < n_cmp, ov, 0.0)
    return ov.astype(BF16)


def kernel(x, mem, norm_mix_g, w_in, sg_ln_g, sg_ln_b, sg_w, sg_b, q_norm_g, k_norm_g, cmp_pos, cmp_w1, cmp_b1, cmp_w2, cmp_b2, mix_out_g, w_out, norm_mem_g, mem_kv_norm_g, w_mq, w_mkv, mem_q_norm_g, mem_k_norm_g, w_mo, norm_ffn_g, w_ff1, w_ff2):
    batch, seq, _ = x.shape
    depth = w_in.shape[0]
    n = batch * seq
    tq = ATTN_TILE
    assert WINDOW == 2 * tq and seq % 512 == 0
    n_chunk = seq // CMP_STRIDE
    n_cmp = n_chunk - 1
    n_slc = seq // SLC_BLOCK
    assert n_slc <= HEAD_DIM
    ovT = _overlap_T(n_chunk, n_cmp, n_slc)
    row = lambda a: a.reshape(1, -1)
    r_idx = lax.broadcasted_iota(jnp.int32, (tq, tq), 0)
    c_idx = lax.broadcasted_iota(jnp.int32, (tq, tq), 1)
    masks = jnp.stack([jnp.zeros((tq, tq), F32), jnp.where(r_idx <= c_idx, 0.0, MASKED),
                       jnp.where(r_idx > c_idx, 0.0, MASKED), jnp.full((tq, tq), MASKED, F32)])

    x2 = x.reshape(n, D_MODEL)
    for l in range(depth):
        w = w_in[l].astype(BF16)
        w_gT = jnp.pad(w[:, COL_GATE:].T.reshape(NSA_KV_HEADS, GATES_PER_HEAD, D_MODEL),
                       ((0, 0), (0, GATE_ROWS - GATES_PER_HEAD), (0, 0))).reshape(NSA_KV_HEADS * GATE_ROWS, D_MODEL)
        stream = lambda s: w[:, COL_KV + s * KV_WIDTH:COL_KV + (s + 1) * KV_WIDTH]
        w_k = jnp.concatenate([stream(0), stream(1), stream(2), stream(4)], axis=1)
        w_vT = jnp.concatenate([stream(3), stream(5)], axis=1).T
        bs_full = jnp.repeat(sg_b[l].T, HEAD_DIM, axis=1)
        a_n, kx, kw, vT, k_cmp, v_cmp, qT, glT = _front(
            x2, row(norm_mix_g[l]), w[:, :COL_Q], w_k, w_vT, w[:, COL_Q:COL_KV].T, w_gT,
            row(sg_ln_g[l]), row(sg_ln_b[l]), sg_w[l], bs_full, row(mix_out_g[l, 0]),
            row(k_norm_g[l, 1]), row(k_norm_g[l, 2]), batch, seq)
        tiles = lambda a: a.reshape(batch, NSA_KV_HEADS, seq // tq, tq, a.shape[-1])
        chunks = lambda a: a.reshape(batch * NSA_KV_HEADS, n_chunk, CMP_STRIDE * HEAD_DIM)
        pos8 = lambda p: jnp.broadcast_to(p.reshape(1, -1), (8, CMP_BLOCK * HEAD_DIM))
        kc = _compress(chunks(k_cmp), pos8(cmp_pos[l, 0]), cmp_w1[l, 0], row(cmp_b1[l, 0]), cmp_w2[l, 0],
                       row(cmp_b2[l, 0]), row(k_norm_g[l, 0]), True)
        vc = _compress(chunks(v_cmp), pos8(cmp_pos[l, 1]), cmp_w1[l, 1], row(cmp_b1[l, 1]), cmp_w2[l, 1],
                       row(cmp_b2[l, 1]), row(k_norm_g[l, 0]), False)
        vcT = vc.transpose(0, 2, 1)

        gq = (q_norm_g[l] * (HEAD_DIM ** -0.5 * LOG2E)).reshape(HEAD_DIM, 1)
        kx, kw = tiles(kx), tiles(kw)
        peak = lambda g: jnp.max(jnp.abs(g))
        bound = HEAD_DIM ** 0.5 * LOG2E * peak(q_norm_g[l]) * peak(k_norm_g[l])

        kc_x = jnp.concatenate([kc, jnp.ones_like(kc[..., :1]), jnp.zeros_like(kc[..., 1:])], axis=-1).astype(BF16)
        vcT_x = jnp.concatenate([vcT, jnp.ones_like(vcT[:, :1]),
                                 jnp.zeros_like(vcT[:, :BF16_SUBLANES - 1])], axis=1).astype(BF16)
        ops = (qT, glT, kc_x, vcT_x, ovT, kx, kw, vT, masks, gq)
        oT = lax.cond(2.0 * bound <= MAX_SCORE_SPAN,
                      lambda ops, bound: _attention(*ops, (-bound).reshape(1), n_cmp, True),
                      lambda ops, bound: _attention(*ops, jnp.zeros((1,), F32), n_cmp, False),
                      ops, bound)

        mk, mv = _mem_kv(mem, row(mem_kv_norm_g[l]), w_mkv[l].astype(BF16), row(mem_k_norm_g[l]))
        x2 = _tail(a_n, oT, x2, mix_out_g[l, 1].reshape(NSA_WIDTH, 1), w_out[l].astype(BF16),
                   row(norm_mem_g[l]), w_mq[l].astype(BF16), row(mem_q_norm_g[l]), mk, mv, w_mo[l].astype(BF16),
                   row(norm_ffn_g[l]), w_ff1[l].astype(BF16), w_ff2[l].astype(BF16), seq)
    return x2.reshape(batch, seq, D_MODEL)
```

```python
import functools

import jax
import jax.numpy as jnp
from jax import lax
from jax.experimental import pallas as pl
from jax.experimental.pallas import tpu as pltpu

F32 = jnp.float32
BF16 = jnp.bfloat16

D_MODEL = 1024
HEAD_DIM = 64
SG_WIDTH = 512
SG_GROUPS = 8
SG_CHUNK = 128
NSA_WIDTH = 512
NSA_KV_HEADS = 2
NSA_GROUP = 4
KV_WIDTH = NSA_KV_HEADS * HEAD_DIM
N_BRANCH = 3
CMP_BLOCK = 32
CMP_STRIDE = 16
CMP_HIDDEN = 256
SLC_BLOCK = 64
SLC_TOPK = 16
WINDOW = 512
FORCE_SCORE = 1e4
MEM_HEADS = 4
MEM_HEAD_DIM = 128
D_FF = 4 * D_MODEL
EPS = 1e-6

COL_Q = 2 * SG_WIDTH
COL_KV = COL_Q + NSA_WIDTH
COL_GATE = COL_KV + 6 * KV_WIDTH
GATES_PER_HEAD = NSA_GROUP * N_BRANCH
GATE_ROWS = 16

MASKED = -1e30
LOG2E = 1.4426950408889634
MAX_SCORE_SPAN = 100.0
BF16_SUBLANES = 16
V_ROWS = HEAD_DIM + BF16_SUBLANES
ATTN_TILE = 256
VMEM_LIMIT = 48 * 1024 * 1024
TAIL_VMEM_LIMIT = 56 * 1024 * 1024

NT_DIMS = (((1,), (1,)), ((), ()))
TN_DIMS = (((0,), (0,)), ((), ()))


def _cparams(*sem):
    return pltpu.CompilerParams(dimension_semantics=sem, vmem_limit_bytes=VMEM_LIMIT)


def _rms(x, g):
    return x * lax.rsqrt(jnp.mean(x * x, axis=-1, keepdims=True) + EPS) * g


def _const_spec(shape):
    return pl.BlockSpec(shape, lambda *_: (0,) * len(shape))


def _dot_3pass(a, b):
    a_hi = a.astype(BF16)
    b_hi = b.astype(BF16)
    a_lo = (a - a_hi.astype(F32)).astype(BF16)
    b_lo = (b - b_hi.astype(F32)).astype(BF16)
    dot = functools.partial(jnp.dot, preferred_element_type=F32)
    return dot(a_hi, b_hi) + (dot(a_hi, b_lo) + dot(a_lo, b_hi))


def _dot_1pass(a, b):
    return jnp.dot(a.astype(BF16), b.astype(BF16), preferred_element_type=F32)


def _front_kernel(x_ref, g_ref, wuv_ref, wk_ref, wvT_ref, wqT_ref, wgT_ref,
                  lng_ref, lnb_ref, ws_ref, bs_ref, og_ref, gs_ref, gw_ref,
                  a_ref, kx_ref, kw_ref, vT_ref, kc_ref, vc_ref, qT_ref, gT_ref, h_sc, *, tm, seq):
    h_sc[...] = _rms(x_ref[...], g_ref[...]).astype(BF16)
    tq = vT_ref.shape[-1]

    def queries():
        qT_ref[...] = lax.dot_general(wqT_ref[...], h_sc[...], NT_DIMS, preferred_element_type=F32)
        gT_ref[...] = lax.dot_general(wgT_ref[...], h_sc[...], NT_DIMS, preferred_element_type=F32)

    def values():
        vT = lax.dot_general(wvT_ref[...], h_sc[...], NT_DIMS, preferred_element_type=F32).astype(BF16)
        tail = jnp.where(lax.broadcasted_iota(jnp.int32, (BF16_SUBLANES, tq), 0) == 0, 1.0, 0.0).astype(BF16)
        for branch in range(2):
            for hd in range(NSA_KV_HEADS):
                r0 = (branch * NSA_KV_HEADS + hd) * HEAD_DIM
                for t in range(tm // tq):
                    vT_ref[hd, branch, t, :HEAD_DIM, :] = vT[r0:r0 + HEAD_DIM, t * tq:(t + 1) * tq]
                    vT_ref[hd, branch, t, HEAD_DIM:, :] = tail

    def keys(half):
        rows = slice(half * (tm // 2), (half + 1) * (tm // 2))
        kk = jnp.dot(h_sc[rows, :], wk_ref[...], preferred_element_type=F32)
        t0 = (pl.program_id(0) * tm) % seq + half * (tm // 2)
        blk = (t0 + lax.broadcasted_iota(jnp.int32, (tm // 2, SLC_BLOCK), 0)) // SLC_BLOCK
        onehot = jnp.where(blk == lax.broadcasted_iota(jnp.int32, (tm // 2, SLC_BLOCK), 1), 1.0, 0.0).astype(BF16)
        for hd in range(NSA_KV_HEADS):
            def part(idx):
                c0 = idx * KV_WIDTH + hd * HEAD_DIM
                return kk[:, c0:c0 + HEAD_DIM]
            kc_ref[hd, rows, :] = part(0)
            vc_ref[hd, rows, :] = part(1)
            kx_ref[hd, rows, :HEAD_DIM] = _rms(part(2), gs_ref[...]).astype(BF16)
            kx_ref[hd, rows, HEAD_DIM:] = onehot
            kw_ref[hd, rows, :HEAD_DIM] = _rms(part(3), gw_ref[...]).astype(BF16)
            kw_ref[hd, rows, HEAD_DIM:] = onehot

    row = lax.broadcasted_iota(jnp.int32, (SG_CHUNK, SG_CHUNK), 0)
    col = lax.broadcasted_iota(jnp.int32, (SG_CHUNK, SG_CHUNK), 1)
    causal = col <= row
    w = [jnp.where(causal, ws_ref[g], 0.0).astype(BF16) for g in range(SG_GROUPS)]
    between = [lambda: keys(0), lambda: keys(1), values, queries]
    for c in range(tm // SG_CHUNK):
        rows = slice(c * SG_CHUNK, (c + 1) * SG_CHUNK)
        uv = jnp.dot(h_sc[rows, :], wuv_ref[...], preferred_element_type=F32)
        between[c]()
        u = jax.nn.gelu(uv[:, :SG_WIDTH])
        v = jax.nn.gelu(uv[:, SG_WIDTH:])
        mu = jnp.mean(v, axis=-1, keepdims=True)
        vc = v - mu
        vn = vc * lax.rsqrt(jnp.mean(vc * vc, axis=-1, keepdims=True) + EPS) * lng_ref[...] + lnb_ref[...]
        vb = vn.astype(BF16)
        s = jnp.concatenate(
            [jnp.dot(w[g], vb[:, g * HEAD_DIM:(g + 1) * HEAD_DIM], preferred_element_type=F32)
             for g in range(SG_GROUPS)], axis=-1) + bs_ref[...]
        a_ref[rows, :] = _rms(u * s, og_ref[...]).astype(BF16)


def _front(x2, g, w_uv, w_k, w_vT, w_qT, w_gT, ln_g, ln_b, w_s, bs_full, out_g, g_slc, g_win, batch, seq):
    n = x2.shape[0]
    tm = 512
    tq = ATTN_TILE
    per_b = seq // tm
    hd = NSA_KV_HEADS
    g_rows = w_gT.shape[0]

    def per_head(width):
        return pl.BlockSpec((None, hd, tm, width), lambda i: (i // per_b, 0, i % per_b, 0))

    def per_head_shape(width, dtype):
        return jax.ShapeDtypeStruct((batch, hd, seq, width), dtype)

    def cols(height):
        return pl.BlockSpec((None, height, tm), lambda i: (i // per_b, 0, i % per_b))

    return pl.pallas_call(
        functools.partial(_front_kernel, tm=tm, seq=seq),
        out_shape=(jax.ShapeDtypeStruct((n, SG_WIDTH), BF16),
                   per_head_shape(2 * HEAD_DIM, BF16), per_head_shape(2 * HEAD_DIM, BF16),
                   jax.ShapeDtypeStruct((batch, hd, 2, seq // tq, V_ROWS, tq), BF16),
                   per_head_shape(HEAD_DIM, F32), per_head_shape(HEAD_DIM, F32),
                   jax.ShapeDtypeStruct((batch, NSA_WIDTH, seq), F32),
                   jax.ShapeDtypeStruct((batch, g_rows, seq), F32)),
        grid=(n // tm,),
        in_specs=[pl.BlockSpec((tm, D_MODEL), lambda i: (i, 0)), _const_spec((1, D_MODEL)),
                  _const_spec((D_MODEL, 2 * SG_WIDTH)), _const_spec((D_MODEL, 4 * KV_WIDTH)),
                  _const_spec((2 * KV_WIDTH, D_MODEL)), _const_spec((NSA_WIDTH, D_MODEL)),
                  _const_spec((g_rows, D_MODEL)),
                  _const_spec((1, SG_WIDTH)), _const_spec((1, SG_WIDTH)),
                  _const_spec((SG_GROUPS, SG_CHUNK, SG_CHUNK)),
                  _const_spec((SG_CHUNK, SG_WIDTH)), _const_spec((1, SG_WIDTH)),
                  _const_spec((1, HEAD_DIM)), _const_spec((1, HEAD_DIM))],
        out_specs=(pl.BlockSpec((tm, SG_WIDTH), lambda i: (i, 0)),
                   per_head(2 * HEAD_DIM), per_head(2 * HEAD_DIM),
                   pl.BlockSpec((None, hd, 2, tm // tq, V_ROWS, tq), lambda i: (i // per_b, 0, 0, i % per_b, 0, 0)),
                   per_head(HEAD_DIM), per_head(HEAD_DIM), cols(NSA_WIDTH), cols(g_rows)),
        scratch_shapes=[pltpu.VMEM((tm, D_MODEL), BF16)],
        compiler_params=_cparams("parallel"),
        name="front",
    )(x2, g, w_uv, w_k, w_vT, w_qT, w_gT, ln_g, ln_b, w_s, bs_full, out_g, g_slc, g_win)


def _cmp_kernel(c_ref, pos_ref, w1_ref, b1_ref, w2_ref, b2_ref, g_ref, o_ref, *, norm):
    dot = _dot_3pass if norm else _dot_1pass
    c = c_ref[...]
    nb, half = c.shape
    first = dot(c, w1_ref[:half, :])
    second = dot(c, w1_ref[half:, :])
    pos = dot(pos_ref[...], w1_ref[...])[0:1, :]
    h1 = jax.nn.gelu(first + pltpu.roll(second, nb - 1, 0) + pos + b1_ref[...])
    y = dot(h1, w2_ref[...]) + b2_ref[...]
    if norm:
        y = _rms(y, g_ref[...])
    o_ref[...] = y


def _compress(chunks, pos8, w1, b1, w2, b2, g, norm):
    bh, nb, half = chunks.shape
    return pl.pallas_call(
        functools.partial(_cmp_kernel, norm=norm),
        out_shape=jax.ShapeDtypeStruct((bh, nb, HEAD_DIM), F32),
        grid=(bh,),
        in_specs=[pl.BlockSpec((None, nb, half), lambda i: (i, 0, 0)),
                  _const_spec((8, 2 * half)), _const_spec((2 * half, CMP_HIDDEN)), _const_spec((1, CMP_HIDDEN)),
                  _const_spec((CMP_HIDDEN, HEAD_DIM)), _const_spec((1, HEAD_DIM)), _const_spec((1, HEAD_DIM))],
        out_specs=pl.BlockSpec((None, nb, HEAD_DIM), lambda i: (i, 0, 0)),
        compiler_params=_cparams("parallel"),
        name="compress_k" if norm else "compress_v",
    )(chunks, pos8, w1, b1, w2, b2, g)


def _stable_rank(score, score_sc, rank_sc, rows_sc, tq, interleave):
    interleave = list(interleave)
    n_slc = score.shape[0]
    for k in range(n_slc):
        rows_sc[k] = score[k:k + 1, :]
    score_sc[...] = score
    sub, k_chunk = 8, 16
    row_in_group = lax.broadcasted_iota(jnp.int32, (sub, tq), 0)
    for k0 in range(0, n_slc, k_chunk):
        if interleave:
            interleave.pop(0)()
        rks = [jnp.broadcast_to(rows_sc[k], (sub, tq)) for k in range(k0, k0 + k_chunk)]
        for r0 in range(0, n_slc, sub):
            sg = score_sc[r0:r0 + sub, :]
            acc = rank_sc[r0:r0 + sub, :] if k0 else jnp.zeros((sub, tq), F32)
            for k in range(k0, k0 + k_chunk):
                rk = rks[k - k0]
                ge = jnp.where(rk >= sg, 1.0, 0.0)
                gt = jnp.where(rk > sg, 1.0, 0.0)
                if r0 > k:
                    before = ge
                elif r0 + sub - 1 < k:
                    before = gt
                else:
                    before = jnp.where(row_in_group > k - r0, ge, gt)
                acc = acc + before
            rank_sc[r0:r0 + sub, :] = acc
    for rest in interleave:
        rest()


def _attn_kernel(qT_ref, glT_ref, kc_ref, vcT_ref, ovT_ref, kx_ref, kw_ref, vT_ref, mask_ref, cmask_ref, gq_ref,
                 shift_ref, o_ref, qw_sc, qx_sc, s_a, mo_a, mn_a, s_b, mo_b, mn_b, s_c, score_sc, rank_sc, rows_sc,
                 mrun_sc, acc_sc, *, tq, bounded):
    i = pl.program_id(2)
    d = HEAD_DIM
    G = NSA_GROUP
    shift = shift_ref[0]

    def lanes(g):
        return slice(g * tq, (g + 1) * tq)

    for g in range(G):
        qg = qT_ref[g * d:(g + 1) * d, :]
        qg = qg * lax.rsqrt(jnp.mean(qg * qg, axis=0, keepdims=True) + EPS) * gq_ref[...]
        qw_sc[0:d, lanes(g)] = qg.astype(BF16)
    qw_sc[d:2 * d, :] = jnp.full((d, G * tq), shift, F32).astype(BF16)

    gates = jax.nn.sigmoid(glT_ref[...])

    if not bounded:
        mrun_sc[...] = jnp.full(mrun_sc.shape, -jnp.inf, F32)
    acc_sc[...] = jnp.zeros(acc_sc.shape, F32)

    def scores(branch, k_ref, k_idx, q_sc, mask, s_buf, mo_buf, mn_buf):
        for g in range(G):
            s = jnp.dot(k_ref[k_idx], q_sc[:, lanes(g)], preferred_element_type=F32)
            if mask is not None:
                s = s + mask_ref[mask]
            s_buf[g] = s
            if not bounded:
                m_old = mrun_sc[branch, g]
                m_new = jnp.maximum(m_old, jnp.max(s, axis=0, keepdims=True))
                mo_buf[g] = m_old
                mn_buf[g] = m_new
                mrun_sc[branch, g] = m_new

    def accumulate(branch, v_idx, s_buf, mo_buf, mn_buf):
        for g in range(G):
            if bounded:
                p = jnp.exp2(s_buf[g]).astype(BF16)
                acc_sc[branch, g] += jnp.dot(vT_ref[branch, v_idx], p, preferred_element_type=F32)
            else:
                m_new = mn_buf[g]
                alpha = jnp.exp2(mo_buf[g] - m_new)
                p = jnp.exp2(s_buf[g] - m_new).astype(BF16)
                acc_sc[branch, g] = alpha * acc_sc[branch, g] + jnp.dot(vT_ref[branch, v_idx], p,
                                                                        preferred_element_type=F32)

    buf_a = (s_a, mo_a, mn_a)
    buf_b = (s_b, mo_b, mn_b)

    w_tiles = [jnp.maximum(i - 2, 0), jnp.maximum(i - 1, 0), i]
    w_masks = [jnp.where(i >= 2, 2, 3), jnp.where(i >= 1, 0, 3), 1]

    def window_head():
        scores(1, kw_ref, w_tiles[0], qw_sc, w_masks[0], *buf_a)
        scores(1, kw_ref, w_tiles[1], qw_sc, w_masks[1], *buf_b)

    def window_mid():
        accumulate(1, w_tiles[0], *buf_a)
        scores(1, kw_ref, w_tiles[2], qw_sc, w_masks[2], *buf_a)

    window_stages = [window_head, window_mid,
                     lambda: accumulate(1, w_tiles[1], *buf_b),
                     lambda: accumulate(1, w_tiles[2], *buf_a)]

    ncp = kc_ref.shape[0]
    t_row = i * tq + lax.broadcasted_iota(jnp.int32, (1, tq), 1)
    sees_any = jnp.where(t_row >= CMP_BLOCK - 1, 1.0, 0.0)
    c_max = []
    for g in range(G):
        s = jnp.dot(kc_ref[...], qw_sc[:, lanes(g)], preferred_element_type=F32) + cmask_ref[...]
        s_c[g] = s
        if not bounded:
            c_max.append(jnp.max(s, axis=0, keepdims=True))
    oc = []
    for g in range(G):
        e = jnp.exp2(s_c[g]) if bounded else jnp.exp2(s_c[g] - c_max[g])
        s_c[g] = e
        oc.append(jnp.dot(vcT_ref[...], e.astype(BF16), preferred_element_type=F32))
    psum = jnp.zeros((ncp, tq), F32)
    for g in range(G):
        den = oc[g][d:d + 1, :]
        r = sees_any / jnp.where(den > 0, den, 1.0)
        psum = psum + s_c[g] * r
        o_ref[g * d:(g + 1) * d, :] = (gates[g * 3:g * 3 + 1, :] * r) * oc[g][0:d, :]

    p_hi = psum.astype(BF16)
    p_lo = (psum - p_hi.astype(F32)).astype(BF16)
    imp = (jnp.dot(ovT_ref[...], p_hi, preferred_element_type=F32)
           + jnp.dot(ovT_ref[...], p_lo, preferred_element_type=F32))
    n_slc = imp.shape[0]
    j_idx = lax.broadcasted_iota(jnp.int32, (n_slc, tq), 0)
    t_blk = (i * tq + lax.broadcasted_iota(jnp.int32, (n_slc, tq), 1)) // SLC_BLOCK
    forced = (j_idx == 0) | (j_idx == t_blk) | (j_idx == t_blk - 1)
    score = jnp.where(forced, FORCE_SCORE, jnp.where(j_idx <= t_blk, imp, -FORCE_SCORE))
    _stable_rank(score, score_sc, rank_sc, rows_sc, tq, window_stages)
    n_sel = min(SLC_TOPK, n_slc)
    bias = jnp.where(rank_sc[...] < n_sel, shift, MASKED).astype(BF16)
    if n_slc < d:
        bias = jnp.concatenate([bias, jnp.zeros((d - n_slc, tq), BF16)], axis=0)
    qx_sc[0:d, :] = qw_sc[0:d, :]
    for g in range(G):
        qx_sc[d:2 * d, lanes(g)] = bias

    def slc_mask(j):
        return jnp.where(j == i, 1, jnp.where(j > i, 3, 0))

    def slc_tile(j):
        return jnp.minimum(j, i)

    scores(0, kx_ref, 0, qx_sc, slc_mask(0), *buf_a)
    n_pairs = jnp.maximum(i - 1, 0) // 2

    def pair(j0):
        scores(0, kx_ref, j0 + 1, qx_sc, None, *buf_b)
        accumulate(0, j0, *buf_a)
        scores(0, kx_ref, j0 + 2, qx_sc, None, *buf_a)
        accumulate(0, j0 + 1, *buf_b)

    def two_pairs(kk, carry):
        pair(4 * kk)
        pair(4 * kk + 2)
        return carry

    def one_pair(kk, carry):
        pair(4 * (n_pairs // 2))
        return carry

    lax.fori_loop(0, n_pairs // 2, two_pairs, 0)
    lax.fori_loop(0, n_pairs % 2, one_pair, 0)
    j0 = 2 * n_pairs
    scores(0, kx_ref, slc_tile(j0 + 1), qx_sc, slc_mask(j0 + 1), *buf_b)
    accumulate(0, j0, *buf_a)
    scores(0, kx_ref, slc_tile(j0 + 2), qx_sc, slc_mask(j0 + 2), *buf_a)
    accumulate(0, slc_tile(j0 + 1), *buf_b)
    accumulate(0, slc_tile(j0 + 2), *buf_a)

    for branch in range(2):
        for g in range(G):
            acc = acc_sc[branch, g]
            o = acc[0:d, :] * (1.0 / acc[d:d + 1, :])
            gate = gates[g * 3 + branch + 1:g * 3 + branch + 2, :]
            o_ref[g * d:(g + 1) * d, :] += gate * o


def _attention(qT, glT, kc_x, vcT_x, ovT, kx, kw, vT, masks, cmask, gq, shift, bounded):
    batch, _, seq = qT.shape
    tq = ATTN_TILE
    nq = seq // tq
    hd = NSA_KV_HEADS
    gw = NSA_GROUP * HEAD_DIM
    ncp = kc_x.shape[1]
    n_slc = ovT.shape[0]

    def per_head(shape):
        return pl.BlockSpec((None, None) + shape, lambda b, h, i: (b, h) + (0,) * len(shape))

    stage = [pltpu.VMEM((NSA_GROUP, tq, tq), F32),
             pltpu.VMEM((NSA_GROUP, 1, tq), F32),
             pltpu.VMEM((NSA_GROUP, 1, tq), F32)]
    return pl.pallas_call(
        functools.partial(_attn_kernel, tq=tq, bounded=bounded),
        out_shape=jax.ShapeDtypeStruct((batch, NSA_WIDTH, seq), F32),
        grid=(batch, hd, nq),
        in_specs=[pl.BlockSpec((None, gw, tq), lambda b, h, i: (b, h, i)),
                  pl.BlockSpec((None, GATE_ROWS, tq), lambda b, h, i: (b, h, i)),
                  pl.BlockSpec((None, ncp, 2 * HEAD_DIM), lambda b, h, i: (b * hd + h, 0, 0)),
                  pl.BlockSpec((None, V_ROWS, ncp), lambda b, h, i: (b * hd + h, 0, 0)),
                  _const_spec((n_slc, ncp)),
                  per_head((nq, tq, 2 * HEAD_DIM)), per_head((nq, tq, 2 * HEAD_DIM)),
                  per_head((2, nq, V_ROWS, tq)),
                  _const_spec((4, tq, tq)),
                  pl.BlockSpec((None, ncp, tq), lambda b, h, i: (i, 0, 0)),
                  _const_spec((HEAD_DIM, 1)),
                  pl.BlockSpec(memory_space=pltpu.SMEM)],
        out_specs=pl.BlockSpec((None, gw, tq), lambda b, h, i: (b, h, i)),
        scratch_shapes=[pltpu.VMEM((2 * HEAD_DIM, NSA_GROUP * tq), BF16),
                        pltpu.VMEM((2 * HEAD_DIM, NSA_GROUP * tq), BF16)] + stage + stage + [
            pltpu.VMEM((NSA_GROUP, ncp, tq), F32),
            pltpu.VMEM((n_slc, tq), F32), pltpu.VMEM((n_slc, tq), F32), pltpu.VMEM((n_slc, 1, tq), F32),
            pltpu.VMEM((2, NSA_GROUP, 1, tq), F32),
            pltpu.VMEM((2, NSA_GROUP, V_ROWS, tq), F32)],
        compiler_params=_cparams("parallel", "parallel", "arbitrary"),
        name="nsa_attention_bounded" if bounded else "nsa_attention",
    )(qT, glT, kc_x, vcT_x, ovT, kx, kw, vT, masks, cmask, gq, shift)


def _mem_kv_kernel(mem_ref, g_ref, w_ref, kg_ref, k_ref, v_ref):
    mn = _rms(mem_ref[...], g_ref[...]).astype(BF16)
    kv = jnp.dot(mn, w_ref[...], preferred_element_type=F32)
    width = MEM_HEADS * MEM_HEAD_DIM
    for hh in range(MEM_HEADS):
        cols = slice(hh * MEM_HEAD_DIM, (hh + 1) * MEM_HEAD_DIM)
        k_ref[:, cols] = _rms(kv[:, cols], kg_ref[...]).astype(BF16)
    v_ref[...] = kv[:, width:].astype(BF16)


def _mem_kv(mem, g, w_mkv, k_g):
    batch, m_len, _ = mem.shape
    width = MEM_HEADS * MEM_HEAD_DIM
    spec = pl.BlockSpec((None, m_len, width), lambda b: (b, 0, 0))
    return pl.pallas_call(
        _mem_kv_kernel,
        out_shape=(jax.ShapeDtypeStruct((batch, m_len, width), BF16),) * 2,
        grid=(batch,),
        in_specs=[pl.BlockSpec((None, m_len, D_MODEL), lambda b: (b, 0, 0)),
                  _const_spec((1, D_MODEL)), _const_spec((D_MODEL, 2 * width)),
                  _const_spec((1, MEM_HEAD_DIM))],
        out_specs=(spec, spec),
        compiler_params=_cparams("parallel"),
        name="mem_kv",
    )(mem, g, w_mkv, k_g)


def _tail_kernel(a_ref, bT_ref, x_ref, gb_ref, wout_ref, gm_ref, wq_ref, qg_ref, k_ref, v_ref, wo_ref,
                 gf_ref, w1_ref, w2_ref, o_ref):
    bT = bT_ref[...]
    bn = (bT * lax.rsqrt(jnp.mean(bT * bT, axis=0, keepdims=True) + EPS) * gb_ref[...]).astype(BF16)
    x = x_ref[...] + jnp.dot(a_ref[...], wout_ref[:SG_WIDTH, :], preferred_element_type=F32)
    x = x + lax.dot_general(bn, wout_ref[SG_WIDTH:, :], TN_DIMS, preferred_element_type=F32)

    h = _rms(x, gm_ref[...]).astype(BF16)
    q = jnp.dot(h, wq_ref[...], preferred_element_type=F32)
    outs = []
    for hh in range(MEM_HEADS):
        cols = slice(hh * MEM_HEAD_DIM, (hh + 1) * MEM_HEAD_DIM)
        qh = (_rms(q[:, cols], qg_ref[...]) * MEM_HEAD_DIM ** -0.5).astype(BF16)
        s = lax.dot_general(qh, k_ref[:, cols], NT_DIMS, preferred_element_type=F32)
        e = jnp.exp(s - jnp.max(s, axis=-1, keepdims=True))
        p = e * (1.0 / jnp.sum(e, axis=-1, keepdims=True))
        outs.append(jnp.dot(p.astype(BF16), v_ref[:, cols], preferred_element_type=F32))
    o = jnp.concatenate(outs, axis=-1).astype(BF16)
    x = x + jnp.dot(o, wo_ref[...], preferred_element_type=F32)

    h = _rms(x, gf_ref[...]).astype(BF16)
    step = 1024
    for c0 in range(0, D_FF, step):
        a = jnp.maximum(jnp.dot(h, w1_ref[:, c0:c0 + step], preferred_element_type=F32), 0.0)
        x = x + jnp.dot((a * a).astype(BF16), w2_ref[c0:c0 + step, :], preferred_element_type=F32)
    o_ref[...] = x


def _tail(a_n, bT, x2, g_b, w_out, g_mem, w_mq, q_g, mk, mv, w_mo, g_ffn, w1, w2, seq):
    n = x2.shape[0]
    tm = 512
    per_b = seq // tm
    m_len = mk.shape[1]
    width = MEM_HEADS * MEM_HEAD_DIM

    def weight(shape):
        return pl.BlockSpec(shape, lambda i: (0,) * len(shape), pipeline_mode=pl.Buffered(1))

    kv_spec = pl.BlockSpec((None, m_len, width), lambda i: (i // per_b, 0, 0))
    return pl.pallas_call(
        _tail_kernel,
        out_shape=jax.ShapeDtypeStruct((n, D_MODEL), F32),
        grid=(n // tm,),
        in_specs=[pl.BlockSpec((tm, SG_WIDTH), lambda i: (i, 0)),
                  pl.BlockSpec((None, NSA_WIDTH, tm), lambda i: (i // per_b, 0, i % per_b)),
                  pl.BlockSpec((tm, D_MODEL), lambda i: (i, 0)),
                  _const_spec((NSA_WIDTH, 1)), weight((D_MODEL, D_MODEL)),
                  _const_spec((1, D_MODEL)), weight((D_MODEL, width)), _const_spec((1, MEM_HEAD_DIM)),
                  kv_spec, kv_spec, weight((width, D_MODEL)),
                  _const_spec((1, D_MODEL)), weight((D_MODEL, D_FF)), weight((D_FF, D_MODEL))],
        out_specs=pl.BlockSpec((tm, D_MODEL), lambda i: (i, 0)),
        compiler_params=pltpu.CompilerParams(dimension_semantics=("parallel",), vmem_limit_bytes=TAIL_VMEM_LIMIT),
        name="tail",
    )(a_n, bT, x2, g_b, w_out, g_mem, w_mq, q_g, mk, mv, w_mo, g_ffn, w1, w2)


def _overlap_T(n_cmp_pad, n_cmp, n_slc):
    cs = jnp.arange(n_cmp_pad) * CMP_STRIDE
    ss = jnp.arange(n_slc) * SLC_BLOCK
    ov = jnp.clip(jnp.minimum(cs[None, :] + CMP_BLOCK, ss[:, None] + SLC_BLOCK)
                  - jnp.maximum(cs[None, :], ss[:, None]), 0, None).astype(F32) / CMP_BLOCK
    ov = jnp.where(jnp.arange(n_cmp_pad)[None, :] < n_cmp, ov, 0.0)
    return ov.astype(BF16)


def kernel(x, mem, norm_mix_g, w_in, sg_ln_g, sg_ln_b, sg_w, sg_b, q_norm_g, k_norm_g, cmp_pos, cmp_w1, cmp_b1, cmp_w2, cmp_b2, mix_out_g, w_out, norm_mem_g, mem_kv_norm_g, w_mq, w_mkv, mem_q_norm_g, mem_k_norm_g, w_mo, norm_ffn_g, w_ff1, w_ff2):
    batch, seq, _ = x.shape
    depth = w_in.shape[0]
    n = batch * seq
    tq = ATTN_TILE
    assert WINDOW == 2 * tq and seq % 512 == 0
    n_chunk = seq // CMP_STRIDE
    n_cmp = n_chunk - 1
    n_slc = seq // SLC_BLOCK
    assert n_slc <= HEAD_DIM
    ovT = _overlap_T(n_chunk, n_cmp, n_slc)
    row = lambda a: a.reshape(1, -1)
    r_idx = lax.broadcasted_iota(jnp.int32, (tq, tq), 0)
    c_idx = lax.broadcasted_iota(jnp.int32, (tq, tq), 1)
    masks = jnp.stack([jnp.zeros((tq, tq), F32), jnp.where(r_idx <= c_idx, 0.0, MASKED),
                       jnp.where(r_idx > c_idx, 0.0, MASKED), jnp.full((tq, tq), MASKED, F32)])
    blk_end = jnp.arange(n_chunk) * CMP_STRIDE + (CMP_BLOCK - 1)
    cmask = jnp.where((blk_end[:, None] <= jnp.arange(seq)[None, :]) & (jnp.arange(n_chunk)[:, None] < n_cmp),
                      0.0, MASKED).astype(F32).reshape(n_chunk, seq // tq, tq).transpose(1, 0, 2)

    x2 = x.reshape(n, D_MODEL)
    for l in range(depth):
        w = w_in[l].astype(BF16)
        w_gT = jnp.pad(w[:, COL_GATE:].T.reshape(NSA_KV_HEADS, GATES_PER_HEAD, D_MODEL),
                       ((0, 0), (0, GATE_ROWS - GATES_PER_HEAD), (0, 0))).reshape(NSA_KV_HEADS * GATE_ROWS, D_MODEL)
        stream = lambda s: w[:, COL_KV + s * KV_WIDTH:COL_KV + (s + 1) * KV_WIDTH]
        w_k = jnp.concatenate([stream(0), stream(1), stream(2), stream(4)], axis=1)
        w_vT = jnp.concatenate([stream(3), stream(5)], axis=1).T
        bs_full = jnp.repeat(sg_b[l].T, HEAD_DIM, axis=1)
        a_n, kx, kw, vT, k_cmp, v_cmp, qT, glT = _front(
            x2, row(norm_mix_g[l]), w[:, :COL_Q], w_k, w_vT, w[:, COL_Q:COL_KV].T, w_gT,
            row(sg_ln_g[l]), row(sg_ln_b[l]), sg_w[l], bs_full, row(mix_out_g[l, 0]),
            row(k_norm_g[l, 1]), row(k_norm_g[l, 2]), batch, seq)
        tiles = lambda a: a.reshape(batch, NSA_KV_HEADS, seq // tq, tq, a.shape[-1])
        chunks = lambda a: a.reshape(batch * NSA_KV_HEADS, n_chunk, CMP_STRIDE * HEAD_DIM)
        pos8 = lambda p: jnp.broadcast_to(p.reshape(1, -1), (8, CMP_BLOCK * HEAD_DIM))
        kc = _compress(chunks(k_cmp), pos8(cmp_pos[l, 0]), cmp_w1[l, 0], row(cmp_b1[l, 0]), cmp_w2[l, 0],
                       row(cmp_b2[l, 0]), row(k_norm_g[l, 0]), True)
        vc = _compress(chunks(v_cmp), pos8(cmp_pos[l, 1]), cmp_w1[l, 1], row(cmp_b1[l, 1]), cmp_w2[l, 1],
                       row(cmp_b2[l, 1]), row(k_norm_g[l, 0]), False)
        vcT = vc.transpose(0, 2, 1)

        gq = (q_norm_g[l] * (HEAD_DIM ** -0.5 * LOG2E)).reshape(HEAD_DIM, 1)
        kx, kw = tiles(kx), tiles(kw)
        peak = lambda g: jnp.max(jnp.abs(g))
        bound = HEAD_DIM ** 0.5 * LOG2E * peak(q_norm_g[l]) * peak(k_norm_g[l])

        kc_x = jnp.concatenate([kc, jnp.ones_like(kc[..., :1]), jnp.zeros_like(kc[..., 1:])], axis=-1).astype(BF16)
        vcT_x = jnp.concatenate([vcT, jnp.ones_like(vcT[:, :1]),
                                 jnp.zeros_like(vcT[:, :BF16_SUBLANES - 1])], axis=1).astype(BF16)
        ops = (qT, glT, kc_x, vcT_x, ovT, kx, kw, vT, masks, cmask, gq)
        oT = lax.cond(2.0 * bound <= MAX_SCORE_SPAN,
                      lambda ops, bound: _attention(*ops, (-bound).reshape(1), True),
                      lambda ops, bound: _attention(*ops, jnp.zeros((1,), F32), False),
                      ops, bound)

        mk, mv = _mem_kv(mem, row(mem_kv_norm_g[l]), w_mkv[l].astype(BF16), row(mem_k_norm_g[l]))
        x2 = _tail(a_n, oT, x2, mix_out_g[l, 1].reshape(NSA_WIDTH, 1), w_out[l].astype(BF16),
                   row(norm_mem_g[l]), w_mq[l].astype(BF16), row(mem_q_norm_g[l]), mk, mv, w_mo[l].astype(BF16),
                   row(norm_ffn_g[l]), w_ff1[l].astype(BF16), w_ff2[l].astype(BF16), seq)
    return x2.reshape(batch, seq, D_MODEL)
```

```python
import functools

import jax
import jax.numpy as jnp
from jax import lax
from jax.experimental import pallas as pl
from jax.experimental.pallas import tpu as pltpu

F32 = jnp.float32
BF16 = jnp.bfloat16

D_MODEL = 1024
HEAD_DIM = 64
SG_WIDTH = 512
SG_GROUPS = 8
SG_CHUNK = 128
NSA_WIDTH = 512
NSA_KV_HEADS = 2
NSA_GROUP = 4
KV_WIDTH = NSA_KV_HEADS * HEAD_DIM
N_BRANCH = 3
CMP_BLOCK = 32
CMP_STRIDE = 16
CMP_HIDDEN = 256
SLC_BLOCK = 64
SLC_TOPK = 16
WINDOW = 512
FORCE_SCORE = 1e4
MEM_HEADS = 4
MEM_HEAD_DIM = 128
D_FF = 4 * D_MODEL
EPS = 1e-6

COL_Q = 2 * SG_WIDTH
COL_KV = COL_Q + NSA_WIDTH
COL_GATE = COL_KV + 6 * KV_WIDTH
GATES_PER_HEAD = NSA_GROUP * N_BRANCH
GATE_ROWS = 16

MASKED = -1e30
LOG2E = 1.4426950408889634
MAX_SCORE_SPAN = 100.0
BF16_SUBLANES = 16
V_ROWS = HEAD_DIM + BF16_SUBLANES
ATTN_TILE = 256
VMEM_LIMIT = 48 * 1024 * 1024
TAIL_VMEM_LIMIT = 56 * 1024 * 1024

NT_DIMS = (((1,), (1,)), ((), ()))
TN_DIMS = (((0,), (0,)), ((), ()))


def _cparams(*sem):
    return pltpu.CompilerParams(dimension_semantics=sem, vmem_limit_bytes=VMEM_LIMIT)


def _rms(x, g):
    return x * lax.rsqrt(jnp.mean(x * x, axis=-1, keepdims=True) + EPS) * g


def _const_spec(shape):
    return pl.BlockSpec(shape, lambda *_: (0,) * len(shape))


def _dot_3pass(a, b):
    a_hi = a.astype(BF16)
    b_hi = b.astype(BF16)
    a_lo = (a - a_hi.astype(F32)).astype(BF16)
    b_lo = (b - b_hi.astype(F32)).astype(BF16)
    dot = functools.partial(jnp.dot, preferred_element_type=F32)
    return dot(a_hi, b_hi) + (dot(a_hi, b_lo) + dot(a_lo, b_hi))


def _dot_1pass(a, b):
    return jnp.dot(a.astype(BF16), b.astype(BF16), preferred_element_type=F32)


def _front_kernel(x_ref, g_ref, wuv_ref, wk_ref, wvT_ref, wqT_ref, wgT_ref,
                  lng_ref, lnb_ref, ws_ref, bs_ref, og_ref, gs_ref, gw_ref,
                  a_ref, kx_ref, kw_ref, vT_ref, kc_ref, vc_ref, qT_ref, gT_ref, h_sc, cmp_sc, *, tm, seq):
    h_sc[...] = _rms(x_ref[...], g_ref[...]).astype(BF16)
    tq = vT_ref.shape[-1]

    def queries():
        qT_ref[...] = lax.dot_general(wqT_ref[...], h_sc[...], NT_DIMS, preferred_element_type=F32)
        gT_ref[...] = lax.dot_general(wgT_ref[...], h_sc[...], NT_DIMS, preferred_element_type=F32)

    def values():
        vT = lax.dot_general(wvT_ref[...], h_sc[...], NT_DIMS, preferred_element_type=F32).astype(BF16)
        tail = jnp.where(lax.broadcasted_iota(jnp.int32, (BF16_SUBLANES, tq), 0) == 0, 1.0, 0.0).astype(BF16)
        for branch in range(2):
            for hd in range(NSA_KV_HEADS):
                r0 = (branch * NSA_KV_HEADS + hd) * HEAD_DIM
                for t in range(tm // tq):
                    vT_ref[hd, branch, t, :HEAD_DIM, :] = vT[r0:r0 + HEAD_DIM, t * tq:(t + 1) * tq]
                    vT_ref[hd, branch, t, HEAD_DIM:, :] = tail

    def keys(half):
        rows = slice(half * (tm // 2), (half + 1) * (tm // 2))
        kk = jnp.dot(h_sc[rows, :], wk_ref[...], preferred_element_type=F32)
        t0 = (pl.program_id(0) * tm) % seq + half * (tm // 2)
        blk = (t0 + lax.broadcasted_iota(jnp.int32, (tm // 2, SLC_BLOCK), 0)) // SLC_BLOCK
        onehot = jnp.where(blk == lax.broadcasted_iota(jnp.int32, (tm // 2, SLC_BLOCK), 1), 1.0, 0.0).astype(BF16)
        n_half = (tm // 2) // CMP_STRIDE
        chunk_rows = slice(half * n_half, (half + 1) * n_half)
        for out_ref, idx in ((kc_ref, 0), (vc_ref, 1)):
            cmp_sc[idx, rows, :] = kk[:, idx * KV_WIDTH:(idx + 1) * KV_WIDTH]
            for p in range(CMP_STRIDE):
                at_p = cmp_sc[idx, pl.ds(half * (tm // 2) + p, n_half, stride=CMP_STRIDE), :]
                for hd in range(NSA_KV_HEADS):
                    out_ref[hd, chunk_rows, p * HEAD_DIM:(p + 1) * HEAD_DIM] = at_p[:, hd * HEAD_DIM:(hd + 1) * HEAD_DIM]
        for hd in range(NSA_KV_HEADS):
            def part(idx):
                c0 = idx * KV_WIDTH + hd * HEAD_DIM
                return kk[:, c0:c0 + HEAD_DIM]
            kx_ref[hd, rows, :HEAD_DIM] = _rms(part(2), gs_ref[...]).astype(BF16)
            kx_ref[hd, rows, HEAD_DIM:] = onehot
            kw_ref[hd, rows, :HEAD_DIM] = _rms(part(3), gw_ref[...]).astype(BF16)
            kw_ref[hd, rows, HEAD_DIM:] = onehot

    row = lax.broadcasted_iota(jnp.int32, (SG_CHUNK, SG_CHUNK), 0)
    col = lax.broadcasted_iota(jnp.int32, (SG_CHUNK, SG_CHUNK), 1)
    causal = col <= row
    w = [jnp.where(causal, ws_ref[g], 0.0).astype(BF16) for g in range(SG_GROUPS)]
    between = [lambda: keys(0), lambda: keys(1), values, queries]
    for c in range(tm // SG_CHUNK):
        rows = slice(c * SG_CHUNK, (c + 1) * SG_CHUNK)
        uv = jnp.dot(h_sc[rows, :], wuv_ref[...], preferred_element_type=F32)
        between[c]()
        u = jax.nn.gelu(uv[:, :SG_WIDTH])
        v = jax.nn.gelu(uv[:, SG_WIDTH:])
        mu = jnp.mean(v, axis=-1, keepdims=True)
        vc = v - mu
        vn = vc * lax.rsqrt(jnp.mean(vc * vc, axis=-1, keepdims=True) + EPS) * lng_ref[...] + lnb_ref[...]
        vb = vn.astype(BF16)
        s = jnp.concatenate(
            [jnp.dot(w[g], vb[:, g * HEAD_DIM:(g + 1) * HEAD_DIM], preferred_element_type=F32)
             for g in range(SG_GROUPS)], axis=-1) + bs_ref[...]
        a_ref[rows, :] = _rms(u * s, og_ref[...]).astype(BF16)


def _front(x2, g, w_uv, w_k, w_vT, w_qT, w_gT, ln_g, ln_b, w_s, bs_full, out_g, g_slc, g_win, batch, seq):
    n = x2.shape[0]
    tm = 512
    tq = ATTN_TILE
    per_b = seq // tm
    hd = NSA_KV_HEADS
    g_rows = w_gT.shape[0]

    def per_head(width):
        return pl.BlockSpec((None, hd, tm, width), lambda i: (i // per_b, 0, i % per_b, 0))

    def per_head_shape(width, dtype):
        return jax.ShapeDtypeStruct((batch, hd, seq, width), dtype)

    def cols(height):
        return pl.BlockSpec((None, height, tm), lambda i: (i // per_b, 0, i % per_b))

    chunked_shape = jax.ShapeDtypeStruct((batch, hd, seq // CMP_STRIDE, CMP_STRIDE * HEAD_DIM), F32)
    chunked = pl.BlockSpec((None, hd, tm // CMP_STRIDE, CMP_STRIDE * HEAD_DIM),
                           lambda i: (i // per_b, 0, i % per_b, 0))

    return pl.pallas_call(
        functools.partial(_front_kernel, tm=tm, seq=seq),
        out_shape=(jax.ShapeDtypeStruct((n, SG_WIDTH), BF16),
                   per_head_shape(2 * HEAD_DIM, BF16), per_head_shape(2 * HEAD_DIM, BF16),
                   jax.ShapeDtypeStruct((batch, hd, 2, seq // tq, V_ROWS, tq), BF16),
                   chunked_shape, chunked_shape,
                   jax.ShapeDtypeStruct((batch, NSA_WIDTH, seq), F32),
                   jax.ShapeDtypeStruct((batch, g_rows, seq), F32)),
        grid=(n // tm,),
        in_specs=[pl.BlockSpec((tm, D_MODEL), lambda i: (i, 0)), _const_spec((1, D_MODEL)),
                  _const_spec((D_MODEL, 2 * SG_WIDTH)), _const_spec((D_MODEL, 4 * KV_WIDTH)),
                  _const_spec((2 * KV_WIDTH, D_MODEL)), _const_spec((NSA_WIDTH, D_MODEL)),
                  _const_spec((g_rows, D_MODEL)),
                  _const_spec((1, SG_WIDTH)), _const_spec((1, SG_WIDTH)),
                  _const_spec((SG_GROUPS, SG_CHUNK, SG_CHUNK)),
                  _const_spec((SG_CHUNK, SG_WIDTH)), _const_spec((1, SG_WIDTH)),
                  _const_spec((1, HEAD_DIM)), _const_spec((1, HEAD_DIM))],
        out_specs=(pl.BlockSpec((tm, SG_WIDTH), lambda i: (i, 0)),
                   per_head(2 * HEAD_DIM), per_head(2 * HEAD_DIM),
                   pl.BlockSpec((None, hd, 2, tm // tq, V_ROWS, tq), lambda i: (i // per_b, 0, 0, i % per_b, 0, 0)),
                   chunked, chunked, cols(NSA_WIDTH), cols(g_rows)),
        scratch_shapes=[pltpu.VMEM((tm, D_MODEL), BF16), pltpu.VMEM((2, tm, KV_WIDTH), F32)],
        compiler_params=_cparams("parallel"),
        name="front",
    )(x2, g, w_uv, w_k, w_vT, w_qT, w_gT, ln_g, ln_b, w_s, bs_full, out_g, g_slc, g_win)


def _cmp_kernel(c_ref, pos_ref, w1_ref, b1_ref, w2_ref, b2_ref, g_ref, o_ref, *, norm):
    dot = _dot_3pass if norm else _dot_1pass
    c = c_ref[...]
    nb, half = c.shape
    first = dot(c, w1_ref[:half, :])
    second = dot(c, w1_ref[half:, :])
    pos = dot(pos_ref[...], w1_ref[...])[0:1, :]
    h1 = jax.nn.gelu(first + pltpu.roll(second, nb - 1, 0) + pos + b1_ref[...])
    y = dot(h1, w2_ref[...]) + b2_ref[...]
    if norm:
        y = _rms(y, g_ref[...])
    o_ref[...] = y


def _compress(chunks, pos8, w1, b1, w2, b2, g, norm):
    bh, nb, half = chunks.shape
    return pl.pallas_call(
        functools.partial(_cmp_kernel, norm=norm),
        out_shape=jax.ShapeDtypeStruct((bh, nb, HEAD_DIM), F32),
        grid=(bh,),
        in_specs=[pl.BlockSpec((None, nb, half), lambda i: (i, 0, 0)),
                  _const_spec((8, 2 * half)), _const_spec((2 * half, CMP_HIDDEN)), _const_spec((1, CMP_HIDDEN)),
                  _const_spec((CMP_HIDDEN, HEAD_DIM)), _const_spec((1, HEAD_DIM)), _const_spec((1, HEAD_DIM))],
        out_specs=pl.BlockSpec((None, nb, HEAD_DIM), lambda i: (i, 0, 0)),
        compiler_params=_cparams("parallel"),
        name="compress_k" if norm else "compress_v",
    )(chunks, pos8, w1, b1, w2, b2, g)


def _stable_rank(score, score_sc, rank_sc, rows_sc, tq, interleave):
    interleave = list(interleave)
    n_slc = score.shape[0]
    for k in range(n_slc):
        rows_sc[k] = score[k:k + 1, :]
    score_sc[...] = score
    sub, k_chunk = 8, 16
    row_in_group = lax.broadcasted_iota(jnp.int32, (sub, tq), 0)
    for k0 in range(0, n_slc, k_chunk):
        if interleave:
            interleave.pop(0)()
        rks = [jnp.broadcast_to(rows_sc[k], (sub, tq)) for k in range(k0, k0 + k_chunk)]
        for r0 in range(0, n_slc, sub):
            sg = score_sc[r0:r0 + sub, :]
            acc = rank_sc[r0:r0 + sub, :] if k0 else jnp.zeros((sub, tq), F32)
            for k in range(k0, k0 + k_chunk):
                rk = rks[k - k0]
                ge = jnp.where(rk >= sg, 1.0, 0.0)
                gt = jnp.where(rk > sg, 1.0, 0.0)
                if r0 > k:
                    before = ge
                elif r0 + sub - 1 < k:
                    before = gt
                else:
                    before = jnp.where(row_in_group > k - r0, ge, gt)
                acc = acc + before
            rank_sc[r0:r0 + sub, :] = acc
    for rest in interleave:
        rest()


def _attn_kernel(qT_ref, glT_ref, kc_ref, vcT_ref, ovT_ref, kx_ref, kw_ref, vT_ref, mask_ref, cmask_ref, gq_ref,
                 shift_ref, o_ref, qw_sc, qx_sc, s_a, mo_a, mn_a, s_b, mo_b, mn_b, s_c, score_sc, rank_sc, rows_sc,
                 mrun_sc, acc_sc, *, tq, bounded):
    i = pl.program_id(2)
    d = HEAD_DIM
    G = NSA_GROUP
    shift = shift_ref[0]

    def lanes(g):
        return slice(g * tq, (g + 1) * tq)

    for g in range(G):
        qg = qT_ref[g * d:(g + 1) * d, :]
        qg = qg * lax.rsqrt(jnp.mean(qg * qg, axis=0, keepdims=True) + EPS) * gq_ref[...]
        qw_sc[0:d, lanes(g)] = qg.astype(BF16)
    qw_sc[d:2 * d, :] = jnp.full((d, G * tq), shift, F32).astype(BF16)

    gates = jax.nn.sigmoid(glT_ref[...])

    if not bounded:
        mrun_sc[...] = jnp.full(mrun_sc.shape, -jnp.inf, F32)
    acc_sc[...] = jnp.zeros(acc_sc.shape, F32)

    def scores(branch, k_ref, k_idx, q_sc, mask, s_buf, mo_buf, mn_buf):
        for g in range(G):
            s = jnp.dot(k_ref[k_idx], q_sc[:, lanes(g)], preferred_element_type=F32)
            if mask is not None:
                s = s + mask_ref[mask]
            s_buf[g] = s
            if not bounded:
                m_old = mrun_sc[branch, g]
                m_new = jnp.maximum(m_old, jnp.max(s, axis=0, keepdims=True))
                mo_buf[g] = m_old
                mn_buf[g] = m_new
                mrun_sc[branch, g] = m_new

    def accumulate(branch, v_idx, s_buf, mo_buf, mn_buf):
        for g in range(G):
            if bounded:
                p = jnp.exp2(s_buf[g]).astype(BF16)
                acc_sc[branch, g] += jnp.dot(vT_ref[branch, v_idx], p, preferred_element_type=F32)
            else:
                m_new = mn_buf[g]
                alpha = jnp.exp2(mo_buf[g] - m_new)
                p = jnp.exp2(s_buf[g] - m_new).astype(BF16)
                acc_sc[branch, g] = alpha * acc_sc[branch, g] + jnp.dot(vT_ref[branch, v_idx], p,
                                                                        preferred_element_type=F32)

    buf_a = (s_a, mo_a, mn_a)
    buf_b = (s_b, mo_b, mn_b)

    w_tiles = [jnp.maximum(i - 2, 0), jnp.maximum(i - 1, 0), i]
    w_masks = [jnp.where(i >= 2, 2, 3), jnp.where(i >= 1, 0, 3), 1]

    def window_head():
        scores(1, kw_ref, w_tiles[0], qw_sc, w_masks[0], *buf_a)
        scores(1, kw_ref, w_tiles[1], qw_sc, w_masks[1], *buf_b)

    def window_mid():
        accumulate(1, w_tiles[0], *buf_a)
        scores(1, kw_ref, w_tiles[2], qw_sc, w_masks[2], *buf_a)

    window_stages = [window_head, window_mid,
                     lambda: accumulate(1, w_tiles[1], *buf_b),
                     lambda: accumulate(1, w_tiles[2], *buf_a)]

    ncp = kc_ref.shape[0]
    t_row = i * tq + lax.broadcasted_iota(jnp.int32, (1, tq), 1)
    sees_any = jnp.where(t_row >= CMP_BLOCK - 1, 1.0, 0.0)
    c_max = []
    for g in range(G):
        s = jnp.dot(kc_ref[...], qw_sc[:, lanes(g)], preferred_element_type=F32) + cmask_ref[...]
        s_c[g] = s
        if not bounded:
            c_max.append(jnp.max(s, axis=0, keepdims=True))
    oc = []
    for g in range(G):
        e = jnp.exp2(s_c[g]) if bounded else jnp.exp2(s_c[g] - c_max[g])
        s_c[g] = e
        oc.append(jnp.dot(vcT_ref[...], e.astype(BF16), preferred_element_type=F32))
    psum = jnp.zeros((ncp, tq), F32)
    for g in range(G):
        den = oc[g][d:d + 1, :]
        r = sees_any / jnp.where(den > 0, den, 1.0)
        psum = psum + s_c[g] * r
        o_ref[g * d:(g + 1) * d, :] = (gates[g * 3:g * 3 + 1, :] * r) * oc[g][0:d, :]

    p_hi = psum.astype(BF16)
    p_lo = (psum - p_hi.astype(F32)).astype(BF16)
    imp = (jnp.dot(ovT_ref[...], p_hi, preferred_element_type=F32)
           + jnp.dot(ovT_ref[...], p_lo, preferred_element_type=F32))
    n_slc = imp.shape[0]
    j_idx = lax.broadcasted_iota(jnp.int32, (n_slc, tq), 0)
    t_blk = (i * tq + lax.broadcasted_iota(jnp.int32, (n_slc, tq), 1)) // SLC_BLOCK
    forced = (j_idx == 0) | (j_idx == t_blk) | (j_idx == t_blk - 1)
    score = jnp.where(forced, FORCE_SCORE, jnp.where(j_idx <= t_blk, imp, -FORCE_SCORE))
    _stable_rank(score, score_sc, rank_sc, rows_sc, tq, window_stages)
    n_sel = min(SLC_TOPK, n_slc)
    bias = jnp.where(rank_sc[...] < n_sel, shift, MASKED).astype(BF16)
    if n_slc < d:
        bias = jnp.concatenate([bias, jnp.zeros((d - n_slc, tq), BF16)], axis=0)
    qx_sc[0:d, :] = qw_sc[0:d, :]
    for g in range(G):
        qx_sc[d:2 * d, lanes(g)] = bias

    def slc_mask(j):
        return jnp.where(j == i, 1, jnp.where(j > i, 3, 0))

    def slc_tile(j):
        return jnp.minimum(j, i)

    scores(0, kx_ref, 0, qx_sc, slc_mask(0), *buf_a)
    n_pairs = jnp.maximum(i - 1, 0) // 2

    def pair(j0):
        scores(0, kx_ref, j0 + 1, qx_sc, None, *buf_b)
        accumulate(0, j0, *buf_a)
        scores(0, kx_ref, j0 + 2, qx_sc, None, *buf_a)
        accumulate(0, j0 + 1, *buf_b)

    def two_pairs(kk, carry):
        pair(4 * kk)
        pair(4 * kk + 2)
        return carry

    def one_pair(kk, carry):
        pair(4 * (n_pairs // 2))
        return carry

    lax.fori_loop(0, n_pairs // 2, two_pairs, 0)
    lax.fori_loop(0, n_pairs % 2, one_pair, 0)
    j0 = 2 * n_pairs
    scores(0, kx_ref, slc_tile(j0 + 1), qx_sc, slc_mask(j0 + 1), *buf_b)
    accumulate(0, j0, *buf_a)
    scores(0, kx_ref, slc_tile(j0 + 2), qx_sc, slc_mask(j0 + 2), *buf_a)
    accumulate(0, slc_tile(j0 + 1), *buf_b)
    accumulate(0, slc_tile(j0 + 2), *buf_a)

    for branch in range(2):
        for g in range(G):
            acc = acc_sc[branch, g]
            o = acc[0:d, :] * (1.0 / acc[d:d + 1, :])
            gate = gates[g * 3 + branch + 1:g * 3 + branch + 2, :]
            o_ref[g * d:(g + 1) * d, :] += gate * o


def _attention(qT, glT, kc_x, vcT_x, ovT, kx, kw, vT, masks, cmask, gq, shift, bounded):
    batch, _, seq = qT.shape
    tq = ATTN_TILE
    nq = seq // tq
    hd = NSA_KV_HEADS
    gw = NSA_GROUP * HEAD_DIM
    ncp = kc_x.shape[1]
    n_slc = ovT.shape[0]

    def per_head(shape):
        return pl.BlockSpec((None, None) + shape, lambda b, h, i: (b, h) + (0,) * len(shape))

    stage = [pltpu.VMEM((NSA_GROUP, tq, tq), F32),
             pltpu.VMEM((NSA_GROUP, 1, tq), F32),
             pltpu.VMEM((NSA_GROUP, 1, tq), F32)]
    return pl.pallas_call(
        functools.partial(_attn_kernel, tq=tq, bounded=bounded),
        out_shape=jax.ShapeDtypeStruct((batch, NSA_WIDTH, seq), F32),
        grid=(batch, hd, nq),
        in_specs=[pl.BlockSpec((None, gw, tq), lambda b, h, i: (b, h, i)),
                  pl.BlockSpec((None, GATE_ROWS, tq), lambda b, h, i: (b, h, i)),
                  pl.BlockSpec((None, ncp, 2 * HEAD_DIM), lambda b, h, i: (b * hd + h, 0, 0)),
                  pl.BlockSpec((None, V_ROWS, ncp), lambda b, h, i: (b * hd + h, 0, 0)),
                  _const_spec((n_slc, ncp)),
                  per_head((nq, tq, 2 * HEAD_DIM)), per_head((nq, tq, 2 * HEAD_DIM)),
                  per_head((2, nq, V_ROWS, tq)),
                  _const_spec((4, tq, tq)),
                  pl.BlockSpec((None, ncp, tq), lambda b, h, i: (i, 0, 0)),
                  _const_spec((HEAD_DIM, 1)),
                  pl.BlockSpec(memory_space=pltpu.SMEM)],
        out_specs=pl.BlockSpec((None, gw, tq), lambda b, h, i: (b, h, i)),
        scratch_shapes=[pltpu.VMEM((2 * HEAD_DIM, NSA_GROUP * tq), BF16),
                        pltpu.VMEM((2 * HEAD_DIM, NSA_GROUP * tq), BF16)] + stage + stage + [
            pltpu.VMEM((NSA_GROUP, ncp, tq), F32),
            pltpu.VMEM((n_slc, tq), F32), pltpu.VMEM((n_slc, tq), F32), pltpu.VMEM((n_slc, 1, tq), F32),
            pltpu.VMEM((2, NSA_GROUP, 1, tq), F32),
            pltpu.VMEM((2, NSA_GROUP, V_ROWS, tq), F32)],
        compiler_params=_cparams("parallel", "parallel", "arbitrary"),
        name="nsa_attention_bounded" if bounded else "nsa_attention",
    )(qT, glT, kc_x, vcT_x, ovT, kx, kw, vT, masks, cmask, gq, shift)


def _mem_kv_kernel(mem_ref, g_ref, w_ref, kg_ref, k_ref, v_ref):
    mn = _rms(mem_ref[...], g_ref[...]).astype(BF16)
    kv = jnp.dot(mn, w_ref[...], preferred_element_type=F32)
    width = MEM_HEADS * MEM_HEAD_DIM
    for hh in range(MEM_HEADS):
        cols = slice(hh * MEM_HEAD_DIM, (hh + 1) * MEM_HEAD_DIM)
        k_ref[:, cols] = _rms(kv[:, cols], kg_ref[...]).astype(BF16)
    v_ref[...] = kv[:, width:].astype(BF16)


def _mem_kv(mem, g, w_mkv, k_g):
    batch, m_len, _ = mem.shape
    width = MEM_HEADS * MEM_HEAD_DIM
    spec = pl.BlockSpec((None, m_len, width), lambda b: (b, 0, 0))
    return pl.pallas_call(
        _mem_kv_kernel,
        out_shape=(jax.ShapeDtypeStruct((batch, m_len, width), BF16),) * 2,
        grid=(batch,),
        in_specs=[pl.BlockSpec((None, m_len, D_MODEL), lambda b: (b, 0, 0)),
                  _const_spec((1, D_MODEL)), _const_spec((D_MODEL, 2 * width)),
                  _const_spec((1, MEM_HEAD_DIM))],
        out_specs=(spec, spec),
        compiler_params=_cparams("parallel"),
        name="mem_kv",
    )(mem, g, w_mkv, k_g)


def _tail_kernel(a_ref, bT_ref, x_ref, gb_ref, wout_ref, gm_ref, wq_ref, qg_ref, k_ref, v_ref, wo_ref,
                 gf_ref, w1_ref, w2_ref, o_ref):
    bT = bT_ref[...]
    bn = (bT * lax.rsqrt(jnp.mean(bT * bT, axis=0, keepdims=True) + EPS) * gb_ref[...]).astype(BF16)
    x = x_ref[...] + jnp.dot(a_ref[...], wout_ref[:SG_WIDTH, :], preferred_element_type=F32)
    x = x + lax.dot_general(bn, wout_ref[SG_WIDTH:, :], TN_DIMS, preferred_element_type=F32)

    h = _rms(x, gm_ref[...]).astype(BF16)
    q = jnp.dot(h, wq_ref[...], preferred_element_type=F32)
    outs = []
    for hh in range(MEM_HEADS):
        cols = slice(hh * MEM_HEAD_DIM, (hh + 1) * MEM_HEAD_DIM)
        qh = (_rms(q[:, cols], qg_ref[...]) * MEM_HEAD_DIM ** -0.5).astype(BF16)
        s = lax.dot_general(qh, k_ref[:, cols], NT_DIMS, preferred_element_type=F32)
        e = jnp.exp(s - jnp.max(s, axis=-1, keepdims=True))
        p = e * (1.0 / jnp.sum(e, axis=-1, keepdims=True))
        outs.append(jnp.dot(p.astype(BF16), v_ref[:, cols], preferred_element_type=F32))
    o = jnp.concatenate(outs, axis=-1).astype(BF16)
    x = x + jnp.dot(o, wo_ref[...], preferred_element_type=F32)

    h = _rms(x, gf_ref[...]).astype(BF16)
    step = 1024
    for c0 in range(0, D_FF, step):
        a = jnp.maximum(jnp.dot(h, w1_ref[:, c0:c0 + step], preferred_element_type=F32), 0.0)
        x = x + jnp.dot((a * a).astype(BF16), w2_ref[c0:c0 + step, :], preferred_element_type=F32)
    o_ref[...] = x


def _tail(a_n, bT, x2, g_b, w_out, g_mem, w_mq, q_g, mk, mv, w_mo, g_ffn, w1, w2, seq):
    n = x2.shape[0]
    tm = 512
    per_b = seq // tm
    m_len = mk.shape[1]
    width = MEM_HEADS * MEM_HEAD_DIM

    def weight(shape):
        return pl.BlockSpec(shape, lambda i: (0,) * len(shape), pipeline_mode=pl.Buffered(1))

    kv_spec = pl.BlockSpec((None, m_len, width), lambda i: (i // per_b, 0, 0))
    return pl.pallas_call(
        _tail_kernel,
        out_shape=jax.ShapeDtypeStruct((n, D_MODEL), F32),
        grid=(n // tm,),
        in_specs=[pl.BlockSpec((tm, SG_WIDTH), lambda i: (i, 0)),
                  pl.BlockSpec((None, NSA_WIDTH, tm), lambda i: (i // per_b, 0, i % per_b)),
                  pl.BlockSpec((tm, D_MODEL), lambda i: (i, 0)),
                  _const_spec((NSA_WIDTH, 1)), weight((D_MODEL, D_MODEL)),
                  _const_spec((1, D_MODEL)), weight((D_MODEL, width)), _const_spec((1, MEM_HEAD_DIM)),
                  kv_spec, kv_spec, weight((width, D_MODEL)),
                  _const_spec((1, D_MODEL)), weight((D_MODEL, D_FF)), weight((D_FF, D_MODEL))],
        out_specs=pl.BlockSpec((tm, D_MODEL), lambda i: (i, 0)),
        compiler_params=pltpu.CompilerParams(dimension_semantics=("parallel",), vmem_limit_bytes=TAIL_VMEM_LIMIT),
        name="tail",
    )(a_n, bT, x2, g_b, w_out, g_mem, w_mq, q_g, mk, mv, w_mo, g_ffn, w1, w2)


def _overlap_T(n_cmp_pad, n_cmp, n_slc):
    cs = jnp.arange(n_cmp_pad) * CMP_STRIDE
    ss = jnp.arange(n_slc) * SLC_BLOCK
    ov = jnp.clip(jnp.minimum(cs[None, :] + CMP_BLOCK, ss[:, None] + SLC_BLOCK)
                  - jnp.maximum(cs[None, :], ss[:, None]), 0, None).astype(F32) / CMP_BLOCK
    ov = jnp.where(jnp.arange(n_cmp_pad)[None, :] < n_cmp, ov, 0.0)
    return ov.astype(BF16)


def kernel(x, mem, norm_mix_g, w_in, sg_ln_g, sg_ln_b, sg_w, sg_b, q_norm_g, k_norm_g, cmp_pos, cmp_w1, cmp_b1, cmp_w2, cmp_b2, mix_out_g, w_out, norm_mem_g, mem_kv_norm_g, w_mq, w_mkv, mem_q_norm_g, mem_k_norm_g, w_mo, norm_ffn_g, w_ff1, w_ff2):
    batch, seq, _ = x.shape
    depth = w_in.shape[0]
    n = batch * seq
    tq = ATTN_TILE
    assert WINDOW == 2 * tq and seq % 512 == 0
    n_chunk = seq // CMP_STRIDE
    n_cmp = n_chunk - 1
    n_slc = seq // SLC_BLOCK
    assert n_slc <= HEAD_DIM
    ovT = _overlap_T(n_chunk, n_cmp, n_slc)
    row = lambda a: a.reshape(1, -1)
    r_idx = lax.broadcasted_iota(jnp.int32, (tq, tq), 0)
    c_idx = lax.broadcasted_iota(jnp.int32, (tq, tq), 1)
    masks = jnp.stack([jnp.zeros((tq, tq), F32), jnp.where(r_idx <= c_idx, 0.0, MASKED),
                       jnp.where(r_idx > c_idx, 0.0, MASKED), jnp.full((tq, tq), MASKED, F32)])
    blk_end = jnp.arange(n_chunk) * CMP_STRIDE + (CMP_BLOCK - 1)
    cmask = jnp.where((blk_end[:, None] <= jnp.arange(seq)[None, :]) & (jnp.arange(n_chunk)[:, None] < n_cmp),
                      0.0, MASKED).astype(F32).reshape(n_chunk, seq // tq, tq).transpose(1, 0, 2)

    x2 = x.reshape(n, D_MODEL)
    for l in range(depth):
        w = w_in[l].astype(BF16)
        w_gT = jnp.pad(w[:, COL_GATE:].T.reshape(NSA_KV_HEADS, GATES_PER_HEAD, D_MODEL),
                       ((0, 0), (0, GATE_ROWS - GATES_PER_HEAD), (0, 0))).reshape(NSA_KV_HEADS * GATE_ROWS, D_MODEL)
        stream = lambda s: w[:, COL_KV + s * KV_WIDTH:COL_KV + (s + 1) * KV_WIDTH]
        w_k = jnp.concatenate([stream(0), stream(1), stream(2), stream(4)], axis=1)
        w_vT = jnp.concatenate([stream(3), stream(5)], axis=1).T
        bs_full = jnp.repeat(sg_b[l].T, HEAD_DIM, axis=1)
        a_n, kx, kw, vT, k_cmp, v_cmp, qT, glT = _front(
            x2, row(norm_mix_g[l]), w[:, :COL_Q], w_k, w_vT, w[:, COL_Q:COL_KV].T, w_gT,
            row(sg_ln_g[l]), row(sg_ln_b[l]), sg_w[l], bs_full, row(mix_out_g[l, 0]),
            row(k_norm_g[l, 1]), row(k_norm_g[l, 2]), batch, seq)
        tiles = lambda a: a.reshape(batch, NSA_KV_HEADS, seq // tq, tq, a.shape[-1])
        chunks = lambda a: a.reshape(batch * NSA_KV_HEADS, n_chunk, CMP_STRIDE * HEAD_DIM)
        pos8 = lambda p: jnp.broadcast_to(p.reshape(1, -1), (8, CMP_BLOCK * HEAD_DIM))
        kc = _compress(chunks(k_cmp), pos8(cmp_pos[l, 0]), cmp_w1[l, 0], row(cmp_b1[l, 0]), cmp_w2[l, 0],
                       row(cmp_b2[l, 0]), row(k_norm_g[l, 0]), True)
        vc = _compress(chunks(v_cmp), pos8(cmp_pos[l, 1]), cmp_w1[l, 1], row(cmp_b1[l, 1]), cmp_w2[l, 1],
                       row(cmp_b2[l, 1]), row(k_norm_g[l, 0]), False)
        vcT = vc.transpose(0, 2, 1)

        gq = (q_norm_g[l] * (HEAD_DIM ** -0.5 * LOG2E)).reshape(HEAD_DIM, 1)
        kx, kw = tiles(kx), tiles(kw)
        peak = lambda g: jnp.max(jnp.abs(g))
        bound = HEAD_DIM ** 0.5 * LOG2E * peak(q_norm_g[l]) * peak(k_norm_g[l])

        kc_x = jnp.concatenate([kc, jnp.ones_like(kc[..., :1]), jnp.zeros_like(kc[..., 1:])], axis=-1).astype(BF16)
        vcT_x = jnp.concatenate([vcT, jnp.ones_like(vcT[:, :1]),
                                 jnp.zeros_like(vcT[:, :BF16_SUBLANES - 1])], axis=1).astype(BF16)
        ops = (qT, glT, kc_x, vcT_x, ovT, kx, kw, vT, masks, cmask, gq)
        oT = lax.cond(2.0 * bound <= MAX_SCORE_SPAN,
                      lambda ops, bound: _attention(*ops, (-bound).reshape(1), True),
                      lambda ops, bound: _attention(*ops, jnp.zeros((1,), F32), False),
                      ops, bound)

        mk, mv = _mem_kv(mem, row(mem_kv_norm_g[l]), w_mkv[l].astype(BF16), row(mem_k_norm_g[l]))
        x2 = _tail(a_n, oT, x2, mix_out_g[l, 1].reshape(NSA_WIDTH, 1), w_out[l].astype(BF16),
                   row(norm_mem_g[l]), w_mq[l].astype(BF16), row(mem_q_norm_g[l]), mk, mv, w_mo[l].astype(BF16),
                   row(norm_ffn_g[l]), w_ff1[l].astype(BF16), w_ff2[l].astype(BF16), seq)
    return x2.reshape(batch, seq, D_MODEL)
```

```python
import functools

import jax
import jax.numpy as jnp
from jax import lax
from jax.experimental import pallas as pl
from jax.experimental.pallas import tpu as pltpu

F32 = jnp.float32
BF16 = jnp.bfloat16

D_MODEL = 1024
HEAD_DIM = 64
SG_WIDTH = 512
SG_GROUPS = 8
SG_CHUNK = 128
NSA_WIDTH = 512
NSA_KV_HEADS = 2
NSA_GROUP = 4
KV_WIDTH = NSA_KV_HEADS * HEAD_DIM
N_BRANCH = 3
CMP_BLOCK = 32
CMP_STRIDE = 16
CMP_HIDDEN = 256
SLC_BLOCK = 64
SLC_TOPK = 16
WINDOW = 512
FORCE_SCORE = 1e4
MEM_HEADS = 4
MEM_HEAD_DIM = 128
D_FF = 4 * D_MODEL
EPS = 1e-6

COL_Q = 2 * SG_WIDTH
COL_KV = COL_Q + NSA_WIDTH
COL_GATE = COL_KV + 6 * KV_WIDTH
GATES_PER_HEAD = NSA_GROUP * N_BRANCH
GATE_ROWS = 16

MASKED = -1e30
LOG2E = 1.4426950408889634
MAX_SCORE_SPAN = 100.0
BF16_SUBLANES = 16
V_ROWS = HEAD_DIM + BF16_SUBLANES
ATTN_TILE = 256
VMEM_LIMIT = 48 * 1024 * 1024
TAIL_VMEM_LIMIT = 56 * 1024 * 1024

NT_DIMS = (((1,), (1,)), ((), ()))
TN_DIMS = (((0,), (0,)), ((), ()))


def _cparams(*sem):
    return pltpu.CompilerParams(dimension_semantics=sem, vmem_limit_bytes=VMEM_LIMIT)


def _rms(x, g):
    return x * lax.rsqrt(jnp.mean(x * x, axis=-1, keepdims=True) + EPS) * g


def _const_spec(shape):
    return pl.BlockSpec(shape, lambda *_: (0,) * len(shape))


def _dot_3pass(a, b):
    a_hi = a.astype(BF16)
    b_hi = b.astype(BF16)
    a_lo = (a - a_hi.astype(F32)).astype(BF16)
    b_lo = (b - b_hi.astype(F32)).astype(BF16)
    dot = functools.partial(jnp.dot, preferred_element_type=F32)
    return dot(a_hi, b_hi) + (dot(a_hi, b_lo) + dot(a_lo, b_hi))


def _dot_1pass(a, b):
    return jnp.dot(a.astype(BF16), b.astype(BF16), preferred_element_type=F32)


def _front_kernel(x_ref, g_ref, wuv_ref, wk_ref, wvT_ref, wqT_ref, wgT_ref,
                  lng_ref, lnb_ref, ws_ref, bs_ref, og_ref, gs_ref, gw_ref,
                  a_ref, kx_ref, kw_ref, vT_ref, kc_ref, vc_ref, qT_ref, gT_ref, h_sc, cmp_sc, *, tm, seq):
    h_sc[...] = _rms(x_ref[...], g_ref[...]).astype(BF16)
    tq = vT_ref.shape[-1]

    def queries():
        qT_ref[...] = lax.dot_general(wqT_ref[...], h_sc[...], NT_DIMS, preferred_element_type=F32)
        gT_ref[...] = lax.dot_general(wgT_ref[...], h_sc[...], NT_DIMS, preferred_element_type=F32)

    def values():
        vT = lax.dot_general(wvT_ref[...], h_sc[...], NT_DIMS, preferred_element_type=F32).astype(BF16)
        tail = jnp.where(lax.broadcasted_iota(jnp.int32, (BF16_SUBLANES, tq), 0) == 0, 1.0, 0.0).astype(BF16)
        for branch in range(2):
            for hd in range(NSA_KV_HEADS):
                r0 = (branch * NSA_KV_HEADS + hd) * HEAD_DIM
                for t in range(tm // tq):
                    vT_ref[hd, branch, t, :HEAD_DIM, :] = vT[r0:r0 + HEAD_DIM, t * tq:(t + 1) * tq]
                    vT_ref[hd, branch, t, HEAD_DIM:, :] = tail

    def keys(half):
        rows = slice(half * (tm // 2), (half + 1) * (tm // 2))
        kk = jnp.dot(h_sc[rows, :], wk_ref[...], preferred_element_type=F32)
        t0 = (pl.program_id(0) * tm) % seq + half * (tm // 2)
        blk = (t0 + lax.broadcasted_iota(jnp.int32, (tm // 2, SLC_BLOCK), 0)) // SLC_BLOCK
        onehot = jnp.where(blk == lax.broadcasted_iota(jnp.int32, (tm // 2, SLC_BLOCK), 1), 1.0, 0.0).astype(BF16)
        n_half = (tm // 2) // CMP_STRIDE
        chunk_rows = slice(half * n_half, (half + 1) * n_half)
        for out_ref, idx in ((kc_ref, 0), (vc_ref, 1)):
            cmp_sc[idx, rows, :] = kk[:, idx * KV_WIDTH:(idx + 1) * KV_WIDTH]
            for p in range(CMP_STRIDE):
                at_p = cmp_sc[idx, pl.ds(half * (tm // 2) + p, n_half, stride=CMP_STRIDE), :]
                for hd in range(NSA_KV_HEADS):
                    out_ref[hd, chunk_rows, p * HEAD_DIM:(p + 1) * HEAD_DIM] = at_p[:, hd * HEAD_DIM:(hd + 1) * HEAD_DIM]
        for hd in range(NSA_KV_HEADS):
            def part(idx):
                c0 = idx * KV_WIDTH + hd * HEAD_DIM
                return kk[:, c0:c0 + HEAD_DIM]
            kx_ref[hd, rows, :HEAD_DIM] = _rms(part(2), gs_ref[...]).astype(BF16)
            kx_ref[hd, rows, HEAD_DIM:] = onehot
            kw_ref[hd, rows, :HEAD_DIM] = _rms(part(3), gw_ref[...]).astype(BF16)
            kw_ref[hd, rows, HEAD_DIM:] = onehot

    row = lax.broadcasted_iota(jnp.int32, (SG_CHUNK, SG_CHUNK), 0)
    col = lax.broadcasted_iota(jnp.int32, (SG_CHUNK, SG_CHUNK), 1)
    causal = col <= row
    w = [jnp.where(causal, ws_ref[g], 0.0).astype(BF16) for g in range(SG_GROUPS)]
    between = [lambda: keys(0), lambda: keys(1), values, queries]
    for c in range(tm // SG_CHUNK):
        rows = slice(c * SG_CHUNK, (c + 1) * SG_CHUNK)
        uv = jnp.dot(h_sc[rows, :], wuv_ref[...], preferred_element_type=F32)
        between[c]()
        u = jax.nn.gelu(uv[:, :SG_WIDTH])
        v = jax.nn.gelu(uv[:, SG_WIDTH:])
        mu = jnp.mean(v, axis=-1, keepdims=True)
        vc = v - mu
        vn = vc * lax.rsqrt(jnp.mean(vc * vc, axis=-1, keepdims=True) + EPS) * lng_ref[...] + lnb_ref[...]
        vb = vn.astype(BF16)
        s = jnp.concatenate(
            [jnp.dot(w[g], vb[:, g * HEAD_DIM:(g + 1) * HEAD_DIM], preferred_element_type=F32)
             for g in range(SG_GROUPS)], axis=-1) + bs_ref[...]
        a_ref[rows, :] = _rms(u * s, og_ref[...]).astype(BF16)


def _front(x2, g, w_uv, w_k, w_vT, w_qT, w_gT, ln_g, ln_b, w_s, bs_full, out_g, g_slc, g_win, batch, seq):
    n = x2.shape[0]
    tm = 512
    tq = ATTN_TILE
    per_b = seq // tm
    hd = NSA_KV_HEADS
    g_rows = w_gT.shape[0]

    def per_head(width):
        return pl.BlockSpec((None, hd, tm, width), lambda i: (i // per_b, 0, i % per_b, 0))

    def per_head_shape(width, dtype):
        return jax.ShapeDtypeStruct((batch, hd, seq, width), dtype)

    def cols(height):
        return pl.BlockSpec((None, height, tm), lambda i: (i // per_b, 0, i % per_b))

    chunked_shape = jax.ShapeDtypeStruct((batch, hd, seq // CMP_STRIDE, CMP_STRIDE * HEAD_DIM), F32)
    chunked = pl.BlockSpec((None, hd, tm // CMP_STRIDE, CMP_STRIDE * HEAD_DIM),
                           lambda i: (i // per_b, 0, i % per_b, 0))

    return pl.pallas_call(
        functools.partial(_front_kernel, tm=tm, seq=seq),
        out_shape=(jax.ShapeDtypeStruct((n, SG_WIDTH), BF16),
                   per_head_shape(2 * HEAD_DIM, BF16), per_head_shape(2 * HEAD_DIM, BF16),
                   jax.ShapeDtypeStruct((batch, hd, 2, seq // tq, V_ROWS, tq), BF16),
                   chunked_shape, chunked_shape,
                   jax.ShapeDtypeStruct((batch, NSA_WIDTH, seq), F32),
                   jax.ShapeDtypeStruct((batch, g_rows, seq), F32)),
        grid=(n // tm,),
        in_specs=[pl.BlockSpec((tm, D_MODEL), lambda i: (i, 0)), _const_spec((1, D_MODEL)),
                  _const_spec((D_MODEL, 2 * SG_WIDTH)), _const_spec((D_MODEL, 4 * KV_WIDTH)),
                  _const_spec((2 * KV_WIDTH, D_MODEL)), _const_spec((NSA_WIDTH, D_MODEL)),
                  _const_spec((g_rows, D_MODEL)),
                  _const_spec((1, SG_WIDTH)), _const_spec((1, SG_WIDTH)),
                  _const_spec((SG_GROUPS, SG_CHUNK, SG_CHUNK)),
                  _const_spec((SG_CHUNK, SG_WIDTH)), _const_spec((1, SG_WIDTH)),
                  _const_spec((1, HEAD_DIM)), _const_spec((1, HEAD_DIM))],
        out_specs=(pl.BlockSpec((tm, SG_WIDTH), lambda i: (i, 0)),
                   per_head(2 * HEAD_DIM), per_head(2 * HEAD_DIM),
                   pl.BlockSpec((None, hd, 2, tm // tq, V_ROWS, tq), lambda i: (i // per_b, 0, 0, i % per_b, 0, 0)),
                   chunked, chunked, cols(NSA_WIDTH), cols(g_rows)),
        scratch_shapes=[pltpu.VMEM((tm, D_MODEL), BF16), pltpu.VMEM((2, tm, KV_WIDTH), F32)],
        compiler_params=_cparams("parallel"),
        name="front",
    )(x2, g, w_uv, w_k, w_vT, w_qT, w_gT, ln_g, ln_b, w_s, bs_full, out_g, g_slc, g_win)


def _cmp_kernel(c_ref, pos_ref, w1_ref, b1_ref, w2_ref, b2_ref, g_ref, o_ref, *, norm):
    dot = _dot_3pass if norm else _dot_1pass
    c = c_ref[...]
    nb, half = c.shape
    first = dot(c, w1_ref[:half, :])
    second = dot(c, w1_ref[half:, :])
    pos = dot(pos_ref[...], w1_ref[...])[0:1, :]
    h1 = jax.nn.gelu(first + pltpu.roll(second, nb - 1, 0) + pos + b1_ref[...])
    y = dot(h1, w2_ref[...]) + b2_ref[...]
    if norm:
        y = _rms(y, g_ref[...])
    o_ref[...] = y


def _compress(chunks, pos8, w1, b1, w2, b2, g, norm):
    bh, nb, half = chunks.shape
    return pl.pallas_call(
        functools.partial(_cmp_kernel, norm=norm),
        out_shape=jax.ShapeDtypeStruct((bh, nb, HEAD_DIM), F32),
        grid=(bh,),
        in_specs=[pl.BlockSpec((None, nb, half), lambda i: (i, 0, 0)),
                  _const_spec((8, 2 * half)), _const_spec((2 * half, CMP_HIDDEN)), _const_spec((1, CMP_HIDDEN)),
                  _const_spec((CMP_HIDDEN, HEAD_DIM)), _const_spec((1, HEAD_DIM)), _const_spec((1, HEAD_DIM))],
        out_specs=pl.BlockSpec((None, nb, HEAD_DIM), lambda i: (i, 0, 0)),
        compiler_params=_cparams("parallel"),
        name="compress_k" if norm else "compress_v",
    )(chunks, pos8, w1, b1, w2, b2, g)


def _stable_rank(score, score_sc, rank_sc, rows_sc, tq, interleave):
    interleave = list(interleave)
    n_slc = score.shape[0]
    for k in range(n_slc):
        rows_sc[k] = score[k:k + 1, :]
    score_sc[...] = score
    sub, k_chunk = 8, 16
    per_gap = -(-len(interleave) // (n_slc // k_chunk))
    row_in_group = lax.broadcasted_iota(jnp.int32, (sub, tq), 0)
    for k0 in range(0, n_slc, k_chunk):
        for _ in range(min(per_gap, len(interleave))):
            interleave.pop(0)()
        rks = [jnp.broadcast_to(rows_sc[k], (sub, tq)) for k in range(k0, k0 + k_chunk)]
        for r0 in range(0, n_slc, sub):
            sg = score_sc[r0:r0 + sub, :]
            acc = rank_sc[r0:r0 + sub, :] if k0 else jnp.zeros((sub, tq), F32)
            for k in range(k0, k0 + k_chunk):
                rk = rks[k - k0]
                ge = jnp.where(rk >= sg, 1.0, 0.0)
                gt = jnp.where(rk > sg, 1.0, 0.0)
                if r0 > k:
                    before = ge
                elif r0 + sub - 1 < k:
                    before = gt
                else:
                    before = jnp.where(row_in_group > k - r0, ge, gt)
                acc = acc + before
            rank_sc[r0:r0 + sub, :] = acc
    for rest in interleave:
        rest()


def _attn_kernel(qTa_ref, qTb_ref, glTa_ref, glTb_ref, kc_ref, vcT_ref, ovT_ref, kx_ref, kw_ref, vT_ref, mask_ref,
                 cma_ref, cmb_ref, gq_ref, shift_ref, oa_ref, ob_ref,
                 qw_sc, qx_sc, s_a, mo_a, mn_a, s_b, mo_b, mn_b, s_c, score_sc, rank_sc, rows_sc, mrun_sc, acc_sc,
                 *, tq, nq, bounded):
    i = pl.program_id(2)
    d = HEAD_DIM
    G = NSA_GROUP
    shift = shift_ref[0]
    n_slc = ovT_ref.shape[0]
    ncp = kc_ref.shape[0]
    q_tile = [i, nq - 1 - i]
    q_refs, gl_refs, cm_refs, o_refs = (qTa_ref, qTb_ref), (glTa_ref, glTb_ref), (cma_ref, cmb_ref), (oa_ref, ob_ref)

    def lanes(g):
        return slice(g * tq, (g + 1) * tq)

    if not bounded:
        mrun_sc[...] = jnp.full(mrun_sc.shape, -jnp.inf, F32)
    acc_sc[...] = jnp.zeros(acc_sc.shape, F32)
    gates = [jax.nn.sigmoid(gl_refs[slot][...]) for slot in range(2)]

    def scores(slot, branch, k_ref, k_idx, q_sc, mask, s_buf, mo_buf, mn_buf):
        for g in range(G):
            s = jnp.dot(k_ref[k_idx], q_sc[slot, :, lanes(g)], preferred_element_type=F32)
            if mask is not None:
                s = s + mask_ref[mask]
            s_buf[g] = s
            if not bounded:
                m_old = mrun_sc[slot, branch, g]
                m_new = jnp.maximum(m_old, jnp.max(s, axis=0, keepdims=True))
                mo_buf[g] = m_old
                mn_buf[g] = m_new
                mrun_sc[slot, branch, g] = m_new

    def accumulate(slot, branch, v_idx, s_buf, mo_buf, mn_buf):
        for g in range(G):
            if bounded:
                p = jnp.exp2(s_buf[g]).astype(BF16)
                acc_sc[slot, branch, g] += jnp.dot(vT_ref[branch, v_idx], p, preferred_element_type=F32)
            else:
                m_new = mn_buf[g]
                alpha = jnp.exp2(mo_buf[g] - m_new)
                p = jnp.exp2(s_buf[g] - m_new).astype(BF16)
                acc_sc[slot, branch, g] = alpha * acc_sc[slot, branch, g] + jnp.dot(
                    vT_ref[branch, v_idx], p, preferred_element_type=F32)

    buf_a = (s_a, mo_a, mn_a)
    buf_b = (s_b, mo_b, mn_b)

    def window_stages(slot):
        t = q_tile[slot]
        tiles = [jnp.maximum(t - 2, 0), jnp.maximum(t - 1, 0), t]
        w_masks = [jnp.where(t >= 2, 2, 3), jnp.where(t >= 1, 0, 3), 1]

        def head():
            scores(slot, 1, kw_ref, tiles[0], qw_sc, w_masks[0], *buf_a)
            scores(slot, 1, kw_ref, tiles[1], qw_sc, w_masks[1], *buf_b)

        def mid():
            accumulate(slot, 1, tiles[0], *buf_a)
            scores(slot, 1, kw_ref, tiles[2], qw_sc, w_masks[2], *buf_a)

        return [head, mid, lambda: accumulate(slot, 1, tiles[1], *buf_b),
                lambda: accumulate(slot, 1, tiles[2], *buf_a)]

    def queries(slot):
        for g in range(G):
            qg = q_refs[slot][g * d:(g + 1) * d, :]
            qg = qg * lax.rsqrt(jnp.mean(qg * qg, axis=0, keepdims=True) + EPS) * gq_ref[...]
            qw_sc[slot, 0:d, lanes(g)] = qg.astype(BF16)
        qw_sc[slot, d:2 * d, :] = jnp.full((d, G * tq), shift, F32).astype(BF16)

    def select(slot, before, inside):
        t = q_tile[slot]
        for stage in before:
            stage()
        t_row = t * tq + lax.broadcasted_iota(jnp.int32, (1, tq), 1)
        sees_any = jnp.where(t_row >= CMP_BLOCK - 1, 1.0, 0.0)
        c_max = []
        for g in range(G):
            s = jnp.dot(kc_ref[...], qw_sc[slot, :, lanes(g)], preferred_element_type=F32) + cm_refs[slot][...]
            s_c[g] = s
            if not bounded:
                c_max.append(jnp.max(s, axis=0, keepdims=True))
        oc = []
        for g in range(G):
            e = jnp.exp2(s_c[g]) if bounded else jnp.exp2(s_c[g] - c_max[g])
            s_c[g] = e
            oc.append(jnp.dot(vcT_ref[...], e.astype(BF16), preferred_element_type=F32))
        psum = jnp.zeros((ncp, tq), F32)
        for g in range(G):
            den = oc[g][d:d + 1, :]
            r = sees_any / jnp.where(den > 0, den, 1.0)
            psum = psum + s_c[g] * r
            o_refs[slot][g * d:(g + 1) * d, :] = (gates[slot][g * 3:g * 3 + 1, :] * r) * oc[g][0:d, :]
        p_hi = psum.astype(BF16)
        p_lo = (psum - p_hi.astype(F32)).astype(BF16)
        imp = (jnp.dot(ovT_ref[...], p_hi, preferred_element_type=F32)
               + jnp.dot(ovT_ref[...], p_lo, preferred_element_type=F32))
        j_idx = lax.broadcasted_iota(jnp.int32, (n_slc, tq), 0)
        t_blk = (t * tq + lax.broadcasted_iota(jnp.int32, (n_slc, tq), 1)) // SLC_BLOCK
        forced = (j_idx == 0) | (j_idx == t_blk) | (j_idx == t_blk - 1)
        score = jnp.where(forced, FORCE_SCORE, jnp.where(j_idx <= t_blk, imp, -FORCE_SCORE))
        _stable_rank(score, score_sc, rank_sc, rows_sc, tq, inside)
        bias = jnp.where(rank_sc[...] < min(SLC_TOPK, n_slc), shift, MASKED).astype(BF16)
        if n_slc < d:
            bias = jnp.concatenate([bias, jnp.zeros((d - n_slc, tq), BF16)], axis=0)
        qx_sc[slot, 0:d, :] = qw_sc[slot, 0:d, :]
        for g in range(G):
            qx_sc[slot, d:2 * d, lanes(g)] = bias

    n_static = nq // 2

    def job(k):
        if k < n_static:
            return 1, k, None
        if k == nq:
            return 0, i, 1
        in_b = k <= q_tile[1]
        return (jnp.where(in_b, 1, 0), jnp.where(in_b, k, k - q_tile[1] - 1),
                jnp.where(k == q_tile[1], 1, 0))

    bufs = [buf_a, buf_b]

    def sweep_stage(k):
        def stage():
            if k + 1 <= nq:
                slot, tile, mask = job(k + 1)
                scores(slot, 0, kx_ref, tile, qx_sc, mask, *bufs[(k + 1) % 2])
            if k >= 0:
                slot, tile, _ = job(k)
                accumulate(slot, 0, tile, *bufs[k % 2])
        return stage

    queries(0)
    queries(1)
    win0, win1 = window_stages(0), window_stages(1)
    select(1, win1[:1], win1[1:] + win0[:1])
    static_sweep = [sweep_stage(k) for k in range(-1, n_static - 1)]
    select(0, win0[1:3], win0[3:] + static_sweep)
    for k in range(n_static - 1, nq + 1):
        sweep_stage(k)()

    for slot in range(2):
        for g in range(G):
            o = o_refs[slot][g * d:(g + 1) * d, :]
            for branch in range(2):
                acc = acc_sc[slot, branch, g]
                gate = gates[slot][g * 3 + branch + 1:g * 3 + branch + 2, :]
                o = o + (gate * (1.0 / acc[d:d + 1, :])) * acc[0:d, :]
            o_refs[slot][g * d:(g + 1) * d, :] = o


def _attention(qT, glT, kc_x, vcT_x, ovT, kx, kw, vT, masks, cmask, gq, shift, bounded):
    batch, _, seq = qT.shape
    tq = ATTN_TILE
    nq = seq // tq
    half = nq // 2
    hd = NSA_KV_HEADS
    gw = NSA_GROUP * HEAD_DIM
    ncp = kc_x.shape[1]
    n_slc = ovT.shape[0]

    def per_head(shape):
        return pl.BlockSpec((None, None) + shape, lambda b, h, i: (b, h) + (0,) * len(shape))

    def other(i):
        return nq - 1 - i

    stage = [pltpu.VMEM((NSA_GROUP, tq, tq), F32),
             pltpu.VMEM((NSA_GROUP, 1, tq), F32),
             pltpu.VMEM((NSA_GROUP, 1, tq), F32)]
    out_half = jax.ShapeDtypeStruct((batch, NSA_WIDTH, seq // 2), F32)
    return pl.pallas_call(
        functools.partial(_attn_kernel, tq=tq, nq=nq, bounded=bounded),
        out_shape=(out_half, out_half),
        grid=(batch, hd, half),
        in_specs=[pl.BlockSpec((None, gw, tq), lambda b, h, i: (b, h, i)),
                  pl.BlockSpec((None, gw, tq), lambda b, h, i: (b, h, other(i))),
                  pl.BlockSpec((None, GATE_ROWS, tq), lambda b, h, i: (b, h, i)),
                  pl.BlockSpec((None, GATE_ROWS, tq), lambda b, h, i: (b, h, other(i))),
                  pl.BlockSpec((None, ncp, 2 * HEAD_DIM), lambda b, h, i: (b * hd + h, 0, 0)),
                  pl.BlockSpec((None, V_ROWS, ncp), lambda b, h, i: (b * hd + h, 0, 0)),
                  _const_spec((n_slc, ncp)),
                  per_head((nq, tq, 2 * HEAD_DIM)), per_head((nq, tq, 2 * HEAD_DIM)),
                  per_head((2, nq, V_ROWS, tq)),
                  _const_spec((4, tq, tq)),
                  pl.BlockSpec((None, ncp, tq), lambda b, h, i: (i, 0, 0)),
                  pl.BlockSpec((None, ncp, tq), lambda b, h, i: (other(i), 0, 0)),
                  _const_spec((HEAD_DIM, 1)),
                  pl.BlockSpec(memory_space=pltpu.SMEM)],
        out_specs=(pl.BlockSpec((None, gw, tq), lambda b, h, i: (b, h, i)),
                   pl.BlockSpec((None, gw, tq), lambda b, h, i: (b, h, other(i) - half))),
        scratch_shapes=[pltpu.VMEM((2, 2 * HEAD_DIM, NSA_GROUP * tq), BF16),
                        pltpu.VMEM((2, 2 * HEAD_DIM, NSA_GROUP * tq), BF16)] + stage + stage + [
            pltpu.VMEM((NSA_GROUP, ncp, tq), F32),
            pltpu.VMEM((n_slc, tq), F32), pltpu.VMEM((n_slc, tq), F32), pltpu.VMEM((n_slc, 1, tq), F32),
            pltpu.VMEM((2, 2, NSA_GROUP, 1, tq), F32),
            pltpu.VMEM((2, 2, NSA_GROUP, V_ROWS, tq), F32)],
        compiler_params=_cparams("parallel", "parallel", "arbitrary"),
        name="nsa_attention_bounded" if bounded else "nsa_attention",
    )(qT, qT, glT, glT, kc_x, vcT_x, ovT, kx, kw, vT, masks, cmask, cmask, gq, shift)


def _mem_kv_kernel(mem_ref, g_ref, w_ref, kg_ref, k_ref, v_ref):
    mn = _rms(mem_ref[...], g_ref[...]).astype(BF16)
    kv = jnp.dot(mn, w_ref[...], preferred_element_type=F32)
    width = MEM_HEADS * MEM_HEAD_DIM
    for hh in range(MEM_HEADS):
        cols = slice(hh * MEM_HEAD_DIM, (hh + 1) * MEM_HEAD_DIM)
        k_ref[:, cols] = _rms(kv[:, cols], kg_ref[...]).astype(BF16)
    v_ref[...] = kv[:, width:].astype(BF16)


def _mem_kv(mem, g, w_mkv, k_g):
    batch, m_len, _ = mem.shape
    width = MEM_HEADS * MEM_HEAD_DIM
    spec = pl.BlockSpec((None, m_len, width), lambda b: (b, 0, 0))
    return pl.pallas_call(
        _mem_kv_kernel,
        out_shape=(jax.ShapeDtypeStruct((batch, m_len, width), BF16),) * 2,
        grid=(batch,),
        in_specs=[pl.BlockSpec((None, m_len, D_MODEL), lambda b: (b, 0, 0)),
                  _const_spec((1, D_MODEL)), _const_spec((D_MODEL, 2 * width)),
                  _const_spec((1, MEM_HEAD_DIM))],
        out_specs=(spec, spec),
        compiler_params=_cparams("parallel"),
        name="mem_kv",
    )(mem, g, w_mkv, k_g)


def _tail_kernel(a_ref, bT_ref, x_ref, gb_ref, wout_ref, gm_ref, wq_ref, qg_ref, k_ref, v_ref, wo_ref,
                 gf_ref, w1_ref, w2_ref, o_ref):
    bT = bT_ref[...]
    bn = (bT * lax.rsqrt(jnp.mean(bT * bT, axis=0, keepdims=True) + EPS) * gb_ref[...]).astype(BF16)
    x = x_ref[...] + jnp.dot(a_ref[...], wout_ref[:SG_WIDTH, :], preferred_element_type=F32)
    x = x + lax.dot_general(bn, wout_ref[SG_WIDTH:, :], TN_DIMS, preferred_element_type=F32)

    h = _rms(x, gm_ref[...]).astype(BF16)
    q = jnp.dot(h, wq_ref[...], preferred_element_type=F32)
    outs = []
    for hh in range(MEM_HEADS):
        cols = slice(hh * MEM_HEAD_DIM, (hh + 1) * MEM_HEAD_DIM)
        qh = (_rms(q[:, cols], qg_ref[...]) * MEM_HEAD_DIM ** -0.5).astype(BF16)
        s = lax.dot_general(qh, k_ref[:, cols], NT_DIMS, preferred_element_type=F32)
        e = jnp.exp(s - jnp.max(s, axis=-1, keepdims=True))
        p = e * (1.0 / jnp.sum(e, axis=-1, keepdims=True))
        outs.append(jnp.dot(p.astype(BF16), v_ref[:, cols], preferred_element_type=F32))
    o = jnp.concatenate(outs, axis=-1).astype(BF16)
    x = x + jnp.dot(o, wo_ref[...], preferred_element_type=F32)

    h = _rms(x, gf_ref[...]).astype(BF16)
    step = 1024
    for c0 in range(0, D_FF, step):
        a = jnp.maximum(jnp.dot(h, w1_ref[:, c0:c0 + step], preferred_element_type=F32), 0.0)
        x = x + jnp.dot((a * a).astype(BF16), w2_ref[c0:c0 + step, :], preferred_element_type=F32)
    o_ref[...] = x


def _tail(a_n, bT, x2, g_b, w_out, g_mem, w_mq, q_g, mk, mv, w_mo, g_ffn, w1, w2, seq):
    n = x2.shape[0]
    tm = 512
    per_b = seq // tm
    m_len = mk.shape[1]
    width = MEM_HEADS * MEM_HEAD_DIM

    def weight(shape):
        return pl.BlockSpec(shape, lambda i: (0,) * len(shape), pipeline_mode=pl.Buffered(1))

    kv_spec = pl.BlockSpec((None, m_len, width), lambda i: (i // per_b, 0, 0))
    return pl.pallas_call(
        _tail_kernel,
        out_shape=jax.ShapeDtypeStruct((n, D_MODEL), F32),
        grid=(n // tm,),
        in_specs=[pl.BlockSpec((tm, SG_WIDTH), lambda i: (i, 0)),
                  pl.BlockSpec((None, NSA_WIDTH, tm), lambda i: (i // per_b, 0, i % per_b)),
                  pl.BlockSpec((tm, D_MODEL), lambda i: (i, 0)),
                  _const_spec((NSA_WIDTH, 1)), weight((D_MODEL, D_MODEL)),
                  _const_spec((1, D_MODEL)), weight((D_MODEL, width)), _const_spec((1, MEM_HEAD_DIM)),
                  kv_spec, kv_spec, weight((width, D_MODEL)),
                  _const_spec((1, D_MODEL)), weight((D_MODEL, D_FF)), weight((D_FF, D_MODEL))],
        out_specs=pl.BlockSpec((tm, D_MODEL), lambda i: (i, 0)),
        compiler_params=pltpu.CompilerParams(dimension_semantics=("parallel",), vmem_limit_bytes=TAIL_VMEM_LIMIT),
        name="tail",
    )(a_n, bT, x2, g_b, w_out, g_mem, w_mq, q_g, mk, mv, w_mo, g_ffn, w1, w2)


def _overlap_T(n_cmp_pad, n_cmp, n_slc):
    cs = jnp.arange(n_cmp_pad) * CMP_STRIDE
    ss = jnp.arange(n_slc) * SLC_BLOCK
    ov = jnp.clip(jnp.minimum(cs[None, :] + CMP_BLOCK, ss[:, None] + SLC_BLOCK)
                  - jnp.maximum(cs[None, :], ss[:, None]), 0, None).astype(F32) / CMP_BLOCK
    ov = jnp.where(jnp.arange(n_cmp_pad)[None, :] < n_cmp, ov, 0.0)
    return ov.astype(BF16)


def kernel(x, mem, norm_mix_g, w_in, sg_ln_g, sg_ln_b, sg_w, sg_b, q_norm_g, k_norm_g, cmp_pos, cmp_w1, cmp_b1, cmp_w2, cmp_b2, mix_out_g, w_out, norm_mem_g, mem_kv_norm_g, w_mq, w_mkv, mem_q_norm_g, mem_k_norm_g, w_mo, norm_ffn_g, w_ff1, w_ff2):
    batch, seq, _ = x.shape
    depth = w_in.shape[0]
    n = batch * seq
    tq = ATTN_TILE
    assert WINDOW == 2 * tq and seq % 512 == 0
    n_chunk = seq // CMP_STRIDE
    n_cmp = n_chunk - 1
    n_slc = seq // SLC_BLOCK
    assert n_slc <= HEAD_DIM
    ovT = _overlap_T(n_chunk, n_cmp, n_slc)
    row = lambda a: a.reshape(1, -1)
    r_idx = lax.broadcasted_iota(jnp.int32, (tq, tq), 0)
    c_idx = lax.broadcasted_iota(jnp.int32, (tq, tq), 1)
    masks = jnp.stack([jnp.zeros((tq, tq), F32), jnp.where(r_idx <= c_idx, 0.0, MASKED),
                       jnp.where(r_idx > c_idx, 0.0, MASKED), jnp.full((tq, tq), MASKED, F32)])
    blk_end = jnp.arange(n_chunk) * CMP_STRIDE + (CMP_BLOCK - 1)
    cmask = jnp.where((blk_end[:, None] <= jnp.arange(seq)[None, :]) & (jnp.arange(n_chunk)[:, None] < n_cmp),
                      0.0, MASKED).astype(F32).reshape(n_chunk, seq // tq, tq).transpose(1, 0, 2)

    x2 = x.reshape(n, D_MODEL)
    for l in range(depth):
        w = w_in[l].astype(BF16)
        w_gT = jnp.pad(w[:, COL_GATE:].T.reshape(NSA_KV_HEADS, GATES_PER_HEAD, D_MODEL),
                       ((0, 0), (0, GATE_ROWS - GATES_PER_HEAD), (0, 0))).reshape(NSA_KV_HEADS * GATE_ROWS, D_MODEL)
        stream = lambda s: w[:, COL_KV + s * KV_WIDTH:COL_KV + (s + 1) * KV_WIDTH]
        w_k = jnp.concatenate([stream(0), stream(1), stream(2), stream(4)], axis=1)
        w_vT = jnp.concatenate([stream(3), stream(5)], axis=1).T
        bs_full = jnp.repeat(sg_b[l].T, HEAD_DIM, axis=1)
        a_n, kx, kw, vT, k_cmp, v_cmp, qT, glT = _front(
            x2, row(norm_mix_g[l]), w[:, :COL_Q], w_k, w_vT, w[:, COL_Q:COL_KV].T, w_gT,
            row(sg_ln_g[l]), row(sg_ln_b[l]), sg_w[l], bs_full, row(mix_out_g[l, 0]),
            row(k_norm_g[l, 1]), row(k_norm_g[l, 2]), batch, seq)
        tiles = lambda a: a.reshape(batch, NSA_KV_HEADS, seq // tq, tq, a.shape[-1])
        chunks = lambda a: a.reshape(batch * NSA_KV_HEADS, n_chunk, CMP_STRIDE * HEAD_DIM)
        pos8 = lambda p: jnp.broadcast_to(p.reshape(1, -1), (8, CMP_BLOCK * HEAD_DIM))
        kc = _compress(chunks(k_cmp), pos8(cmp_pos[l, 0]), cmp_w1[l, 0], row(cmp_b1[l, 0]), cmp_w2[l, 0],
                       row(cmp_b2[l, 0]), row(k_norm_g[l, 0]), True)
        vc = _compress(chunks(v_cmp), pos8(cmp_pos[l, 1]), cmp_w1[l, 1], row(cmp_b1[l, 1]), cmp_w2[l, 1],
                       row(cmp_b2[l, 1]), row(k_norm_g[l, 0]), False)
        vcT = vc.transpose(0, 2, 1)

        gq = (q_norm_g[l] * (HEAD_DIM ** -0.5 * LOG2E)).reshape(HEAD_DIM, 1)
        kx, kw = tiles(kx), tiles(kw)
        peak = lambda g: jnp.max(jnp.abs(g))
        bound = HEAD_DIM ** 0.5 * LOG2E * peak(q_norm_g[l]) * peak(k_norm_g[l])

        kc_x = jnp.concatenate([kc, jnp.ones_like(kc[..., :1]), jnp.zeros_like(kc[..., 1:])], axis=-1).astype(BF16)
        vcT_x = jnp.concatenate([vcT, jnp.ones_like(vcT[:, :1]),
                                 jnp.zeros_like(vcT[:, :BF16_SUBLANES - 1])], axis=1).astype(BF16)
        ops = (qT, glT, kc_x, vcT_x, ovT, kx, kw, vT, masks, cmask, gq)
        oT = jnp.concatenate(
            lax.cond(2.0 * bound <= MAX_SCORE_SPAN,
                     lambda ops, bound: _attention(*ops, (-bound).reshape(1), True),
                     lambda ops, bound: _attention(*ops, jnp.zeros((1,), F32), False),
                     ops, bound), axis=2)

        mk, mv = _mem_kv(mem, row(mem_kv_norm_g[l]), w_mkv[l].astype(BF16), row(mem_k_norm_g[l]))
        x2 = _tail(a_n, oT, x2, mix_out_g[l, 1].reshape(NSA_WIDTH, 1), w_out[l].astype(BF16),
                   row(norm_mem_g[l]), w_mq[l].astype(BF16), row(mem_q_norm_g[l]), mk, mv, w_mo[l].astype(BF16),
                   row(norm_ffn_g[l]), w_ff1[l].astype(BF16), w_ff2[l].astype(BF16), seq)
    return x2.reshape(batch, seq, D_MODEL)
```

```python
import functools

import jax
import jax.numpy as jnp
from jax import lax
from jax.experimental import pallas as pl
from jax.experimental.pallas import tpu as pltpu

F32 = jnp.float32
BF16 = jnp.bfloat16

D_MODEL = 1024
HEAD_DIM = 64
SG_WIDTH = 512
SG_GROUPS = 8
SG_CHUNK = 128
NSA_WIDTH = 512
NSA_KV_HEADS = 2
NSA_GROUP = 4
KV_WIDTH = NSA_KV_HEADS * HEAD_DIM
N_BRANCH = 3
CMP_BLOCK = 32
CMP_STRIDE = 16
CMP_HIDDEN = 256
SLC_BLOCK = 64
SLC_TOPK = 16
WINDOW = 512
FORCE_SCORE = 1e4
MEM_HEADS = 4
MEM_HEAD_DIM = 128
D_FF = 4 * D_MODEL
EPS = 1e-6

COL_Q = 2 * SG_WIDTH
COL_KV = COL_Q + NSA_WIDTH
COL_GATE = COL_KV + 6 * KV_WIDTH
GATES_PER_HEAD = NSA_GROUP * N_BRANCH
GATE_ROWS = 16

MASKED = -1e30
LOG2E = 1.4426950408889634
MAX_SCORE_SPAN = 100.0
BF16_SUBLANES = 16
V_ROWS = HEAD_DIM + BF16_SUBLANES
ATTN_TILE = 256
VMEM_LIMIT = 48 * 1024 * 1024
TAIL_VMEM_LIMIT = 56 * 1024 * 1024

NT_DIMS = (((1,), (1,)), ((), ()))
TN_DIMS = (((0,), (0,)), ((), ()))


def _cparams(*sem):
    return pltpu.CompilerParams(dimension_semantics=sem, vmem_limit_bytes=VMEM_LIMIT)


def _rms(x, g):
    return x * lax.rsqrt(jnp.mean(x * x, axis=-1, keepdims=True) + EPS) * g


def _const_spec(shape):
    return pl.BlockSpec(shape, lambda *_: (0,) * len(shape))


def _dot_3pass(a, b):
    a_hi = a.astype(BF16)
    b_hi = b.astype(BF16)
    a_lo = (a - a_hi.astype(F32)).astype(BF16)
    b_lo = (b - b_hi.astype(F32)).astype(BF16)
    dot = functools.partial(jnp.dot, preferred_element_type=F32)
    return dot(a_hi, b_hi) + (dot(a_hi, b_lo) + dot(a_lo, b_hi))


def _dot_1pass(a, b):
    return jnp.dot(a.astype(BF16), b.astype(BF16), preferred_element_type=F32)


def _front_kernel(x_ref, g_ref, wuv_ref, wk_ref, wvT_ref, wqT_ref, wgT_ref,
                  lng_ref, lnb_ref, ws_ref, bs_ref, og_ref, gs_ref, gw_ref,
                  a_ref, kx_ref, kw_ref, vT_ref, kc_ref, vc_ref, qT_ref, gT_ref, h_sc, cmp_sc, *, tm, seq):
    h_sc[...] = _rms(x_ref[...], g_ref[...]).astype(BF16)
    tq = vT_ref.shape[-1]

    def queries():
        qT_ref[...] = lax.dot_general(wqT_ref[...], h_sc[...], NT_DIMS, preferred_element_type=F32)
        gT_ref[...] = lax.dot_general(wgT_ref[...], h_sc[...], NT_DIMS, preferred_element_type=F32)

    def values():
        vT = lax.dot_general(wvT_ref[...], h_sc[...], NT_DIMS, preferred_element_type=F32).astype(BF16)
        tail = jnp.where(lax.broadcasted_iota(jnp.int32, (BF16_SUBLANES, tq), 0) == 0, 1.0, 0.0).astype(BF16)
        for branch in range(2):
            for hd in range(NSA_KV_HEADS):
                r0 = (branch * NSA_KV_HEADS + hd) * HEAD_DIM
                for t in range(tm // tq):
                    vT_ref[hd, branch, t, :HEAD_DIM, :] = vT[r0:r0 + HEAD_DIM, t * tq:(t + 1) * tq]
                    vT_ref[hd, branch, t, HEAD_DIM:, :] = tail

    def keys(half):
        rows = slice(half * (tm // 2), (half + 1) * (tm // 2))
        kk = jnp.dot(h_sc[rows, :], wk_ref[...], preferred_element_type=F32)
        t0 = (pl.program_id(0) * tm) % seq + half * (tm // 2)
        blk = (t0 + lax.broadcasted_iota(jnp.int32, (tm // 2, SLC_BLOCK), 0)) // SLC_BLOCK
        onehot = jnp.where(blk == lax.broadcasted_iota(jnp.int32, (tm // 2, SLC_BLOCK), 1), 1.0, 0.0).astype(BF16)
        n_half = (tm // 2) // CMP_STRIDE
        chunk_rows = slice(half * n_half, (half + 1) * n_half)
        for out_ref, idx in ((kc_ref, 0), (vc_ref, 1)):
            cmp_sc[idx, rows, :] = kk[:, idx * KV_WIDTH:(idx + 1) * KV_WIDTH]
            for p in range(CMP_STRIDE):
                at_p = cmp_sc[idx, pl.ds(half * (tm // 2) + p, n_half, stride=CMP_STRIDE), :]
                for hd in range(NSA_KV_HEADS):
                    out_ref[hd, chunk_rows, p * HEAD_DIM:(p + 1) * HEAD_DIM] = at_p[:, hd * HEAD_DIM:(hd + 1) * HEAD_DIM]
        for hd in range(NSA_KV_HEADS):
            def part(idx):
                c0 = idx * KV_WIDTH + hd * HEAD_DIM
                return kk[:, c0:c0 + HEAD_DIM]
            kx_ref[hd, rows, :HEAD_DIM] = _rms(part(2), gs_ref[...]).astype(BF16)
            kx_ref[hd, rows, HEAD_DIM:] = onehot
            kw_ref[hd, rows, :HEAD_DIM] = _rms(part(3), gw_ref[...]).astype(BF16)
            kw_ref[hd, rows, HEAD_DIM:] = onehot

    row = lax.broadcasted_iota(jnp.int32, (SG_CHUNK, SG_CHUNK), 0)
    col = lax.broadcasted_iota(jnp.int32, (SG_CHUNK, SG_CHUNK), 1)
    causal = col <= row
    w = [jnp.where(causal, ws_ref[g], 0.0).astype(BF16) for g in range(SG_GROUPS)]
    between = [lambda: keys(0), lambda: keys(1), values, queries]
    for c in range(tm // SG_CHUNK):
        rows = slice(c * SG_CHUNK, (c + 1) * SG_CHUNK)
        uv = jnp.dot(h_sc[rows, :], wuv_ref[...], preferred_element_type=F32)
        between[c]()
        u = jax.nn.gelu(uv[:, :SG_WIDTH])
        v = jax.nn.gelu(uv[:, SG_WIDTH:])
        mu = jnp.mean(v, axis=-1, keepdims=True)
        vc = v - mu
        vn = vc * lax.rsqrt(jnp.mean(vc * vc, axis=-1, keepdims=True) + EPS) * lng_ref[...] + lnb_ref[...]
        vb = vn.astype(BF16)
        s = jnp.concatenate(
            [jnp.dot(w[g], vb[:, g * HEAD_DIM:(g + 1) * HEAD_DIM], preferred_element_type=F32)
             for g in range(SG_GROUPS)], axis=-1) + bs_ref[...]
        a_ref[rows, :] = _rms(u * s, og_ref[...]).astype(BF16)


def _front(x2, g, w_uv, w_k, w_vT, w_qT, w_gT, ln_g, ln_b, w_s, bs_full, out_g, g_slc, g_win, batch, seq):
    n = x2.shape[0]
    tm = 512
    tq = ATTN_TILE
    per_b = seq // tm
    hd = NSA_KV_HEADS
    g_rows = w_gT.shape[0]

    def per_head(width):
        return pl.BlockSpec((None, hd, tm, width), lambda i: (i // per_b, 0, i % per_b, 0))

    def per_head_shape(width, dtype):
        return jax.ShapeDtypeStruct((batch, hd, seq, width), dtype)

    def cols(height):
        return pl.BlockSpec((None, height, tm), lambda i: (i // per_b, 0, i % per_b))

    chunked_shape = jax.ShapeDtypeStruct((batch, hd, seq // CMP_STRIDE, CMP_STRIDE * HEAD_DIM), F32)
    chunked = pl.BlockSpec((None, hd, tm // CMP_STRIDE, CMP_STRIDE * HEAD_DIM),
                           lambda i: (i // per_b, 0, i % per_b, 0))

    return pl.pallas_call(
        functools.partial(_front_kernel, tm=tm, seq=seq),
        out_shape=(jax.ShapeDtypeStruct((n, SG_WIDTH), BF16),
                   per_head_shape(2 * HEAD_DIM, BF16), per_head_shape(2 * HEAD_DIM, BF16),
                   jax.ShapeDtypeStruct((batch, hd, 2, seq // tq, V_ROWS, tq), BF16),
                   chunked_shape, chunked_shape,
                   jax.ShapeDtypeStruct((batch, NSA_WIDTH, seq), F32),
                   jax.ShapeDtypeStruct((batch, g_rows, seq), F32)),
        grid=(n // tm,),
        in_specs=[pl.BlockSpec((tm, D_MODEL), lambda i: (i, 0)), _const_spec((1, D_MODEL)),
                  _const_spec((D_MODEL, 2 * SG_WIDTH)), _const_spec((D_MODEL, 4 * KV_WIDTH)),
                  _const_spec((2 * KV_WIDTH, D_MODEL)), _const_spec((NSA_WIDTH, D_MODEL)),
                  _const_spec((g_rows, D_MODEL)),
                  _const_spec((1, SG_WIDTH)), _const_spec((1, SG_WIDTH)),
                  _const_spec((SG_GROUPS, SG_CHUNK, SG_CHUNK)),
                  _const_spec((SG_CHUNK, SG_WIDTH)), _const_spec((1, SG_WIDTH)),
                  _const_spec((1, HEAD_DIM)), _const_spec((1, HEAD_DIM))],
        out_specs=(pl.BlockSpec((tm, SG_WIDTH), lambda i: (i, 0)),
                   per_head(2 * HEAD_DIM), per_head(2 * HEAD_DIM),
                   pl.BlockSpec((None, hd, 2, tm // tq, V_ROWS, tq), lambda i: (i // per_b, 0, 0, i % per_b, 0, 0)),
                   chunked, chunked, cols(NSA_WIDTH), cols(g_rows)),
        scratch_shapes=[pltpu.VMEM((tm, D_MODEL), BF16), pltpu.VMEM((2, tm, KV_WIDTH), F32)],
        compiler_params=_cparams("parallel"),
        name="front",
    )(x2, g, w_uv, w_k, w_vT, w_qT, w_gT, ln_g, ln_b, w_s, bs_full, out_g, g_slc, g_win)


def _cmp_kernel(c_ref, pos_ref, w1_ref, b1_ref, w2_ref, b2_ref, g_ref, o_ref, *, norm):
    dot = _dot_3pass if norm else _dot_1pass
    c = c_ref[...]
    nb, half = c.shape
    first = dot(c, w1_ref[:half, :])
    second = dot(c, w1_ref[half:, :])
    pos = dot(pos_ref[...], w1_ref[...])[0:1, :]
    h1 = jax.nn.gelu(first + pltpu.roll(second, nb - 1, 0) + pos + b1_ref[...])
    y = dot(h1, w2_ref[...]) + b2_ref[...]
    if norm:
        y = _rms(y, g_ref[...])
    o_ref[...] = y


def _compress(chunks, pos8, w1, b1, w2, b2, g, norm):
    bh, nb, half = chunks.shape
    return pl.pallas_call(
        functools.partial(_cmp_kernel, norm=norm),
        out_shape=jax.ShapeDtypeStruct((bh, nb, HEAD_DIM), F32),
        grid=(bh,),
        in_specs=[pl.BlockSpec((None, nb, half), lambda i: (i, 0, 0)),
                  _const_spec((8, 2 * half)), _const_spec((2 * half, CMP_HIDDEN)), _const_spec((1, CMP_HIDDEN)),
                  _const_spec((CMP_HIDDEN, HEAD_DIM)), _const_spec((1, HEAD_DIM)), _const_spec((1, HEAD_DIM))],
        out_specs=pl.BlockSpec((None, nb, HEAD_DIM), lambda i: (i, 0, 0)),
        compiler_params=_cparams("parallel"),
        name="compress_k" if norm else "compress_v",
    )(chunks, pos8, w1, b1, w2, b2, g)


def _stable_rank(score, score_sc, rank_sc, rows_sc, tq, interleave):
    interleave = list(interleave)
    n_slc = score.shape[0]
    for k in range(n_slc):
        rows_sc[k] = score[k:k + 1, :]
    score_sc[0:n_slc, :] = score
    sub, k_chunk = 8, 16
    per_gap = -(-len(interleave) // (n_slc // k_chunk))
    row_in_group = lax.broadcasted_iota(jnp.int32, (sub, tq), 0)
    for k0 in range(0, n_slc, k_chunk):
        for _ in range(min(per_gap, len(interleave))):
            interleave.pop(0)()
        rks = [jnp.broadcast_to(rows_sc[k], (sub, tq)) for k in range(k0, k0 + k_chunk)]
        for r0 in range(0, n_slc, sub):
            sg = score_sc[r0:r0 + sub, :]
            acc = rank_sc[r0:r0 + sub, :] if k0 else jnp.zeros((sub, tq), F32)
            for k in range(k0, k0 + k_chunk):
                rk = rks[k - k0]
                ge = jnp.where(rk >= sg, 1.0, 0.0)
                gt = jnp.where(rk > sg, 1.0, 0.0)
                if r0 > k:
                    before = ge
                elif r0 + sub - 1 < k:
                    before = gt
                else:
                    before = jnp.where(row_in_group > k - r0, ge, gt)
                acc = acc + before
            rank_sc[r0:r0 + sub, :] = acc
    for rest in interleave:
        rest()


def _attn_kernel(qTa_ref, qTb_ref, glTa_ref, glTb_ref, kc_ref, vcT_ref, ovT_ref, kx_ref, kw_ref, vT_ref, mask_ref,
                 cma_ref, cmb_ref, gq_ref, shift_ref, oa_ref, ob_ref,
                 qw_sc, qx_sc, s_a, mo_a, mn_a, s_b, mo_b, mn_b, s_c, score_sc, rank_sc, rows_sc, mrun_sc, acc_sc,
                 *, tq, nq, bounded):
    i = pl.program_id(2)
    d = HEAD_DIM
    G = NSA_GROUP
    shift = shift_ref[0]
    n_slc = ovT_ref.shape[0]
    ncp = kc_ref.shape[0]
    q_tile = [i, nq - 1 - i]
    q_refs, gl_refs, cm_refs, o_refs = (qTa_ref, qTb_ref), (glTa_ref, glTb_ref), (cma_ref, cmb_ref), (oa_ref, ob_ref)

    def lanes(g):
        return slice(g * tq, (g + 1) * tq)

    if not bounded:
        mrun_sc[...] = jnp.full(mrun_sc.shape, -jnp.inf, F32)
    acc_sc[...] = jnp.zeros(acc_sc.shape, F32)
    gates = [jax.nn.sigmoid(gl_refs[slot][...]) for slot in range(2)]

    def scores(slot, branch, k_ref, k_idx, q_sc, mask, s_buf, mo_buf, mn_buf):
        for g in range(G):
            s = jnp.dot(k_ref[k_idx], q_sc[slot, :, lanes(g)], preferred_element_type=F32)
            if mask is not None:
                s = s + mask_ref[mask]
            s_buf[g] = s
            if not bounded:
                m_old = mrun_sc[slot, branch, g]
                m_new = jnp.maximum(m_old, jnp.max(s, axis=0, keepdims=True))
                mo_buf[g] = m_old
                mn_buf[g] = m_new
                mrun_sc[slot, branch, g] = m_new

    def accumulate(slot, branch, v_idx, s_buf, mo_buf, mn_buf):
        for g in range(G):
            if bounded:
                p = jnp.exp2(s_buf[g]).astype(BF16)
                acc_sc[slot, branch, g] += jnp.dot(vT_ref[branch, v_idx], p, preferred_element_type=F32)
            else:
                m_new = mn_buf[g]
                alpha = jnp.exp2(mo_buf[g] - m_new)
                p = jnp.exp2(s_buf[g] - m_new).astype(BF16)
                acc_sc[slot, branch, g] = alpha * acc_sc[slot, branch, g] + jnp.dot(
                    vT_ref[branch, v_idx], p, preferred_element_type=F32)

    buf_a = (s_a, mo_a, mn_a)
    buf_b = (s_b, mo_b, mn_b)

    def window_stages(slot):
        t = q_tile[slot]
        tiles = [jnp.maximum(t - 2, 0), jnp.maximum(t - 1, 0), t]
        w_masks = [jnp.where(t >= 2, 2, 3), jnp.where(t >= 1, 0, 3), 1]

        def head():
            scores(slot, 1, kw_ref, tiles[0], qw_sc, w_masks[0], *buf_a)
            scores(slot, 1, kw_ref, tiles[1], qw_sc, w_masks[1], *buf_b)

        def mid():
            accumulate(slot, 1, tiles[0], *buf_a)
            scores(slot, 1, kw_ref, tiles[2], qw_sc, w_masks[2], *buf_a)

        return [head, mid, lambda: accumulate(slot, 1, tiles[1], *buf_b),
                lambda: accumulate(slot, 1, tiles[2], *buf_a)]

    def queries(slot):
        for g in range(G):
            qg = q_refs[slot][g * d:(g + 1) * d, :]
            qg = qg * lax.rsqrt(jnp.mean(qg * qg, axis=0, keepdims=True) + EPS) * gq_ref[...]
            qw_sc[slot, 0:d, lanes(g)] = qg.astype(BF16)
        qw_sc[slot, d:2 * d, :] = jnp.full((d, G * tq), shift, F32).astype(BF16)

    def select(slot, before, inside):
        t = q_tile[slot]
        n_blk = n_slc if slot else n_slc // 2
        n_row = ncp if slot else ncp // 2
        for stage in before:
            stage()
        t_row = t * tq + lax.broadcasted_iota(jnp.int32, (1, tq), 1)
        sees_any = jnp.where(t_row >= CMP_BLOCK - 1, 1.0, 0.0)
        c_max = []
        for g in range(G):
            s = (jnp.dot(kc_ref[0:n_row, :], qw_sc[slot, :, lanes(g)], preferred_element_type=F32)
                 + cm_refs[slot][0:n_row, :])
            s_c[g, 0:n_row, :] = s
            if not bounded:
                c_max.append(jnp.max(s, axis=0, keepdims=True))
        oc = []
        for g in range(G):
            e = jnp.exp2(s_c[g, 0:n_row, :]) if bounded else jnp.exp2(s_c[g, 0:n_row, :] - c_max[g])
            s_c[g, 0:n_row, :] = e
            oc.append(jnp.dot(vcT_ref[:, 0:n_row], e.astype(BF16), preferred_element_type=F32))
        psum = jnp.zeros((n_row, tq), F32)
        for g in range(G):
            den = oc[g][d:d + 1, :]
            r = sees_any / jnp.where(den > 0, den, 1.0)
            psum = psum + s_c[g, 0:n_row, :] * r
            o_refs[slot][g * d:(g + 1) * d, :] = (gates[slot][g * 3:g * 3 + 1, :] * r) * oc[g][0:d, :]
        p_hi = psum.astype(BF16)
        p_lo = (psum - p_hi.astype(F32)).astype(BF16)
        imp = (jnp.dot(ovT_ref[0:n_blk, 0:n_row], p_hi, preferred_element_type=F32)
               + jnp.dot(ovT_ref[0:n_blk, 0:n_row], p_lo, preferred_element_type=F32))
        j_idx = lax.broadcasted_iota(jnp.int32, (n_blk, tq), 0)
        t_blk = (t * tq + lax.broadcasted_iota(jnp.int32, (n_blk, tq), 1)) // SLC_BLOCK
        forced = (j_idx == 0) | (j_idx == t_blk) | (j_idx == t_blk - 1)
        score = jnp.where(forced, FORCE_SCORE, jnp.where(j_idx <= t_blk, imp, -FORCE_SCORE))
        _stable_rank(score, score_sc, rank_sc, rows_sc, tq, inside)
        bias = jnp.where(rank_sc[0:n_blk, :] < min(SLC_TOPK, n_slc), shift, MASKED).astype(BF16)
        if n_blk < d:
            bias = jnp.concatenate([bias, jnp.full((d - n_blk, tq), MASKED, BF16)], axis=0)
        qx_sc[slot, 0:d, :] = qw_sc[slot, 0:d, :]
        for g in range(G):
            qx_sc[slot, d:2 * d, lanes(g)] = bias

    n_static = nq // 2

    def job(k):
        if k < n_static:
            return 1, k, None
        if k == nq:
            return 0, i, 1
        in_b = k <= q_tile[1]
        return (jnp.where(in_b, 1, 0), jnp.where(in_b, k, k - q_tile[1] - 1),
                jnp.where(k == q_tile[1], 1, 0))

    bufs = [buf_a, buf_b]

    def sweep_stage(k):
        def stage():
            if k + 1 <= nq:
                slot, tile, mask = job(k + 1)
                scores(slot, 0, kx_ref, tile, qx_sc, mask, *bufs[(k + 1) % 2])
            if k >= 0:
                slot, tile, _ = job(k)
                accumulate(slot, 0, tile, *bufs[k % 2])
        return stage

    queries(0)
    queries(1)
    win0, win1 = window_stages(0), window_stages(1)
    select(1, win1[:1], win1[1:] + win0[:1])
    static_sweep = [sweep_stage(k) for k in range(-1, n_static - 1)]
    select(0, win0[1:3], win0[3:] + static_sweep)
    for k in range(n_static - 1, nq + 1):
        sweep_stage(k)()

    for slot in range(2):
        for g in range(G):
            o = o_refs[slot][g * d:(g + 1) * d, :]
            for branch in range(2):
                acc = acc_sc[slot, branch, g]
                gate = gates[slot][g * 3 + branch + 1:g * 3 + branch + 2, :]
                o = o + (gate * (1.0 / acc[d:d + 1, :])) * acc[0:d, :]
            o_refs[slot][g * d:(g + 1) * d, :] = o


def _attention(qT, glT, kc_x, vcT_x, ovT, kx, kw, vT, masks, cmask, gq, shift, bounded):
    batch, _, seq = qT.shape
    tq = ATTN_TILE
    nq = seq // tq
    half = nq // 2
    hd = NSA_KV_HEADS
    gw = NSA_GROUP * HEAD_DIM
    ncp = kc_x.shape[1]
    n_slc = ovT.shape[0]

    def per_head(shape):
        return pl.BlockSpec((None, None) + shape, lambda b, h, i: (b, h) + (0,) * len(shape))

    def other(i):
        return nq - 1 - i

    stage = [pltpu.VMEM((NSA_GROUP, tq, tq), F32),
             pltpu.VMEM((NSA_GROUP, 1, tq), F32),
             pltpu.VMEM((NSA_GROUP, 1, tq), F32)]
    out_half = jax.ShapeDtypeStruct((batch, NSA_WIDTH, seq // 2), F32)
    return pl.pallas_call(
        functools.partial(_attn_kernel, tq=tq, nq=nq, bounded=bounded),
        out_shape=(out_half, out_half),
        grid=(batch, hd, half),
        in_specs=[pl.BlockSpec((None, gw, tq), lambda b, h, i: (b, h, i)),
                  pl.BlockSpec((None, gw, tq), lambda b, h, i: (b, h, other(i))),
                  pl.BlockSpec((None, GATE_ROWS, tq), lambda b, h, i: (b, h, i)),
                  pl.BlockSpec((None, GATE_ROWS, tq), lambda b, h, i: (b, h, other(i))),
                  pl.BlockSpec((None, ncp, 2 * HEAD_DIM), lambda b, h, i: (b * hd + h, 0, 0)),
                  pl.BlockSpec((None, V_ROWS, ncp), lambda b, h, i: (b * hd + h, 0, 0)),
                  _const_spec((n_slc, ncp)),
                  per_head((nq, tq, 2 * HEAD_DIM)), per_head((nq, tq, 2 * HEAD_DIM)),
                  per_head((2, nq, V_ROWS, tq)),
                  _const_spec((4, tq, tq)),
                  pl.BlockSpec((None, ncp, tq), lambda b, h, i: (i, 0, 0)),
                  pl.BlockSpec((None, ncp, tq), lambda b, h, i: (other(i), 0, 0)),
                  _const_spec((HEAD_DIM, 1)),
                  pl.BlockSpec(memory_space=pltpu.SMEM)],
        out_specs=(pl.BlockSpec((None, gw, tq), lambda b, h, i: (b, h, i)),
                   pl.BlockSpec((None, gw, tq), lambda b, h, i: (b, h, other(i) - half))),
        scratch_shapes=[pltpu.VMEM((2, 2 * HEAD_DIM, NSA_GROUP * tq), BF16),
                        pltpu.VMEM((2, 2 * HEAD_DIM, NSA_GROUP * tq), BF16)] + stage + stage + [
            pltpu.VMEM((NSA_GROUP, ncp, tq), F32),
            pltpu.VMEM((n_slc, tq), F32), pltpu.VMEM((n_slc, tq), F32), pltpu.VMEM((n_slc, 1, tq), F32),
            pltpu.VMEM((2, 2, NSA_GROUP, 1, tq), F32),
            pltpu.VMEM((2, 2, NSA_GROUP, V_ROWS, tq), F32)],
        compiler_params=_cparams("parallel", "parallel", "arbitrary"),
        name="nsa_attention_bounded" if bounded else "nsa_attention",
    )(qT, qT, glT, glT, kc_x, vcT_x, ovT, kx, kw, vT, masks, cmask, cmask, gq, shift)


def _mem_kv_kernel(mem_ref, g_ref, w_ref, kg_ref, k_ref, v_ref):
    mn = _rms(mem_ref[...], g_ref[...]).astype(BF16)
    kv = jnp.dot(mn, w_ref[...], preferred_element_type=F32)
    width = MEM_HEADS * MEM_HEAD_DIM
    for hh in range(MEM_HEADS):
        cols = slice(hh * MEM_HEAD_DIM, (hh + 1) * MEM_HEAD_DIM)
        k_ref[:, cols] = _rms(kv[:, cols], kg_ref[...]).astype(BF16)
    v_ref[...] = kv[:, width:].astype(BF16)


def _mem_kv(mem, g, w_mkv, k_g):
    batch, m_len, _ = mem.shape
    width = MEM_HEADS * MEM_HEAD_DIM
    spec = pl.BlockSpec((None, m_len, width), lambda b: (b, 0, 0))
    return pl.pallas_call(
        _mem_kv_kernel,
        out_shape=(jax.ShapeDtypeStruct((batch, m_len, width), BF16),) * 2,
        grid=(batch,),
        in_specs=[pl.BlockSpec((None, m_len, D_MODEL), lambda b: (b, 0, 0)),
                  _const_spec((1, D_MODEL)), _const_spec((D_MODEL, 2 * width)),
                  _const_spec((1, MEM_HEAD_DIM))],
        out_specs=(spec, spec),
        compiler_params=_cparams("parallel"),
        name="mem_kv",
    )(mem, g, w_mkv, k_g)


def _tail_kernel(a_ref, bTa_ref, bTb_ref, x_ref, gb_ref, wout_ref, gm_ref, wq_ref, qg_ref, k_ref, v_ref, wo_ref,
                 gf_ref, w1_ref, w2_ref, o_ref, *, per_b):
    in_first_half = (pl.program_id(0) % per_b) < per_b // 2
    bT = jnp.where(in_first_half, bTa_ref[...], bTb_ref[...])
    bn = (bT * lax.rsqrt(jnp.mean(bT * bT, axis=0, keepdims=True) + EPS) * gb_ref[...]).astype(BF16)
    x = x_ref[...] + jnp.dot(a_ref[...], wout_ref[:SG_WIDTH, :], preferred_element_type=F32)
    x = x + lax.dot_general(bn, wout_ref[SG_WIDTH:, :], TN_DIMS, preferred_element_type=F32)

    h = _rms(x, gm_ref[...]).astype(BF16)
    q = jnp.dot(h, wq_ref[...], preferred_element_type=F32)
    outs = []
    for hh in range(MEM_HEADS):
        cols = slice(hh * MEM_HEAD_DIM, (hh + 1) * MEM_HEAD_DIM)
        qh = (_rms(q[:, cols], qg_ref[...]) * MEM_HEAD_DIM ** -0.5).astype(BF16)
        s = lax.dot_general(qh, k_ref[:, cols], NT_DIMS, preferred_element_type=F32)
        e = jnp.exp(s - jnp.max(s, axis=-1, keepdims=True))
        p = e * (1.0 / jnp.sum(e, axis=-1, keepdims=True))
        outs.append(jnp.dot(p.astype(BF16), v_ref[:, cols], preferred_element_type=F32))
    o = jnp.concatenate(outs, axis=-1).astype(BF16)
    x = x + jnp.dot(o, wo_ref[...], preferred_element_type=F32)

    h = _rms(x, gf_ref[...]).astype(BF16)
    step = 1024
    for c0 in range(0, D_FF, step):
        a = jnp.maximum(jnp.dot(h, w1_ref[:, c0:c0 + step], preferred_element_type=F32), 0.0)
        x = x + jnp.dot((a * a).astype(BF16), w2_ref[c0:c0 + step, :], preferred_element_type=F32)
    o_ref[...] = x


def _tail(a_n, bT_halves, x2, g_b, w_out, g_mem, w_mq, q_g, mk, mv, w_mo, g_ffn, w1, w2, seq):
    n = x2.shape[0]
    tm = 512
    per_b = seq // tm
    half = per_b // 2
    m_len = mk.shape[1]
    width = MEM_HEADS * MEM_HEAD_DIM

    def weight(shape):
        return pl.BlockSpec(shape, lambda i: (0,) * len(shape), pipeline_mode=pl.Buffered(1))

    kv_spec = pl.BlockSpec((None, m_len, width), lambda i: (i // per_b, 0, 0))
    return pl.pallas_call(
        functools.partial(_tail_kernel, per_b=per_b),
        out_shape=jax.ShapeDtypeStruct((n, D_MODEL), F32),
        grid=(n // tm,),
        in_specs=[pl.BlockSpec((tm, SG_WIDTH), lambda i: (i, 0)),
                  pl.BlockSpec((None, NSA_WIDTH, tm), lambda i: (i // per_b, 0, jnp.minimum(i % per_b, half - 1))),
                  pl.BlockSpec((None, NSA_WIDTH, tm), lambda i: (i // per_b, 0, jnp.maximum(i % per_b - half, 0))),
                  pl.BlockSpec((tm, D_MODEL), lambda i: (i, 0)),
                  _const_spec((NSA_WIDTH, 1)), weight((D_MODEL, D_MODEL)),
                  _const_spec((1, D_MODEL)), weight((D_MODEL, width)), _const_spec((1, MEM_HEAD_DIM)),
                  kv_spec, kv_spec, weight((width, D_MODEL)),
                  _const_spec((1, D_MODEL)), weight((D_MODEL, D_FF)), weight((D_FF, D_MODEL))],
        out_specs=pl.BlockSpec((tm, D_MODEL), lambda i: (i, 0)),
        compiler_params=pltpu.CompilerParams(dimension_semantics=("parallel",), vmem_limit_bytes=TAIL_VMEM_LIMIT),
        name="tail",
    )(a_n, *bT_halves, x2, g_b, w_out, g_mem, w_mq, q_g, mk, mv, w_mo, g_ffn, w1, w2)


def _overlap_T(n_cmp_pad, n_cmp, n_slc):
    cs = jnp.arange(n_cmp_pad) * CMP_STRIDE
    ss = jnp.arange(n_slc) * SLC_BLOCK
    ov = jnp.clip(jnp.minimum(cs[None, :] + CMP_BLOCK, ss[:, None] + SLC_BLOCK)
                  - jnp.maximum(cs[None, :], ss[:, None]), 0, None).astype(F32) / CMP_BLOCK
    ov = jnp.where(jnp.arange(n_cmp_pad)[None, :] < n_cmp, ov, 0.0)
    return ov.astype(BF16)


def kernel(x, mem, norm_mix_g, w_in, sg_ln_g, sg_ln_b, sg_w, sg_b, q_norm_g, k_norm_g, cmp_pos, cmp_w1, cmp_b1, cmp_w2, cmp_b2, mix_out_g, w_out, norm_mem_g, mem_kv_norm_g, w_mq, w_mkv, mem_q_norm_g, mem_k_norm_g, w_mo, norm_ffn_g, w_ff1, w_ff2):
    batch, seq, _ = x.shape
    depth = w_in.shape[0]
    n = batch * seq
    tq = ATTN_TILE
    assert WINDOW == 2 * tq and seq % 512 == 0
    n_chunk = seq // CMP_STRIDE
    n_cmp = n_chunk - 1
    n_slc = seq // SLC_BLOCK
    assert n_slc <= HEAD_DIM
    ovT = _overlap_T(n_chunk, n_cmp, n_slc)
    row = lambda a: a.reshape(1, -1)
    r_idx = lax.broadcasted_iota(jnp.int32, (tq, tq), 0)
    c_idx = lax.broadcasted_iota(jnp.int32, (tq, tq), 1)
    masks = jnp.stack([jnp.zeros((tq, tq), F32), jnp.where(r_idx <= c_idx, 0.0, MASKED),
                       jnp.where(r_idx > c_idx, 0.0, MASKED), jnp.full((tq, tq), MASKED, F32)])
    blk_end = jnp.arange(n_chunk) * CMP_STRIDE + (CMP_BLOCK - 1)
    cmask = jnp.where((blk_end[:, None] <= jnp.arange(seq)[None, :]) & (jnp.arange(n_chunk)[:, None] < n_cmp),
                      0.0, MASKED).astype(F32).reshape(n_chunk, seq // tq, tq).transpose(1, 0, 2)

    x2 = x.reshape(n, D_MODEL)
    for l in range(depth):
        w = w_in[l].astype(BF16)
        w_gT = jnp.pad(w[:, COL_GATE:].T.reshape(NSA_KV_HEADS, GATES_PER_HEAD, D_MODEL),
                       ((0, 0), (0, GATE_ROWS - GATES_PER_HEAD), (0, 0))).reshape(NSA_KV_HEADS * GATE_ROWS, D_MODEL)
        stream = lambda s: w[:, COL_KV + s * KV_WIDTH:COL_KV + (s + 1) * KV_WIDTH]
        w_k = jnp.concatenate([stream(0), stream(1), stream(2), stream(4)], axis=1)
        w_vT = jnp.concatenate([stream(3), stream(5)], axis=1).T
        bs_full = jnp.repeat(sg_b[l].T, HEAD_DIM, axis=1)
        a_n, kx, kw, vT, k_cmp, v_cmp, qT, glT = _front(
            x2, row(norm_mix_g[l]), w[:, :COL_Q], w_k, w_vT, w[:, COL_Q:COL_KV].T, w_gT,
            row(sg_ln_g[l]), row(sg_ln_b[l]), sg_w[l], bs_full, row(mix_out_g[l, 0]),
            row(k_norm_g[l, 1]), row(k_norm_g[l, 2]), batch, seq)
        tiles = lambda a: a.reshape(batch, NSA_KV_HEADS, seq // tq, tq, a.shape[-1])
        chunks = lambda a: a.reshape(batch * NSA_KV_HEADS, n_chunk, CMP_STRIDE * HEAD_DIM)
        pos8 = lambda p: jnp.broadcast_to(p.reshape(1, -1), (8, CMP_BLOCK * HEAD_DIM))
        kc = _compress(chunks(k_cmp), pos8(cmp_pos[l, 0]), cmp_w1[l, 0], row(cmp_b1[l, 0]), cmp_w2[l, 0],
                       row(cmp_b2[l, 0]), row(k_norm_g[l, 0]), True)
        vc = _compress(chunks(v_cmp), pos8(cmp_pos[l, 1]), cmp_w1[l, 1], row(cmp_b1[l, 1]), cmp_w2[l, 1],
                       row(cmp_b2[l, 1]), row(k_norm_g[l, 0]), False)
        vcT = vc.transpose(0, 2, 1)

        gq = (q_norm_g[l] * (HEAD_DIM ** -0.5 * LOG2E)).reshape(HEAD_DIM, 1)
        kx, kw = tiles(kx), tiles(kw)
        peak = lambda g: jnp.max(jnp.abs(g))
        bound = HEAD_DIM ** 0.5 * LOG2E * peak(q_norm_g[l]) * peak(k_norm_g[l])

        kc_x = jnp.concatenate([kc, jnp.ones_like(kc[..., :1]), jnp.zeros_like(kc[..., 1:])], axis=-1).astype(BF16)
        vcT_x = jnp.concatenate([vcT, jnp.ones_like(vcT[:, :1]),
                                 jnp.zeros_like(vcT[:, :BF16_SUBLANES - 1])], axis=1).astype(BF16)
        ops = (qT, glT, kc_x, vcT_x, ovT, kx, kw, vT, masks, cmask, gq)
        oT = lax.cond(2.0 * bound <= MAX_SCORE_SPAN,
                      lambda ops, bound: _attention(*ops, (-bound).reshape(1), True),
                      lambda ops, bound: _attention(*ops, jnp.zeros((1,), F32), False),
                      ops, bound)

        mk, mv = _mem_kv(mem, row(mem_kv_norm_g[l]), w_mkv[l].astype(BF16), row(mem_k_norm_g[l]))
        x2 = _tail(a_n, oT, x2, mix_out_g[l, 1].reshape(NSA_WIDTH, 1), w_out[l].astype(BF16),
                   row(norm_mem_g[l]), w_mq[l].astype(BF16), row(mem_q_norm_g[l]), mk, mv, w_mo[l].astype(BF16),
                   row(norm_ffn_g[l]), w_ff1[l].astype(BF16), w_ff2[l].astype(BF16), seq)
    return x2.reshape(batch, seq, D_MODEL)
```

```python
import functools

import jax
import jax.numpy as jnp
from jax import lax
from jax.experimental import pallas as pl
from jax.experimental.pallas import tpu as pltpu

F32 = jnp.float32
BF16 = jnp.bfloat16

D_MODEL = 1024
HEAD_DIM = 64
SG_WIDTH = 512
SG_GROUPS = 8
SG_CHUNK = 128
NSA_WIDTH = 512
NSA_KV_HEADS = 2
NSA_GROUP = 4
KV_WIDTH = NSA_KV_HEADS * HEAD_DIM
N_BRANCH = 3
CMP_BLOCK = 32
CMP_STRIDE = 16
CMP_HIDDEN = 256
SLC_BLOCK = 64
SLC_TOPK = 16
WINDOW = 512
FORCE_SCORE = 1e4
MEM_HEADS = 4
MEM_HEAD_DIM = 128
D_FF = 4 * D_MODEL
EPS = 1e-6

COL_Q = 2 * SG_WIDTH
COL_KV = COL_Q + NSA_WIDTH
COL_GATE = COL_KV + 6 * KV_WIDTH
GATES_PER_HEAD = NSA_GROUP * N_BRANCH
GATE_ROWS = 16

MASKED = -1e30
LOG2E = 1.4426950408889634
MAX_SCORE_SPAN = 100.0
BF16_SUBLANES = 16
V_ROWS = HEAD_DIM + BF16_SUBLANES
ATTN_TILE = 256
VMEM_LIMIT = 48 * 1024 * 1024
TAIL_VMEM_LIMIT = 56 * 1024 * 1024

NT_DIMS = (((1,), (1,)), ((), ()))
TN_DIMS = (((0,), (0,)), ((), ()))


def _cparams(*sem):
    return pltpu.CompilerParams(dimension_semantics=sem, vmem_limit_bytes=VMEM_LIMIT)


def _rms(x, g):
    return x * lax.rsqrt(jnp.mean(x * x, axis=-1, keepdims=True) + EPS) * g


def _const_spec(shape):
    return pl.BlockSpec(shape, lambda *_: (0,) * len(shape))


def _dot_3pass(a, b):
    a_hi = a.astype(BF16)
    b_hi = b.astype(BF16)
    a_lo = (a - a_hi.astype(F32)).astype(BF16)
    b_lo = (b - b_hi.astype(F32)).astype(BF16)
    dot = functools.partial(jnp.dot, preferred_element_type=F32)
    return dot(a_hi, b_hi) + (dot(a_hi, b_lo) + dot(a_lo, b_hi))


def _dot_1pass(a, b):
    return jnp.dot(a.astype(BF16), b.astype(BF16), preferred_element_type=F32)


def _front_kernel(x_ref, g_ref, wuv_ref, wk_ref, wvT_ref, wqT_ref, wgT_ref,
                  lng_ref, lnb_ref, ws_ref, bs_ref, og_ref, gs_ref, gw_ref,
                  a_ref, kx_ref, kw_ref, vT_ref, kc_ref, vc_ref, qT_ref, gT_ref, h_sc, cmp_sc, *, tm, seq):
    h_sc[...] = _rms(x_ref[...], g_ref[...]).astype(BF16)
    tq = vT_ref.shape[-1]

    def queries():
        qT_ref[...] = lax.dot_general(wqT_ref[...], h_sc[...], NT_DIMS, preferred_element_type=F32)
        gT_ref[...] = lax.dot_general(wgT_ref[...], h_sc[...], NT_DIMS, preferred_element_type=F32)

    def values():
        vT = lax.dot_general(wvT_ref[...], h_sc[...], NT_DIMS, preferred_element_type=F32).astype(BF16)
        tail = jnp.where(lax.broadcasted_iota(jnp.int32, (BF16_SUBLANES, tq), 0) == 0, 1.0, 0.0).astype(BF16)
        for branch in range(2):
            for hd in range(NSA_KV_HEADS):
                r0 = (branch * NSA_KV_HEADS + hd) * HEAD_DIM
                for t in range(tm // tq):
                    vT_ref[hd, branch, t, :HEAD_DIM, :] = vT[r0:r0 + HEAD_DIM, t * tq:(t + 1) * tq]
                    vT_ref[hd, branch, t, HEAD_DIM:, :] = tail

    def keys(half):
        rows = slice(half * (tm // 2), (half + 1) * (tm // 2))
        kk = jnp.dot(h_sc[rows, :], wk_ref[...], preferred_element_type=F32)
        t0 = (pl.program_id(0) * tm) % seq + half * (tm // 2)
        blk = (t0 + lax.broadcasted_iota(jnp.int32, (tm // 2, SLC_BLOCK), 0)) // SLC_BLOCK
        onehot = jnp.where(blk == lax.broadcasted_iota(jnp.int32, (tm // 2, SLC_BLOCK), 1), 1.0, 0.0).astype(BF16)
        n_half = (tm // 2) // CMP_STRIDE
        chunk_rows = slice(half * n_half, (half + 1) * n_half)
        for out_ref, idx in ((kc_ref, 0), (vc_ref, 1)):
            cmp_sc[idx, rows, :] = kk[:, idx * KV_WIDTH:(idx + 1) * KV_WIDTH]
            for p in range(CMP_STRIDE):
                at_p = cmp_sc[idx, pl.ds(half * (tm // 2) + p, n_half, stride=CMP_STRIDE), :]
                for hd in range(NSA_KV_HEADS):
                    out_ref[hd, chunk_rows, p * HEAD_DIM:(p + 1) * HEAD_DIM] = at_p[:, hd * HEAD_DIM:(hd + 1) * HEAD_DIM]
        for hd in range(NSA_KV_HEADS):
            def part(idx):
                c0 = idx * KV_WIDTH + hd * HEAD_DIM
                return kk[:, c0:c0 + HEAD_DIM]
            kx_ref[hd, rows, :HEAD_DIM] = _rms(part(2), gs_ref[...]).astype(BF16)
            kx_ref[hd, rows, HEAD_DIM:] = onehot
            kw_ref[hd, rows, :HEAD_DIM] = _rms(part(3), gw_ref[...]).astype(BF16)
            kw_ref[hd, rows, HEAD_DIM:] = onehot

    row = lax.broadcasted_iota(jnp.int32, (SG_CHUNK, SG_CHUNK), 0)
    col = lax.broadcasted_iota(jnp.int32, (SG_CHUNK, SG_CHUNK), 1)
    causal = col <= row
    w = [jnp.where(causal, ws_ref[g], 0.0).astype(BF16) for g in range(SG_GROUPS)]
    between = [lambda: keys(0), lambda: keys(1), values, queries]
    for c in range(tm // SG_CHUNK):
        rows = slice(c * SG_CHUNK, (c + 1) * SG_CHUNK)
        if c % 2 == 0:
            uv_pair = jnp.dot(h_sc[c * SG_CHUNK:(c + 2) * SG_CHUNK, :], wuv_ref[...], preferred_element_type=F32)
        uv = uv_pair[(c % 2) * SG_CHUNK:(c % 2 + 1) * SG_CHUNK]
        between[c]()
        u = jax.nn.gelu(uv[:, :SG_WIDTH])
        v = jax.nn.gelu(uv[:, SG_WIDTH:])
        mu = jnp.mean(v, axis=-1, keepdims=True)
        vc = v - mu
        vn = vc * lax.rsqrt(jnp.mean(vc * vc, axis=-1, keepdims=True) + EPS) * lng_ref[...] + lnb_ref[...]
        vb = vn.astype(BF16)
        s = jnp.concatenate(
            [jnp.dot(w[g], vb[:, g * HEAD_DIM:(g + 1) * HEAD_DIM], preferred_element_type=F32)
             for g in range(SG_GROUPS)], axis=-1) + bs_ref[...]
        a_ref[rows, :] = _rms(u * s, og_ref[...]).astype(BF16)


def _front(x2, g, w_uv, w_k, w_vT, w_qT, w_gT, ln_g, ln_b, w_s, bs_full, out_g, g_slc, g_win, batch, seq):
    n = x2.shape[0]
    tm = 512
    tq = ATTN_TILE
    per_b = seq // tm
    hd = NSA_KV_HEADS
    g_rows = w_gT.shape[0]

    def per_head(width):
        return pl.BlockSpec((None, hd, tm, width), lambda i: (i // per_b, 0, i % per_b, 0))

    def per_head_shape(width, dtype):
        return jax.ShapeDtypeStruct((batch, hd, seq, width), dtype)

    def cols(height):
        return pl.BlockSpec((None, height, tm), lambda i: (i // per_b, 0, i % per_b))

    chunked_shape = jax.ShapeDtypeStruct((batch, hd, seq // CMP_STRIDE, CMP_STRIDE * HEAD_DIM), F32)
    chunked = pl.BlockSpec((None, hd, tm // CMP_STRIDE, CMP_STRIDE * HEAD_DIM),
                           lambda i: (i // per_b, 0, i % per_b, 0))

    return pl.pallas_call(
        functools.partial(_front_kernel, tm=tm, seq=seq),
        out_shape=(jax.ShapeDtypeStruct((n, SG_WIDTH), BF16),
                   per_head_shape(2 * HEAD_DIM, BF16), per_head_shape(2 * HEAD_DIM, BF16),
                   jax.ShapeDtypeStruct((batch, hd, 2, seq // tq, V_ROWS, tq), BF16),
                   chunked_shape, chunked_shape,
                   jax.ShapeDtypeStruct((batch, NSA_WIDTH, seq), F32),
                   jax.ShapeDtypeStruct((batch, g_rows, seq), F32)),
        grid=(n // tm,),
        in_specs=[pl.BlockSpec((tm, D_MODEL), lambda i: (i, 0)), _const_spec((1, D_MODEL)),
                  _const_spec((D_MODEL, 2 * SG_WIDTH)), _const_spec((D_MODEL, 4 * KV_WIDTH)),
                  _const_spec((2 * KV_WIDTH, D_MODEL)), _const_spec((NSA_WIDTH, D_MODEL)),
                  _const_spec((g_rows, D_MODEL)),
                  _const_spec((1, SG_WIDTH)), _const_spec((1, SG_WIDTH)),
                  _const_spec((SG_GROUPS, SG_CHUNK, SG_CHUNK)),
                  _const_spec((SG_CHUNK, SG_WIDTH)), _const_spec((1, SG_WIDTH)),
                  _const_spec((1, HEAD_DIM)), _const_spec((1, HEAD_DIM))],
        out_specs=(pl.BlockSpec((tm, SG_WIDTH), lambda i: (i, 0)),
                   per_head(2 * HEAD_DIM), per_head(2 * HEAD_DIM),
                   pl.BlockSpec((None, hd, 2, tm // tq, V_ROWS, tq), lambda i: (i // per_b, 0, 0, i % per_b, 0, 0)),
                   chunked, chunked, cols(NSA_WIDTH), cols(g_rows)),
        scratch_shapes=[pltpu.VMEM((tm, D_MODEL), BF16), pltpu.VMEM((2, tm, KV_WIDTH), F32)],
        compiler_params=_cparams("parallel"),
        name="front",
    )(x2, g, w_uv, w_k, w_vT, w_qT, w_gT, ln_g, ln_b, w_s, bs_full, out_g, g_slc, g_win)


def _cmp_kernel(c_ref, pos_ref, w1_ref, b1_ref, w2_ref, b2_ref, g_ref, o_ref, *, norm):
    dot = _dot_3pass if norm else _dot_1pass
    c = c_ref[...]
    nb, half = c.shape
    first = dot(c, w1_ref[:half, :])
    second = dot(c, w1_ref[half:, :])
    pos = dot(pos_ref[...], w1_ref[...])[0:1, :]
    h1 = jax.nn.gelu(first + pltpu.roll(second, nb - 1, 0) + pos + b1_ref[...])
    y = dot(h1, w2_ref[...]) + b2_ref[...]
    if norm:
        y = _rms(y, g_ref[...])
    o_ref[...] = y


def _compress(chunks, pos8, w1, b1, w2, b2, g, norm):
    bh, nb, half = chunks.shape
    return pl.pallas_call(
        functools.partial(_cmp_kernel, norm=norm),
        out_shape=jax.ShapeDtypeStruct((bh, nb, HEAD_DIM), F32),
        grid=(bh,),
        in_specs=[pl.BlockSpec((None, nb, half), lambda i: (i, 0, 0)),
                  _const_spec((8, 2 * half)), _const_spec((2 * half, CMP_HIDDEN)), _const_spec((1, CMP_HIDDEN)),
                  _const_spec((CMP_HIDDEN, HEAD_DIM)), _const_spec((1, HEAD_DIM)), _const_spec((1, HEAD_DIM))],
        out_specs=pl.BlockSpec((None, nb, HEAD_DIM), lambda i: (i, 0, 0)),
        compiler_params=_cparams("parallel"),
        name="compress_k" if norm else "compress_v",
    )(chunks, pos8, w1, b1, w2, b2, g)


def _stable_rank(score, score_sc, rank_sc, rows_sc, tq, interleave):
    interleave = list(interleave)
    n_slc = score.shape[0]
    for k in range(n_slc):
        rows_sc[k] = score[k:k + 1, :]
    score_sc[0:n_slc, :] = score
    sub, k_chunk = 8, 16
    per_gap = -(-len(interleave) // (n_slc // k_chunk))
    row_in_group = lax.broadcasted_iota(jnp.int32, (sub, tq), 0)
    for k0 in range(0, n_slc, k_chunk):
        for _ in range(min(per_gap, len(interleave))):
            interleave.pop(0)()
        rks = [jnp.broadcast_to(rows_sc[k], (sub, tq)) for k in range(k0, k0 + k_chunk)]
        for r0 in range(0, n_slc, sub):
            sg = score_sc[r0:r0 + sub, :]
            acc = rank_sc[r0:r0 + sub, :] if k0 else jnp.zeros((sub, tq), F32)
            for k in range(k0, k0 + k_chunk):
                rk = rks[k - k0]
                ge = jnp.where(rk >= sg, 1.0, 0.0)
                gt = jnp.where(rk > sg, 1.0, 0.0)
                if r0 > k:
                    before = ge
                elif r0 + sub - 1 < k:
                    before = gt
                else:
                    before = jnp.where(row_in_group > k - r0, ge, gt)
                acc = acc + before
            rank_sc[r0:r0 + sub, :] = acc
    for rest in interleave:
        rest()


def _attn_kernel(qTa_ref, qTb_ref, glTa_ref, glTb_ref, kc_ref, vcT_ref, ovT_ref, kx_ref, kw_ref, vT_ref, mask_ref,
                 cma_ref, cmb_ref, gq_ref, shift_ref, oa_ref, ob_ref,
                 qw_sc, qx_sc, s_a, mo_a, mn_a, s_b, mo_b, mn_b, s_c, score_sc, rank_sc, rows_sc, mrun_sc, acc_sc,
                 *, tq, nq, bounded):
    i = pl.program_id(2)
    d = HEAD_DIM
    G = NSA_GROUP
    shift = shift_ref[0]
    n_slc = ovT_ref.shape[0]
    ncp = kc_ref.shape[0]
    q_tile = [i, nq - 1 - i]
    q_refs, gl_refs, cm_refs, o_refs = (qTa_ref, qTb_ref), (glTa_ref, glTb_ref), (cma_ref, cmb_ref), (oa_ref, ob_ref)

    def lanes(g):
        return slice(g * tq, (g + 1) * tq)

    if not bounded:
        mrun_sc[...] = jnp.full(mrun_sc.shape, -jnp.inf, F32)
    acc_sc[...] = jnp.zeros(acc_sc.shape, F32)
    gates = [jax.nn.sigmoid(gl_refs[slot][...]) for slot in range(2)]

    def scores(slot, branch, k_ref, k_idx, q_sc, mask, s_buf, mo_buf, mn_buf):
        for g in range(G):
            s = jnp.dot(k_ref[k_idx], q_sc[slot, :, lanes(g)], preferred_element_type=F32)
            if mask is not None:
                s = s + mask_ref[mask]
            s_buf[g] = s
            if not bounded:
                m_old = mrun_sc[slot, branch, g]
                m_new = jnp.maximum(m_old, jnp.max(s, axis=0, keepdims=True))
                mo_buf[g] = m_old
                mn_buf[g] = m_new
                mrun_sc[slot, branch, g] = m_new

    def accumulate(slot, branch, v_idx, s_buf, mo_buf, mn_buf):
        for g in range(G):
            if bounded:
                p = jnp.exp2(s_buf[g]).astype(BF16)
                acc_sc[slot, branch, g] += jnp.dot(vT_ref[branch, v_idx], p, preferred_element_type=F32)
            else:
                m_new = mn_buf[g]
                alpha = jnp.exp2(mo_buf[g] - m_new)
                p = jnp.exp2(s_buf[g] - m_new).astype(BF16)
                acc_sc[slot, branch, g] = alpha * acc_sc[slot, branch, g] + jnp.dot(
                    vT_ref[branch, v_idx], p, preferred_element_type=F32)

    buf_a = (s_a, mo_a, mn_a)
    buf_b = (s_b, mo_b, mn_b)

    def window_stages(slot):
        t = q_tile[slot]
        tiles = [jnp.maximum(t - 2, 0), jnp.maximum(t - 1, 0), t]
        w_masks = [jnp.where(t >= 2, 2, 3), jnp.where(t >= 1, 0, 3), 1]

        def head():
            scores(slot, 1, kw_ref, tiles[0], qw_sc, w_masks[0], *buf_a)
            scores(slot, 1, kw_ref, tiles[1], qw_sc, w_masks[1], *buf_b)

        def mid():
            accumulate(slot, 1, tiles[0], *buf_a)
            scores(slot, 1, kw_ref, tiles[2], qw_sc, w_masks[2], *buf_a)

        return [head, mid, lambda: accumulate(slot, 1, tiles[1], *buf_b),
                lambda: accumulate(slot, 1, tiles[2], *buf_a)]

    def queries(slot):
        for g in range(G):
            qg = q_refs[slot][g * d:(g + 1) * d, :]
            qg = qg * lax.rsqrt(jnp.mean(qg * qg, axis=0, keepdims=True) + EPS) * gq_ref[...]
            qw_sc[slot, 0:d, lanes(g)] = qg.astype(BF16)
        qw_sc[slot, d:2 * d, :] = jnp.full((d, G * tq), shift, F32).astype(BF16)

    def select(slot, before, inside):
        t = q_tile[slot]
        n_blk = n_slc if slot else n_slc // 2
        n_row = ncp if slot else ncp // 2
        for stage in before:
            stage()
        t_row = t * tq + lax.broadcasted_iota(jnp.int32, (1, tq), 1)
        sees_any = jnp.where(t_row >= CMP_BLOCK - 1, 1.0, 0.0)
        c_max = []
        for g in range(G):
            s = (jnp.dot(kc_ref[0:n_row, :], qw_sc[slot, :, lanes(g)], preferred_element_type=F32)
                 + cm_refs[slot][0:n_row, :])
            s_c[g, 0:n_row, :] = s
            if not bounded:
                c_max.append(jnp.max(s, axis=0, keepdims=True))
        oc = []
        for g in range(G):
            e = jnp.exp2(s_c[g, 0:n_row, :]) if bounded else jnp.exp2(s_c[g, 0:n_row, :] - c_max[g])
            s_c[g, 0:n_row, :] = e
            oc.append(jnp.dot(vcT_ref[:, 0:n_row], e.astype(BF16), preferred_element_type=F32))
        psum = jnp.zeros((n_row, tq), F32)
        for g in range(G):
            den = oc[g][d:d + 1, :]
            r = sees_any / jnp.where(den > 0, den, 1.0)
            psum = psum + s_c[g, 0:n_row, :] * r
            o_refs[slot][g * d:(g + 1) * d, :] = (gates[slot][g * 3:g * 3 + 1, :] * r) * oc[g][0:d, :]
        p_hi = psum.astype(BF16)
        p_lo = (psum - p_hi.astype(F32)).astype(BF16)
        imp = (jnp.dot(ovT_ref[0:n_blk, 0:n_row], p_hi, preferred_element_type=F32)
               + jnp.dot(ovT_ref[0:n_blk, 0:n_row], p_lo, preferred_element_type=F32))
        j_idx = lax.broadcasted_iota(jnp.int32, (n_blk, tq), 0)
        t_blk = (t * tq + lax.broadcasted_iota(jnp.int32, (n_blk, tq), 1)) // SLC_BLOCK
        forced = (j_idx == 0) | (j_idx == t_blk) | (j_idx == t_blk - 1)
        score = jnp.where(forced, FORCE_SCORE, jnp.where(j_idx <= t_blk, imp, -FORCE_SCORE))
        _stable_rank(score, score_sc, rank_sc, rows_sc, tq, inside)
        bias = jnp.where(rank_sc[0:n_blk, :] < min(SLC_TOPK, n_slc), shift, MASKED).astype(BF16)
        if n_blk < d:
            bias = jnp.concatenate([bias, jnp.full((d - n_blk, tq), MASKED, BF16)], axis=0)
        qx_sc[slot, 0:d, :] = qw_sc[slot, 0:d, :]
        for g in range(G):
            qx_sc[slot, d:2 * d, lanes(g)] = bias

    n_static = nq // 2

    def job(k):
        if k < n_static:
            return 1, k, None
        if k == nq:
            return 0, i, 1
        in_b = k <= q_tile[1]
        return (jnp.where(in_b, 1, 0), jnp.where(in_b, k, k - q_tile[1] - 1),
                jnp.where(k == q_tile[1], 1, 0))

    bufs = [buf_a, buf_b]

    def sweep_stage(k):
        def stage():
            if k + 1 <= nq:
                slot, tile, mask = job(k + 1)
                scores(slot, 0, kx_ref, tile, qx_sc, mask, *bufs[(k + 1) % 2])
            if k >= 0:
                slot, tile, _ = job(k)
                accumulate(slot, 0, tile, *bufs[k % 2])
        return stage

    queries(0)
    queries(1)
    win0, win1 = window_stages(0), window_stages(1)
    select(1, win1[:1], win1[1:] + win0[:1])
    static_sweep = [sweep_stage(k) for k in range(-1, n_static - 1)]
    select(0, win0[1:3], win0[3:] + static_sweep)
    for k in range(n_static - 1, nq + 1):
        sweep_stage(k)()

    for slot in range(2):
        for g in range(G):
            o = o_refs[slot][g * d:(g + 1) * d, :]
            for branch in range(2):
                acc = acc_sc[slot, branch, g]
                gate = gates[slot][g * 3 + branch + 1:g * 3 + branch + 2, :]
                o = o + (gate * (1.0 / acc[d:d + 1, :])) * acc[0:d, :]
            o_refs[slot][g * d:(g + 1) * d, :] = o


def _attention(qT, glT, kc_x, vcT_x, ovT, kx, kw, vT, masks, cmask, gq, shift, bounded):
    batch, _, seq = qT.shape
    tq = ATTN_TILE
    nq = seq // tq
    half = nq // 2
    hd = NSA_KV_HEADS
    gw = NSA_GROUP * HEAD_DIM
    ncp = kc_x.shape[1]
    n_slc = ovT.shape[0]

    def per_head(shape):
        return pl.BlockSpec((None, None) + shape, lambda b, h, i: (b, h) + (0,) * len(shape))

    def other(i):
        return nq - 1 - i

    stage = [pltpu.VMEM((NSA_GROUP, tq, tq), F32),
             pltpu.VMEM((NSA_GROUP, 1, tq), F32),
             pltpu.VMEM((NSA_GROUP, 1, tq), F32)]
    out_half = jax.ShapeDtypeStruct((batch, NSA_WIDTH, seq // 2), F32)
    return pl.pallas_call(
        functools.partial(_attn_kernel, tq=tq, nq=nq, bounded=bounded),
        out_shape=(out_half, out_half),
        grid=(batch, hd, half),
        in_specs=[pl.BlockSpec((None, gw, tq), lambda b, h, i: (b, h, i)),
                  pl.BlockSpec((None, gw, tq), lambda b, h, i: (b, h, other(i))),
                  pl.BlockSpec((None, GATE_ROWS, tq), lambda b, h, i: (b, h, i)),
                  pl.BlockSpec((None, GATE_ROWS, tq), lambda b, h, i: (b, h, other(i))),
                  pl.BlockSpec((None, ncp, 2 * HEAD_DIM), lambda b, h, i: (b * hd + h, 0, 0)),
                  pl.BlockSpec((None, V_ROWS, ncp), lambda b, h, i: (b * hd + h, 0, 0)),
                  _const_spec((n_slc, ncp)),
                  per_head((nq, tq, 2 * HEAD_DIM)), per_head((nq, tq, 2 * HEAD_DIM)),
                  per_head((2, nq, V_ROWS, tq)),
                  _const_spec((4, tq, tq)),
                  pl.BlockSpec((None, ncp, tq), lambda b, h, i: (i, 0, 0)),
                  pl.BlockSpec((None, ncp, tq), lambda b, h, i: (other(i), 0, 0)),
                  _const_spec((HEAD_DIM, 1)),
                  pl.BlockSpec(memory_space=pltpu.SMEM)],
        out_specs=(pl.BlockSpec((None, gw, tq), lambda b, h, i: (b, h, i)),
                   pl.BlockSpec((None, gw, tq), lambda b, h, i: (b, h, other(i) - half))),
        scratch_shapes=[pltpu.VMEM((2, 2 * HEAD_DIM, NSA_GROUP * tq), BF16),
                        pltpu.VMEM((2, 2 * HEAD_DIM, NSA_GROUP * tq), BF16)] + stage + stage + [
            pltpu.VMEM((NSA_GROUP, ncp, tq), F32),
            pltpu.VMEM((n_slc, tq), F32), pltpu.VMEM((n_slc, tq), F32), pltpu.VMEM((n_slc, 1, tq), F32),
            pltpu.VMEM((2, 2, NSA_GROUP, 1, tq), F32),
            pltpu.VMEM((2, 2, NSA_GROUP, V_ROWS, tq), F32)],
        compiler_params=_cparams("parallel", "parallel", "arbitrary"),
        name="nsa_attention_bounded" if bounded else "nsa_attention",
    )(qT, qT, glT, glT, kc_x, vcT_x, ovT, kx, kw, vT, masks, cmask, cmask, gq, shift)


def _mem_kv_kernel(mem_ref, g_ref, w_ref, kg_ref, k_ref, v_ref):
    mn = _rms(mem_ref[...], g_ref[...]).astype(BF16)
    kv = jnp.dot(mn, w_ref[...], preferred_element_type=F32)
    width = MEM_HEADS * MEM_HEAD_DIM
    for hh in range(MEM_HEADS):
        cols = slice(hh * MEM_HEAD_DIM, (hh + 1) * MEM_HEAD_DIM)
        k_ref[:, cols] = _rms(kv[:, cols], kg_ref[...]).astype(BF16)
    v_ref[...] = kv[:, width:].astype(BF16)


def _mem_kv(mem, g, w_mkv, k_g):
    batch, m_len, _ = mem.shape
    width = MEM_HEADS * MEM_HEAD_DIM
    spec = pl.BlockSpec((None, m_len, width), lambda b: (b, 0, 0))
    return pl.pallas_call(
        _mem_kv_kernel,
        out_shape=(jax.ShapeDtypeStruct((batch, m_len, width), BF16),) * 2,
        grid=(batch,),
        in_specs=[pl.BlockSpec((None, m_len, D_MODEL), lambda b: (b, 0, 0)),
                  _const_spec((1, D_MODEL)), _const_spec((D_MODEL, 2 * width)),
                  _const_spec((1, MEM_HEAD_DIM))],
        out_specs=(spec, spec),
        compiler_params=_cparams("parallel"),
        name="mem_kv",
    )(mem, g, w_mkv, k_g)


def _tail_kernel(a_ref, bTa_ref, bTb_ref, x_ref, gb_ref, wout_ref, gm_ref, wq_ref, qg_ref, k_ref, v_ref, wo_ref,
                 gf_ref, w1_ref, w2_ref, o_ref, *, per_b):
    in_first_half = (pl.program_id(0) % per_b) < per_b // 2
    bT = jnp.where(in_first_half, bTa_ref[...], bTb_ref[...])
    bn = (bT * lax.rsqrt(jnp.mean(bT * bT, axis=0, keepdims=True) + EPS) * gb_ref[...]).astype(BF16)
    x = x_ref[...] + jnp.dot(a_ref[...], wout_ref[:SG_WIDTH, :], preferred_element_type=F32)
    x = x + lax.dot_general(bn, wout_ref[SG_WIDTH:, :], TN_DIMS, preferred_element_type=F32)

    h = _rms(x, gm_ref[...]).astype(BF16)
    q = jnp.dot(h, wq_ref[...], preferred_element_type=F32)
    outs = []
    for hh in range(MEM_HEADS):
        cols = slice(hh * MEM_HEAD_DIM, (hh + 1) * MEM_HEAD_DIM)
        qh = (_rms(q[:, cols], qg_ref[...]) * MEM_HEAD_DIM ** -0.5).astype(BF16)
        s = lax.dot_general(qh, k_ref[:, cols], NT_DIMS, preferred_element_type=F32)
        e = jnp.exp(s - jnp.max(s, axis=-1, keepdims=True))
        p = e * (1.0 / jnp.sum(e, axis=-1, keepdims=True))
        outs.append(jnp.dot(p.astype(BF16), v_ref[:, cols], preferred_element_type=F32))
    o = jnp.concatenate(outs, axis=-1).astype(BF16)
    x = x + jnp.dot(o, wo_ref[...], preferred_element_type=F32)

    h = _rms(x, gf_ref[...]).astype(BF16)
    step = 1024
    for c0 in range(0, D_FF, step):
        a = jnp.maximum(jnp.dot(h, w1_ref[:, c0:c0 + step], preferred_element_type=F32), 0.0)
        x = x + jnp.dot((a * a).astype(BF16), w2_ref[c0:c0 + step, :], preferred_element_type=F32)
    o_ref[...] = x


def _tail(a_n, bT_halves, x2, g_b, w_out, g_mem, w_mq, q_g, mk, mv, w_mo, g_ffn, w1, w2, seq):
    n = x2.shape[0]
    tm = 512
    per_b = seq // tm
    half = per_b // 2
    m_len = mk.shape[1]
    width = MEM_HEADS * MEM_HEAD_DIM

    def weight(shape):
        return pl.BlockSpec(shape, lambda i: (0,) * len(shape), pipeline_mode=pl.Buffered(1))

    kv_spec = pl.BlockSpec((None, m_len, width), lambda i: (i // per_b, 0, 0))
    return pl.pallas_call(
        functools.partial(_tail_kernel, per_b=per_b),
        out_shape=jax.ShapeDtypeStruct((n, D_MODEL), F32),
        grid=(n // tm,),
        in_specs=[pl.BlockSpec((tm, SG_WIDTH), lambda i: (i, 0)),
                  pl.BlockSpec((None, NSA_WIDTH, tm), lambda i: (i // per_b, 0, jnp.minimum(i % per_b, half - 1))),
                  pl.BlockSpec((None, NSA_WIDTH, tm), lambda i: (i // per_b, 0, jnp.maximum(i % per_b - half, 0))),
                  pl.BlockSpec((tm, D_MODEL), lambda i: (i, 0)),
                  _const_spec((NSA_WIDTH, 1)), weight((D_MODEL, D_MODEL)),
                  _const_spec((1, D_MODEL)), weight((D_MODEL, width)), _const_spec((1, MEM_HEAD_DIM)),
                  kv_spec, kv_spec, weight((width, D_MODEL)),
                  _const_spec((1, D_MODEL)), weight((D_MODEL, D_FF)), weight((D_FF, D_MODEL))],
        out_specs=pl.BlockSpec((tm, D_MODEL), lambda i: (i, 0)),
        compiler_params=pltpu.CompilerParams(dimension_semantics=("parallel",), vmem_limit_bytes=TAIL_VMEM_LIMIT),
        name="tail",
    )(a_n, *bT_halves, x2, g_b, w_out, g_mem, w_mq, q_g, mk, mv, w_mo, g_ffn, w1, w2)


def _overlap_T(n_cmp_pad, n_cmp, n_slc):
    cs = jnp.arange(n_cmp_pad) * CMP_STRIDE
    ss = jnp.arange(n_slc) * SLC_BLOCK
    ov = jnp.clip(jnp.minimum(cs[None, :] + CMP_BLOCK, ss[:, None] + SLC_BLOCK)
                  - jnp.maximum(cs[None, :], ss[:, None]), 0, None).astype(F32) / CMP_BLOCK
    ov = jnp.where(jnp.arange(n_cmp_pad)[None, :] < n_cmp, ov, 0.0)
    return ov.astype(BF16)


def kernel(x, mem, norm_mix_g, w_in, sg_ln_g, sg_ln_b, sg_w, sg_b, q_norm_g, k_norm_g, cmp_pos, cmp_w1, cmp_b1, cmp_w2, cmp_b2, mix_out_g, w_out, norm_mem_g, mem_kv_norm_g, w_mq, w_mkv, mem_q_norm_g, mem_k_norm_g, w_mo, norm_ffn_g, w_ff1, w_ff2):
    batch, seq, _ = x.shape
    depth = w_in.shape[0]
    n = batch * seq
    tq = ATTN_TILE
    assert WINDOW == 2 * tq and seq % 512 == 0
    n_chunk = seq // CMP_STRIDE
    n_cmp = n_chunk - 1
    n_slc = seq // SLC_BLOCK
    assert n_slc <= HEAD_DIM
    ovT = _overlap_T(n_chunk, n_cmp, n_slc)
    row = lambda a: a.reshape(1, -1)
    r_idx = lax.broadcasted_iota(jnp.int32, (tq, tq), 0)
    c_idx = lax.broadcasted_iota(jnp.int32, (tq, tq), 1)
    masks = jnp.stack([jnp.zeros((tq, tq), F32), jnp.where(r_idx <= c_idx, 0.0, MASKED),
                       jnp.where(r_idx > c_idx, 0.0, MASKED), jnp.full((tq, tq), MASKED, F32)])
    blk_end = jnp.arange(n_chunk) * CMP_STRIDE + (CMP_BLOCK - 1)
    cmask = jnp.where((blk_end[:, None] <= jnp.arange(seq)[None, :]) & (jnp.arange(n_chunk)[:, None] < n_cmp),
                      0.0, MASKED).astype(F32).reshape(n_chunk, seq // tq, tq).transpose(1, 0, 2)

    x2 = x.reshape(n, D_MODEL)
    for l in range(depth):
        w = w_in[l].astype(BF16)
        w_gT = jnp.pad(w[:, COL_GATE:].T.reshape(NSA_KV_HEADS, GATES_PER_HEAD, D_MODEL),
                       ((0, 0), (0, GATE_ROWS - GATES_PER_HEAD), (0, 0))).reshape(NSA_KV_HEADS * GATE_ROWS, D_MODEL)
        stream = lambda s: w[:, COL_KV + s * KV_WIDTH:COL_KV + (s + 1) * KV_WIDTH]
        w_k = jnp.concatenate([stream(0), stream(1), stream(2), stream(4)], axis=1)
        w_vT = jnp.concatenate([stream(3), stream(5)], axis=1).T
        bs_full = jnp.repeat(sg_b[l].T, HEAD_DIM, axis=1)
        a_n, kx, kw, vT, k_cmp, v_cmp, qT, glT = _front(
            x2, row(norm_mix_g[l]), w[:, :COL_Q], w_k, w_vT, w[:, COL_Q:COL_KV].T, w_gT,
            row(sg_ln_g[l]), row(sg_ln_b[l]), sg_w[l], bs_full, row(mix_out_g[l, 0]),
            row(k_norm_g[l, 1]), row(k_norm_g[l, 2]), batch, seq)
        tiles = lambda a: a.reshape(batch, NSA_KV_HEADS, seq // tq, tq, a.shape[-1])
        chunks = lambda a: a.reshape(batch * NSA_KV_HEADS, n_chunk, CMP_STRIDE * HEAD_DIM)
        pos8 = lambda p: jnp.broadcast_to(p.reshape(1, -1), (8, CMP_BLOCK * HEAD_DIM))
        kc = _compress(chunks(k_cmp), pos8(cmp_pos[l, 0]), cmp_w1[l, 0], row(cmp_b1[l, 0]), cmp_w2[l, 0],
                       row(cmp_b2[l, 0]), row(k_norm_g[l, 0]), True)
        vc = _compress(chunks(v_cmp), pos8(cmp_pos[l, 1]), cmp_w1[l, 1], row(cmp_b1[l, 1]), cmp_w2[l, 1],
                       row(cmp_b2[l, 1]), row(k_norm_g[l, 0]), False)
        vcT = vc.transpose(0, 2, 1)

        gq = (q_norm_g[l] * (HEAD_DIM ** -0.5 * LOG2E)).reshape(HEAD_DIM, 1)
        kx, kw = tiles(kx), tiles(kw)
        peak = lambda g: jnp.max(jnp.abs(g))
        bound = HEAD_DIM ** 0.5 * LOG2E * peak(q_norm_g[l]) * peak(k_norm_g[l])

        kc_x = jnp.concatenate([kc, jnp.ones_like(kc[..., :1]), jnp.zeros_like(kc[..., 1:])], axis=-1).astype(BF16)
        vcT_x = jnp.concatenate([vcT, jnp.ones_like(vcT[:, :1]),
                                 jnp.zeros_like(vcT[:, :BF16_SUBLANES - 1])], axis=1).astype(BF16)
        ops = (qT, glT, kc_x, vcT_x, ovT, kx, kw, vT, masks, cmask, gq)
        oT = lax.cond(2.0 * bound <= MAX_SCORE_SPAN,
                      lambda ops, bound: _attention(*ops, (-bound).reshape(1), True),
                      lambda ops, bound: _attention(*ops, jnp.zeros((1,), F32), False),
                      ops, bound)

        mk, mv = _mem_kv(mem, row(mem_kv_norm_g[l]), w_mkv[l].astype(BF16), row(mem_k_norm_g[l]))
        x2 = _tail(a_n, oT, x2, mix_out_g[l, 1].reshape(NSA_WIDTH, 1), w_out[l].astype(BF16),
                   row(norm_mem_g[l]), w_mq[l].astype(BF16), row(mem_q_norm_g[l]), mk, mv, w_mo[l].astype(BF16),
                   row(norm_ffn_g[l]), w_ff1[l].astype(BF16), w_ff2[l].astype(BF16), seq)
    return x2.reshape(batch, seq, D_MODEL)
```

```python
import functools

import jax
import jax.numpy as jnp
from jax import lax
from jax.experimental import pallas as pl
from jax.experimental.pallas import tpu as pltpu

F32 = jnp.float32
BF16 = jnp.bfloat16

D_MODEL = 1024
HEAD_DIM = 64
SG_WIDTH = 512
SG_GROUPS = 8
SG_CHUNK = 128
NSA_WIDTH = 512
NSA_KV_HEADS = 2
NSA_GROUP = 4
KV_WIDTH = NSA_KV_HEADS * HEAD_DIM
N_BRANCH = 3
CMP_BLOCK = 32
CMP_STRIDE = 16
CMP_HIDDEN = 256
SLC_BLOCK = 64
SLC_TOPK = 16
WINDOW = 512
FORCE_SCORE = 1e4
MEM_HEADS = 4
MEM_HEAD_DIM = 128
D_FF = 4 * D_MODEL
EPS = 1e-6

COL_Q = 2 * SG_WIDTH
COL_KV = COL_Q + NSA_WIDTH
COL_GATE = COL_KV + 6 * KV_WIDTH
GATES_PER_HEAD = NSA_GROUP * N_BRANCH
GATE_ROWS = 16

MASKED = -1e30
LOG2E = 1.4426950408889634
MAX_SCORE_SPAN = 100.0
BF16_SUBLANES = 16
V_ROWS = HEAD_DIM + BF16_SUBLANES
ATTN_TILE = 256
VMEM_LIMIT = 48 * 1024 * 1024
TAIL_VMEM_LIMIT = 56 * 1024 * 1024

NT_DIMS = (((1,), (1,)), ((), ()))
TN_DIMS = (((0,), (0,)), ((), ()))


def _cparams(*sem):
    return pltpu.CompilerParams(dimension_semantics=sem, vmem_limit_bytes=VMEM_LIMIT)


def _rms(x, g):
    return x * lax.rsqrt(jnp.mean(x * x, axis=-1, keepdims=True) + EPS) * g


def _const_spec(shape):
    return pl.BlockSpec(shape, lambda *_: (0,) * len(shape))


def _dot_3pass(a, b):
    a_hi = a.astype(BF16)
    b_hi = b.astype(BF16)
    a_lo = (a - a_hi.astype(F32)).astype(BF16)
    b_lo = (b - b_hi.astype(F32)).astype(BF16)
    dot = functools.partial(jnp.dot, preferred_element_type=F32)
    return dot(a_hi, b_hi) + (dot(a_hi, b_lo) + dot(a_lo, b_hi))


def _dot_1pass(a, b):
    return jnp.dot(a.astype(BF16), b.astype(BF16), preferred_element_type=F32)


def _front_kernel(x_ref, g_ref, wuv_ref, wk_ref, wT_ref,
                  lng_ref, lnb_ref, ws_ref, bs_ref, og_ref, gs_ref, gw_ref,
                  a_ref, kx_ref, kw_ref, vT_ref, kc_ref, vc_ref, qT_ref, gT_ref, h_sc, cmp_sc, *, tm, seq):
    h_sc[...] = _rms(x_ref[...], g_ref[...]).astype(BF16)
    tq = vT_ref.shape[-1]

    n_vg = wT_ref.shape[0] - NSA_WIDTH

    def queries():
        qT_ref[...] = lax.dot_general(wT_ref[n_vg:, :], h_sc[...], NT_DIMS, preferred_element_type=F32)

    def values():
        vgT = lax.dot_general(wT_ref[:n_vg, :], h_sc[...], NT_DIMS, preferred_element_type=F32)
        gT_ref[...] = vgT[2 * KV_WIDTH:]
        vT = vgT[:2 * KV_WIDTH].astype(BF16)
        tail = jnp.where(lax.broadcasted_iota(jnp.int32, (BF16_SUBLANES, tq), 0) == 0, 1.0, 0.0).astype(BF16)
        for branch in range(2):
            for hd in range(NSA_KV_HEADS):
                r0 = (branch * NSA_KV_HEADS + hd) * HEAD_DIM
                for t in range(tm // tq):
                    vT_ref[hd, branch, t, :HEAD_DIM, :] = vT[r0:r0 + HEAD_DIM, t * tq:(t + 1) * tq]
                    vT_ref[hd, branch, t, HEAD_DIM:, :] = tail

    kk_tile = []

    def keys(half):
        rows = slice(half * (tm // 2), (half + 1) * (tm // 2))
        if half == 0:
            kk_tile.append(jnp.dot(h_sc[...], wk_ref[...], preferred_element_type=F32))
        kk = kk_tile[0][rows]
        t0 = (pl.program_id(0) * tm) % seq + half * (tm // 2)
        blk = (t0 + lax.broadcasted_iota(jnp.int32, (tm // 2, SLC_BLOCK), 0)) // SLC_BLOCK
        onehot = jnp.where(blk == lax.broadcasted_iota(jnp.int32, (tm // 2, SLC_BLOCK), 1), 1.0, 0.0).astype(BF16)
        n_half = (tm // 2) // CMP_STRIDE
        chunk_rows = slice(half * n_half, (half + 1) * n_half)
        for out_ref, idx in ((kc_ref, 0), (vc_ref, 1)):
            cmp_sc[idx, rows, :] = kk[:, idx * KV_WIDTH:(idx + 1) * KV_WIDTH]
            for p in range(CMP_STRIDE):
                at_p = cmp_sc[idx, pl.ds(half * (tm // 2) + p, n_half, stride=CMP_STRIDE), :]
                for hd in range(NSA_KV_HEADS):
                    out_ref[hd, chunk_rows, p * HEAD_DIM:(p + 1) * HEAD_DIM] = at_p[:, hd * HEAD_DIM:(hd + 1) * HEAD_DIM]
        for hd in range(NSA_KV_HEADS):
            def part(idx):
                c0 = idx * KV_WIDTH + hd * HEAD_DIM
                return kk[:, c0:c0 + HEAD_DIM]
            kx_ref[hd, rows, :HEAD_DIM] = _rms(part(2), gs_ref[...]).astype(BF16)
            kx_ref[hd, rows, HEAD_DIM:] = onehot
            kw_ref[hd, rows, :HEAD_DIM] = _rms(part(3), gw_ref[...]).astype(BF16)
            kw_ref[hd, rows, HEAD_DIM:] = onehot

    row = lax.broadcasted_iota(jnp.int32, (SG_CHUNK, SG_CHUNK), 0)
    col = lax.broadcasted_iota(jnp.int32, (SG_CHUNK, SG_CHUNK), 1)
    causal = col <= row
    w = [jnp.where(causal, ws_ref[g], 0.0).astype(BF16) for g in range(SG_GROUPS)]
    between = [lambda: keys(0), lambda: keys(1), values, queries]
    for c in range(tm // SG_CHUNK):
        rows = slice(c * SG_CHUNK, (c + 1) * SG_CHUNK)
        if c % 2 == 0:
            uv_pair = jnp.dot(h_sc[c * SG_CHUNK:(c + 2) * SG_CHUNK, :], wuv_ref[...], preferred_element_type=F32)
        uv = uv_pair[(c % 2) * SG_CHUNK:(c % 2 + 1) * SG_CHUNK]
        between[c]()
        u = jax.nn.gelu(uv[:, :SG_WIDTH])
        v = jax.nn.gelu(uv[:, SG_WIDTH:])
        mu = jnp.mean(v, axis=-1, keepdims=True)
        vc = v - mu
        vn = vc * lax.rsqrt(jnp.mean(vc * vc, axis=-1, keepdims=True) + EPS) * lng_ref[...] + lnb_ref[...]
        vb = vn.astype(BF16)
        s = jnp.concatenate(
            [jnp.dot(w[g], vb[:, g * HEAD_DIM:(g + 1) * HEAD_DIM], preferred_element_type=F32)
             for g in range(SG_GROUPS)], axis=-1) + bs_ref[...]
        a_ref[rows, :] = _rms(u * s, og_ref[...]).astype(BF16)


def _front(x2, g, w_uv, w_k, w_T, ln_g, ln_b, w_s, bs_full, out_g, g_slc, g_win, batch, seq):
    n = x2.shape[0]
    tq = ATTN_TILE
    tm = 2 * tq
    per_b = seq // tm
    hd = NSA_KV_HEADS
    g_rows = w_T.shape[0] - 2 * KV_WIDTH - NSA_WIDTH

    def per_head(width):
        return pl.BlockSpec((None, hd, tm, width), lambda i: (i // per_b, 0, i % per_b, 0))

    def per_head_shape(width, dtype):
        return jax.ShapeDtypeStruct((batch, hd, seq, width), dtype)

    def cols(height):
        return pl.BlockSpec((None, height, tm), lambda i: (i // per_b, 0, i % per_b))

    chunked_shape = jax.ShapeDtypeStruct((batch, hd, seq // CMP_STRIDE, CMP_STRIDE * HEAD_DIM), F32)
    chunked = pl.BlockSpec((None, hd, tm // CMP_STRIDE, CMP_STRIDE * HEAD_DIM),
                           lambda i: (i // per_b, 0, i % per_b, 0))

    return pl.pallas_call(
        functools.partial(_front_kernel, tm=tm, seq=seq),
        out_shape=(jax.ShapeDtypeStruct((n, SG_WIDTH), BF16),
                   per_head_shape(2 * HEAD_DIM, BF16), per_head_shape(2 * HEAD_DIM, BF16),
                   jax.ShapeDtypeStruct((batch, hd, 2, seq // tq, V_ROWS, tq), BF16),
                   chunked_shape, chunked_shape,
                   jax.ShapeDtypeStruct((batch, NSA_WIDTH, seq), F32),
                   jax.ShapeDtypeStruct((batch, g_rows, seq), F32)),
        grid=(n // tm,),
        in_specs=[pl.BlockSpec((tm, D_MODEL), lambda i: (i, 0)), _const_spec((1, D_MODEL)),
                  _const_spec((D_MODEL, 2 * SG_WIDTH)), _const_spec((D_MODEL, 4 * KV_WIDTH)),
                  _const_spec(w_T.shape),
                  _const_spec((1, SG_WIDTH)), _const_spec((1, SG_WIDTH)),
                  _const_spec((SG_GROUPS, SG_CHUNK, SG_CHUNK)),
                  _const_spec((SG_CHUNK, SG_WIDTH)), _const_spec((1, SG_WIDTH)),
                  _const_spec((1, HEAD_DIM)), _const_spec((1, HEAD_DIM))],
        out_specs=(pl.BlockSpec((tm, SG_WIDTH), lambda i: (i, 0)),
                   per_head(2 * HEAD_DIM), per_head(2 * HEAD_DIM),
                   pl.BlockSpec((None, hd, 2, tm // tq, V_ROWS, tq), lambda i: (i // per_b, 0, 0, i % per_b, 0, 0)),
                   chunked, chunked, cols(NSA_WIDTH), cols(g_rows)),
        scratch_shapes=[pltpu.VMEM((tm, D_MODEL), BF16), pltpu.VMEM((2, tm, KV_WIDTH), F32)],
        compiler_params=_cparams("parallel"),
        name="front",
    )(x2, g, w_uv, w_k, w_T, ln_g, ln_b, w_s, bs_full, out_g, g_slc, g_win)


def _cmp_kernel(c_ref, pos_ref, w1_ref, b1_ref, w2_ref, b2_ref, g_ref, o_ref, *, norm):
    dot = _dot_3pass if norm else _dot_1pass
    c = c_ref[...]
    nb, half = c.shape
    first = dot(c, w1_ref[:half, :])
    second = dot(c, w1_ref[half:, :])
    pos = dot(pos_ref[...], w1_ref[...])[0:1, :]
    h1 = jax.nn.gelu(first + pltpu.roll(second, nb - 1, 0) + pos + b1_ref[...])
    y = dot(h1, w2_ref[...]) + b2_ref[...]
    if norm:
        y = _rms(y, g_ref[...])
    o_ref[...] = y


def _compress(chunks, pos8, w1, b1, w2, b2, g, norm):
    bh, nb, half = chunks.shape
    return pl.pallas_call(
        functools.partial(_cmp_kernel, norm=norm),
        out_shape=jax.ShapeDtypeStruct((bh, nb, HEAD_DIM), F32),
        grid=(bh,),
        in_specs=[pl.BlockSpec((None, nb, half), lambda i: (i, 0, 0)),
                  _const_spec((8, 2 * half)), _const_spec((2 * half, CMP_HIDDEN)), _const_spec((1, CMP_HIDDEN)),
                  _const_spec((CMP_HIDDEN, HEAD_DIM)), _const_spec((1, HEAD_DIM)), _const_spec((1, HEAD_DIM))],
        out_specs=pl.BlockSpec((None, nb, HEAD_DIM), lambda i: (i, 0, 0)),
        compiler_params=_cparams("parallel"),
        name="compress_k" if norm else "compress_v",
    )(chunks, pos8, w1, b1, w2, b2, g)


def _stable_rank(score, score_sc, rank_sc, rows_sc, tq, interleave):
    interleave = list(interleave)
    n_slc = score.shape[0]
    for k in range(n_slc):
        rows_sc[k] = score[k:k + 1, :]
    score_sc[0:n_slc, :] = score
    sub, k_chunk = 8, 16
    per_gap = -(-len(interleave) // (n_slc // k_chunk))
    row_in_group = lax.broadcasted_iota(jnp.int32, (sub, tq), 0)
    for k0 in range(0, n_slc, k_chunk):
        for _ in range(min(per_gap, len(interleave))):
            interleave.pop(0)()
        rks = [jnp.broadcast_to(rows_sc[k], (sub, tq)) for k in range(k0, k0 + k_chunk)]
        for r0 in range(0, n_slc, sub):
            sg = score_sc[r0:r0 + sub, :]
            acc = rank_sc[r0:r0 + sub, :] if k0 else jnp.zeros((sub, tq), F32)
            for k in range(k0, k0 + k_chunk):
                rk = rks[k - k0]
                ge = jnp.where(rk >= sg, 1.0, 0.0)
                gt = jnp.where(rk > sg, 1.0, 0.0)
                if r0 > k:
                    before = ge
                elif r0 + sub - 1 < k:
                    before = gt
                else:
                    before = jnp.where(row_in_group > k - r0, ge, gt)
                acc = acc + before
            rank_sc[r0:r0 + sub, :] = acc
    for rest in interleave:
        rest()


def _attn_kernel(qTa_ref, qTb_ref, glTa_ref, glTb_ref, kc_ref, vcT_ref, ovT_ref, kx_ref, kw_ref, vT_ref, mask_ref,
                 cma_ref, cmb_ref, gq_ref, shift_ref, oa_ref, ob_ref,
                 qw_sc, qx_sc, s_a, mo_a, mn_a, s_b, mo_b, mn_b, s_c, score_sc, rank_sc, rows_sc, mrun_sc, acc_sc,
                 *, tq, nq, bounded):
    i = pl.program_id(2)
    d = HEAD_DIM
    G = NSA_GROUP
    shift = shift_ref[0]
    n_slc = ovT_ref.shape[0]
    ncp = kc_ref.shape[0]
    q_tile = [i, nq - 1 - i]
    q_refs, gl_refs, cm_refs, o_refs = (qTa_ref, qTb_ref), (glTa_ref, glTb_ref), (cma_ref, cmb_ref), (oa_ref, ob_ref)

    def lanes(g):
        return slice(g * tq, (g + 1) * tq)

    if not bounded:
        mrun_sc[...] = jnp.full(mrun_sc.shape, -jnp.inf, F32)
    acc_sc[...] = jnp.zeros(acc_sc.shape, F32)
    gates = [jax.nn.sigmoid(gl_refs[slot][...]) for slot in range(2)]

    def scores(slot, branch, k_ref, k_idx, q_sc, mask, s_buf, mo_buf, mn_buf):
        for g in range(G):
            s = jnp.dot(k_ref[k_idx], q_sc[slot, :, lanes(g)], preferred_element_type=F32)
            if mask is not None:
                s = s + mask_ref[mask]
            s_buf[g] = s
            if not bounded:
                m_old = mrun_sc[slot, branch, g]
                m_new = jnp.maximum(m_old, jnp.max(s, axis=0, keepdims=True))
                mo_buf[g] = m_old
                mn_buf[g] = m_new
                mrun_sc[slot, branch, g] = m_new

    def accumulate(slot, branch, v_idx, s_buf, mo_buf, mn_buf):
        for g in range(G):
            if bounded:
                p = jnp.exp2(s_buf[g]).astype(BF16)
                acc_sc[slot, branch, g] += jnp.dot(vT_ref[branch, v_idx], p, preferred_element_type=F32)
            else:
                m_new = mn_buf[g]
                alpha = jnp.exp2(mo_buf[g] - m_new)
                p = jnp.exp2(s_buf[g] - m_new).astype(BF16)
                acc_sc[slot, branch, g] = alpha * acc_sc[slot, branch, g] + jnp.dot(
                    vT_ref[branch, v_idx], p, preferred_element_type=F32)

    buf_a = (s_a, mo_a, mn_a)
    buf_b = (s_b, mo_b, mn_b)

    def window_stages(slot):
        t = q_tile[slot]
        tiles = [jnp.maximum(t - 2, 0), jnp.maximum(t - 1, 0), t]
        w_masks = [jnp.where(t >= 2, 2, 3), jnp.where(t >= 1, 0, 3), 1]

        def head():
            scores(slot, 1, kw_ref, tiles[0], qw_sc, w_masks[0], *buf_a)
            scores(slot, 1, kw_ref, tiles[1], qw_sc, w_masks[1], *buf_b)

        def mid():
            accumulate(slot, 1, tiles[0], *buf_a)
            scores(slot, 1, kw_ref, tiles[2], qw_sc, w_masks[2], *buf_a)

        return [head, mid, lambda: accumulate(slot, 1, tiles[1], *buf_b),
                lambda: accumulate(slot, 1, tiles[2], *buf_a)]

    def queries(slot):
        for g in range(G):
            qg = q_refs[slot][g * d:(g + 1) * d, :]
            qg = qg * lax.rsqrt(jnp.mean(qg * qg, axis=0, keepdims=True) + EPS) * gq_ref[...]
            qw_sc[slot, 0:d, lanes(g)] = qg.astype(BF16)
        qw_sc[slot, d:2 * d, :] = jnp.full((d, G * tq), shift, F32).astype(BF16)

    def select(slot, before, inside):
        t = q_tile[slot]
        n_blk = n_slc if slot else n_slc // 2
        n_row = ncp if slot else ncp // 2
        for stage in before:
            stage()
        t_row = t * tq + lax.broadcasted_iota(jnp.int32, (1, tq), 1)
        sees_any = jnp.where(t_row >= CMP_BLOCK - 1, 1.0, 0.0)
        c_max = []
        for g in range(G):
            s = (jnp.dot(kc_ref[0:n_row, :], qw_sc[slot, :, lanes(g)], preferred_element_type=F32)
                 + cm_refs[slot][0:n_row, :])
            s_c[g, 0:n_row, :] = s
            if not bounded:
                c_max.append(jnp.max(s, axis=0, keepdims=True))
        oc = []
        for g in range(G):
            e = jnp.exp2(s_c[g, 0:n_row, :]) if bounded else jnp.exp2(s_c[g, 0:n_row, :] - c_max[g])
            s_c[g, 0:n_row, :] = e
            oc.append(jnp.dot(vcT_ref[:, 0:n_row], e.astype(BF16), preferred_element_type=F32))
        psum = jnp.zeros((n_row, tq), F32)
        for g in range(G):
            den = oc[g][d:d + 1, :]
            r = sees_any / jnp.where(den > 0, den, 1.0)
            psum = psum + s_c[g, 0:n_row, :] * r
            o_refs[slot][g * d:(g + 1) * d, :] = (gates[slot][g * 3:g * 3 + 1, :] * r) * oc[g][0:d, :]
        p_hi = psum.astype(BF16)
        p_lo = (psum - p_hi.astype(F32)).astype(BF16)
        imp = (jnp.dot(ovT_ref[0:n_blk, 0:n_row], p_hi, preferred_element_type=F32)
               + jnp.dot(ovT_ref[0:n_blk, 0:n_row], p_lo, preferred_element_type=F32))
        j_idx = lax.broadcasted_iota(jnp.int32, (n_blk, tq), 0)
        t_blk = (t * tq + lax.broadcasted_iota(jnp.int32, (n_blk, tq), 1)) // SLC_BLOCK
        forced = (j_idx == 0) | (j_idx == t_blk) | (j_idx == t_blk - 1)
        score = jnp.where(forced, FORCE_SCORE, jnp.where(j_idx <= t_blk, imp, -FORCE_SCORE))
        _stable_rank(score, score_sc, rank_sc, rows_sc, tq, inside)
        bias = jnp.where(rank_sc[0:n_blk, :] < min(SLC_TOPK, n_slc), shift, MASKED).astype(BF16)
        if n_blk < d:
            bias = jnp.concatenate([bias, jnp.full((d - n_blk, tq), MASKED, BF16)], axis=0)
        qx_sc[slot, 0:d, :] = qw_sc[slot, 0:d, :]
        for g in range(G):
            qx_sc[slot, d:2 * d, lanes(g)] = bias

    n_static = nq // 2

    def job(k):
        if k < n_static:
            return 1, k, None
        if k == nq:
            return 0, i, 1
        in_b = k <= q_tile[1]
        return (jnp.where(in_b, 1, 0), jnp.where(in_b, k, k - q_tile[1] - 1),
                jnp.where(k == q_tile[1], 1, 0))

    bufs = [buf_a, buf_b]

    def sweep_stage(k):
        def stage():
            if k + 1 <= nq:
                slot, tile, mask = job(k + 1)
                scores(slot, 0, kx_ref, tile, qx_sc, mask, *bufs[(k + 1) % 2])
            if k >= 0:
                slot, tile, _ = job(k)
                accumulate(slot, 0, tile, *bufs[k % 2])
        return stage

    queries(0)
    queries(1)
    win0, win1 = window_stages(0), window_stages(1)
    select(1, win1[:1], win1[1:] + win0[:1])
    static_sweep = [sweep_stage(k) for k in range(-1, n_static - 1)]
    select(0, win0[1:3], win0[3:] + static_sweep)
    for k in range(n_static - 1, nq + 1):
        sweep_stage(k)()

    for slot in range(2):
        for g in range(G):
            o = o_refs[slot][g * d:(g + 1) * d, :]
            for branch in range(2):
                acc = acc_sc[slot, branch, g]
                gate = gates[slot][g * 3 + branch + 1:g * 3 + branch + 2, :]
                o = o + (gate * (1.0 / acc[d:d + 1, :])) * acc[0:d, :]
            o_refs[slot][g * d:(g + 1) * d, :] = o


def _attention(qT, glT, kc_x, vcT_x, ovT, kx, kw, vT, masks, cmask, gq, shift, bounded):
    batch, _, seq = qT.shape
    tq = ATTN_TILE
    nq = seq // tq
    half = nq // 2
    hd = NSA_KV_HEADS
    gw = NSA_GROUP * HEAD_DIM
    ncp = kc_x.shape[1]
    n_slc = ovT.shape[0]

    def per_head(shape):
        return pl.BlockSpec((None, None) + shape, lambda b, h, i: (b, h) + (0,) * len(shape))

    def other(i):
        return nq - 1 - i

    stage = [pltpu.VMEM((NSA_GROUP, tq, tq), F32),
             pltpu.VMEM((NSA_GROUP, 1, tq), F32),
             pltpu.VMEM((NSA_GROUP, 1, tq), F32)]
    out_half = jax.ShapeDtypeStruct((batch, NSA_WIDTH, seq // 2), F32)
    return pl.pallas_call(
        functools.partial(_attn_kernel, tq=tq, nq=nq, bounded=bounded),
        out_shape=(out_half, out_half),
        grid=(batch, hd, half),
        in_specs=[pl.BlockSpec((None, gw, tq), lambda b, h, i: (b, h, i)),
                  pl.BlockSpec((None, gw, tq), lambda b, h, i: (b, h, other(i))),
                  pl.BlockSpec((None, GATE_ROWS, tq), lambda b, h, i: (b, h, i)),
                  pl.BlockSpec((None, GATE_ROWS, tq), lambda b, h, i: (b, h, other(i))),
                  pl.BlockSpec((None, ncp, 2 * HEAD_DIM), lambda b, h, i: (b * hd + h, 0, 0)),
                  pl.BlockSpec((None, V_ROWS, ncp), lambda b, h, i: (b * hd + h, 0, 0)),
                  _const_spec((n_slc, ncp)),
                  per_head((nq, tq, 2 * HEAD_DIM)), per_head((nq, tq, 2 * HEAD_DIM)),
                  per_head((2, nq, V_ROWS, tq)),
                  _const_spec((4, tq, tq)),
                  pl.BlockSpec((None, ncp, tq), lambda b, h, i: (i, 0, 0)),
                  pl.BlockSpec((None, ncp, tq), lambda b, h, i: (other(i), 0, 0)),
                  _const_spec((HEAD_DIM, 1)),
                  pl.BlockSpec(memory_space=pltpu.SMEM)],
        out_specs=(pl.BlockSpec((None, gw, tq), lambda b, h, i: (b, h, i)),
                   pl.BlockSpec((None, gw, tq), lambda b, h, i: (b, h, other(i) - half))),
        scratch_shapes=[pltpu.VMEM((2, 2 * HEAD_DIM, NSA_GROUP * tq), BF16),
                        pltpu.VMEM((2, 2 * HEAD_DIM, NSA_GROUP * tq), BF16)] + stage + stage + [
            pltpu.VMEM((NSA_GROUP, ncp, tq), F32),
            pltpu.VMEM((n_slc, tq), F32), pltpu.VMEM((n_slc, tq), F32), pltpu.VMEM((n_slc, 1, tq), F32),
            pltpu.VMEM((2, 2, NSA_GROUP, 1, tq), F32),
            pltpu.VMEM((2, 2, NSA_GROUP, V_ROWS, tq), F32)],
        compiler_params=_cparams("parallel", "parallel", "arbitrary"),
        name="nsa_attention_bounded" if bounded else "nsa_attention",
    )(qT, qT, glT, glT, kc_x, vcT_x, ovT, kx, kw, vT, masks, cmask, cmask, gq, shift)


def _mem_kv_kernel(mem_ref, g_ref, w_ref, kg_ref, k_ref, v_ref):
    mn = _rms(mem_ref[...], g_ref[...]).astype(BF16)
    kv = jnp.dot(mn, w_ref[...], preferred_element_type=F32)
    width = MEM_HEADS * MEM_HEAD_DIM
    for hh in range(MEM_HEADS):
        cols = slice(hh * MEM_HEAD_DIM, (hh + 1) * MEM_HEAD_DIM)
        k_ref[:, cols] = _rms(kv[:, cols], kg_ref[...]).astype(BF16)
    v_ref[...] = kv[:, width:].astype(BF16)


def _mem_kv(mem, g, w_mkv, k_g):
    batch, m_len, _ = mem.shape
    width = MEM_HEADS * MEM_HEAD_DIM
    spec = pl.BlockSpec((None, m_len, width), lambda b: (b, 0, 0))
    return pl.pallas_call(
        _mem_kv_kernel,
        out_shape=(jax.ShapeDtypeStruct((batch, m_len, width), BF16),) * 2,
        grid=(batch,),
        in_specs=[pl.BlockSpec((None, m_len, D_MODEL), lambda b: (b, 0, 0)),
                  _const_spec((1, D_MODEL)), _const_spec((D_MODEL, 2 * width)),
                  _const_spec((1, MEM_HEAD_DIM))],
        out_specs=(spec, spec),
        compiler_params=_cparams("parallel"),
        name="mem_kv",
    )(mem, g, w_mkv, k_g)


def _tail_kernel(a_ref, bTa_ref, bTb_ref, x_ref, gb_ref, wout_ref, gm_ref, wq_ref, qg_ref, k_ref, v_ref, wo_ref,
                 gf_ref, w1_ref, w2_ref, o_ref, *, per_b):
    in_first_half = (pl.program_id(0) % per_b) < per_b // 2
    bT = jnp.where(in_first_half, bTa_ref[...], bTb_ref[...])
    bn = (bT * lax.rsqrt(jnp.mean(bT * bT, axis=0, keepdims=True) + EPS) * gb_ref[...]).astype(BF16)
    x = x_ref[...] + jnp.dot(a_ref[...], wout_ref[:SG_WIDTH, :], preferred_element_type=F32)
    x = x + lax.dot_general(bn, wout_ref[SG_WIDTH:, :], TN_DIMS, preferred_element_type=F32)

    h = _rms(x, gm_ref[...]).astype(BF16)
    q = jnp.dot(h, wq_ref[...], preferred_element_type=F32)
    outs = []
    for hh in range(MEM_HEADS):
        cols = slice(hh * MEM_HEAD_DIM, (hh + 1) * MEM_HEAD_DIM)
        qh = (_rms(q[:, cols], qg_ref[...]) * MEM_HEAD_DIM ** -0.5).astype(BF16)
        s = lax.dot_general(qh, k_ref[:, cols], NT_DIMS, preferred_element_type=F32)
        e = jnp.exp(s - jnp.max(s, axis=-1, keepdims=True))
        p = e * (1.0 / jnp.sum(e, axis=-1, keepdims=True))
        outs.append(jnp.dot(p.astype(BF16), v_ref[:, cols], preferred_element_type=F32))
    o = jnp.concatenate(outs, axis=-1).astype(BF16)
    x = x + jnp.dot(o, wo_ref[...], preferred_element_type=F32)

    h = _rms(x, gf_ref[...]).astype(BF16)
    step = 1024
    for c0 in range(0, D_FF, step):
        a = jnp.maximum(jnp.dot(h, w1_ref[:, c0:c0 + step], preferred_element_type=F32), 0.0)
        x = x + jnp.dot((a * a).astype(BF16), w2_ref[c0:c0 + step, :], preferred_element_type=F32)
    o_ref[...] = x


def _tail(a_n, bT_halves, x2, g_b, w_out, g_mem, w_mq, q_g, mk, mv, w_mo, g_ffn, w1, w2, seq):
    n = x2.shape[0]
    tm = 512
    per_b = seq // tm
    half = per_b // 2
    m_len = mk.shape[1]
    width = MEM_HEADS * MEM_HEAD_DIM

    def weight(shape):
        return pl.BlockSpec(shape, lambda i: (0,) * len(shape), pipeline_mode=pl.Buffered(1))

    kv_spec = pl.BlockSpec((None, m_len, width), lambda i: (i // per_b, 0, 0))
    return pl.pallas_call(
        functools.partial(_tail_kernel, per_b=per_b),
        out_shape=jax.ShapeDtypeStruct((n, D_MODEL), F32),
        grid=(n // tm,),
        in_specs=[pl.BlockSpec((tm, SG_WIDTH), lambda i: (i, 0)),
                  pl.BlockSpec((None, NSA_WIDTH, tm), lambda i: (i // per_b, 0, jnp.minimum(i % per_b, half - 1))),
                  pl.BlockSpec((None, NSA_WIDTH, tm), lambda i: (i // per_b, 0, jnp.maximum(i % per_b - half, 0))),
                  pl.BlockSpec((tm, D_MODEL), lambda i: (i, 0)),
                  _const_spec((NSA_WIDTH, 1)), weight((D_MODEL, D_MODEL)),
                  _const_spec((1, D_MODEL)), weight((D_MODEL, width)), _const_spec((1, MEM_HEAD_DIM)),
                  kv_spec, kv_spec, weight((width, D_MODEL)),
                  _const_spec((1, D_MODEL)), weight((D_MODEL, D_FF)), weight((D_FF, D_MODEL))],
        out_specs=pl.BlockSpec((tm, D_MODEL), lambda i: (i, 0)),
        compiler_params=pltpu.CompilerParams(dimension_semantics=("parallel",), vmem_limit_bytes=TAIL_VMEM_LIMIT),
        name="tail",
    )(a_n, *bT_halves, x2, g_b, w_out, g_mem, w_mq, q_g, mk, mv, w_mo, g_ffn, w1, w2)


def _overlap_T(n_cmp_pad, n_cmp, n_slc):
    cs = jnp.arange(n_cmp_pad) * CMP_STRIDE
    ss = jnp.arange(n_slc) * SLC_BLOCK
    ov = jnp.clip(jnp.minimum(cs[None, :] + CMP_BLOCK, ss[:, None] + SLC_BLOCK)
                  - jnp.maximum(cs[None, :], ss[:, None]), 0, None).astype(F32) / CMP_BLOCK
    ov = jnp.where(jnp.arange(n_cmp_pad)[None, :] < n_cmp, ov, 0.0)
    return ov.astype(BF16)


def kernel(x, mem, norm_mix_g, w_in, sg_ln_g, sg_ln_b, sg_w, sg_b, q_norm_g, k_norm_g, cmp_pos, cmp_w1, cmp_b1, cmp_w2, cmp_b2, mix_out_g, w_out, norm_mem_g, mem_kv_norm_g, w_mq, w_mkv, mem_q_norm_g, mem_k_norm_g, w_mo, norm_ffn_g, w_ff1, w_ff2):
    batch, seq, _ = x.shape
    depth = w_in.shape[0]
    n = batch * seq
    tq = ATTN_TILE
    assert WINDOW == 2 * tq and seq % 512 == 0
    n_chunk = seq // CMP_STRIDE
    n_cmp = n_chunk - 1
    n_slc = seq // SLC_BLOCK
    assert n_slc <= HEAD_DIM
    ovT = _overlap_T(n_chunk, n_cmp, n_slc)
    row = lambda a: a.reshape(1, -1)
    r_idx = lax.broadcasted_iota(jnp.int32, (tq, tq), 0)
    c_idx = lax.broadcasted_iota(jnp.int32, (tq, tq), 1)
    masks = jnp.stack([jnp.zeros((tq, tq), F32), jnp.where(r_idx <= c_idx, 0.0, MASKED),
                       jnp.where(r_idx > c_idx, 0.0, MASKED), jnp.full((tq, tq), MASKED, F32)])
    blk_end = jnp.arange(n_chunk) * CMP_STRIDE + (CMP_BLOCK - 1)
    cmask = jnp.where((blk_end[:, None] <= jnp.arange(seq)[None, :]) & (jnp.arange(n_chunk)[:, None] < n_cmp),
                      0.0, MASKED).astype(F32).reshape(n_chunk, seq // tq, tq).transpose(1, 0, 2)

    x2 = x.reshape(n, D_MODEL)
    for l in range(depth):
        w = w_in[l].astype(BF16)
        w_gT = jnp.pad(w[:, COL_GATE:].T.reshape(NSA_KV_HEADS, GATES_PER_HEAD, D_MODEL),
                       ((0, 0), (0, GATE_ROWS - GATES_PER_HEAD), (0, 0))).reshape(NSA_KV_HEADS * GATE_ROWS, D_MODEL)
        stream = lambda s: w[:, COL_KV + s * KV_WIDTH:COL_KV + (s + 1) * KV_WIDTH]
        w_k = jnp.concatenate([stream(0), stream(1), stream(2), stream(4)], axis=1)
        w_T = jnp.concatenate([stream(3).T, stream(5).T, w_gT, w[:, COL_Q:COL_KV].T], axis=0)
        bs_full = jnp.repeat(sg_b[l].T, HEAD_DIM, axis=1)
        a_n, kx, kw, vT, k_cmp, v_cmp, qT, glT = _front(
            x2, row(norm_mix_g[l]), w[:, :COL_Q], w_k, w_T,
            row(sg_ln_g[l]), row(sg_ln_b[l]), sg_w[l], bs_full, row(mix_out_g[l, 0]),
            row(k_norm_g[l, 1]), row(k_norm_g[l, 2]), batch, seq)
        tiles = lambda a: a.reshape(batch, NSA_KV_HEADS, seq // tq, tq, a.shape[-1])
        chunks = lambda a: a.reshape(batch * NSA_KV_HEADS, n_chunk, CMP_STRIDE * HEAD_DIM)
        pos8 = lambda p: jnp.broadcast_to(p.reshape(1, -1), (8, CMP_BLOCK * HEAD_DIM))
        kc = _compress(chunks(k_cmp), pos8(cmp_pos[l, 0]), cmp_w1[l, 0], row(cmp_b1[l, 0]), cmp_w2[l, 0],
                       row(cmp_b2[l, 0]), row(k_norm_g[l, 0]), True)
        vc = _compress(chunks(v_cmp), pos8(cmp_pos[l, 1]), cmp_w1[l, 1], row(cmp_b1[l, 1]), cmp_w2[l, 1],
                       row(cmp_b2[l, 1]), row(k_norm_g[l, 0]), False)
        vcT = vc.transpose(0, 2, 1)

        gq = (q_norm_g[l] * (HEAD_DIM ** -0.5 * LOG2E)).reshape(HEAD_DIM, 1)
        kx, kw = tiles(kx), tiles(kw)
        peak = lambda g: jnp.max(jnp.abs(g))
        bound = HEAD_DIM ** 0.5 * LOG2E * peak(q_norm_g[l]) * peak(k_norm_g[l])

        kc_x = jnp.concatenate([kc, jnp.ones_like(kc[..., :1]), jnp.zeros_like(kc[..., 1:])], axis=-1).astype(BF16)
        vcT_x = jnp.concatenate([vcT, jnp.ones_like(vcT[:, :1]),
                                 jnp.zeros_like(vcT[:, :BF16_SUBLANES - 1])], axis=1).astype(BF16)
        ops = (qT, glT, kc_x, vcT_x, ovT, kx, kw, vT, masks, cmask, gq)
        oT = lax.cond(2.0 * bound <= MAX_SCORE_SPAN,
                      lambda ops, bound: _attention(*ops, (-bound).reshape(1), True),
                      lambda ops, bound: _attention(*ops, jnp.zeros((1,), F32), False),
                      ops, bound)

        mk, mv = _mem_kv(mem, row(mem_kv_norm_g[l]), w_mkv[l].astype(BF16), row(mem_k_norm_g[l]))
        x2 = _tail(a_n, oT, x2, mix_out_g[l, 1].reshape(NSA_WIDTH, 1), w_out[l].astype(BF16),
                   row(norm_mem_g[l]), w_mq[l].astype(BF16), row(mem_q_norm_g[l]), mk, mv, w_mo[l].astype(BF16),
                   row(norm_ffn_g[l]), w_ff1[l].astype(BF16), w_ff2[l].astype(BF16), seq)
    return x2.reshape(batch, seq, D_MODEL)
```

```python
import functools

import jax
import jax.numpy as jnp
from jax import lax
from jax.experimental import pallas as pl
from jax.experimental.pallas import tpu as pltpu

F32 = jnp.float32
BF16 = jnp.bfloat16

D_MODEL = 1024
HEAD_DIM = 64
SG_WIDTH = 512
SG_GROUPS = 8
SG_CHUNK = 128
NSA_WIDTH = 512
NSA_KV_HEADS = 2
NSA_GROUP = 4
KV_WIDTH = NSA_KV_HEADS * HEAD_DIM
N_BRANCH = 3
CMP_BLOCK = 32
CMP_STRIDE = 16
CMP_HIDDEN = 256
SLC_BLOCK = 64
SLC_TOPK = 16
WINDOW = 512
FORCE_SCORE = 1e4
MEM_HEADS = 4
MEM_HEAD_DIM = 128
D_FF = 4 * D_MODEL
EPS = 1e-6

COL_Q = 2 * SG_WIDTH
COL_KV = COL_Q + NSA_WIDTH
COL_GATE = COL_KV + 6 * KV_WIDTH
GATES_PER_HEAD = NSA_GROUP * N_BRANCH
GATE_ROWS = 16

MASKED = -1e30
LOG2E = 1.4426950408889634
MAX_SCORE_SPAN = 100.0
BF16_SUBLANES = 16
V_ROWS = HEAD_DIM + BF16_SUBLANES
ATTN_TILE = 256
VMEM_LIMIT = 48 * 1024 * 1024
TAIL_VMEM_LIMIT = 56 * 1024 * 1024

NT_DIMS = (((1,), (1,)), ((), ()))
TN_DIMS = (((0,), (0,)), ((), ()))


def _cparams(*sem):
    return pltpu.CompilerParams(dimension_semantics=sem, vmem_limit_bytes=VMEM_LIMIT)


def _rms(x, g):
    return x * lax.rsqrt(jnp.mean(x * x, axis=-1, keepdims=True) + EPS) * g


def _const_spec(shape):
    return pl.BlockSpec(shape, lambda *_: (0,) * len(shape))


def _dot_3pass(a, b):
    a_hi = a.astype(BF16)
    b_hi = b.astype(BF16)
    a_lo = (a - a_hi.astype(F32)).astype(BF16)
    b_lo = (b - b_hi.astype(F32)).astype(BF16)
    dot = functools.partial(jnp.dot, preferred_element_type=F32)
    return dot(a_hi, b_hi) + (dot(a_hi, b_lo) + dot(a_lo, b_hi))


def _dot_1pass(a, b):
    return jnp.dot(a.astype(BF16), b.astype(BF16), preferred_element_type=F32)


def _front_kernel(x_ref, g_ref, wuv_ref, wk_ref, wT_ref,
                  lng_ref, lnb_ref, ws_ref, bs_ref, og_ref, gs_ref, gw_ref,
                  a_ref, kx_ref, kw_ref, vT_ref, kc_ref, vc_ref, qT_ref, gT_ref, h_sc, cmp_sc, *, tm, seq):
    h_sc[...] = _rms(x_ref[...], g_ref[...]).astype(BF16)
    tq = vT_ref.shape[-1]

    n_vg = wT_ref.shape[0] - NSA_WIDTH

    def queries():
        qT_ref[...] = lax.dot_general(wT_ref[n_vg:, :], h_sc[...], NT_DIMS, preferred_element_type=F32)

    def values():
        vgT = lax.dot_general(wT_ref[:n_vg, :], h_sc[...], NT_DIMS, preferred_element_type=F32)
        gT_ref[...] = vgT[2 * KV_WIDTH:]
        vT = vgT[:2 * KV_WIDTH].astype(BF16)
        tail = jnp.where(lax.broadcasted_iota(jnp.int32, (BF16_SUBLANES, tq), 0) == 0, 1.0, 0.0).astype(BF16)
        for branch in range(2):
            for hd in range(NSA_KV_HEADS):
                r0 = (branch * NSA_KV_HEADS + hd) * HEAD_DIM
                for t in range(tm // tq):
                    vT_ref[hd, branch, t, :HEAD_DIM, :] = vT[r0:r0 + HEAD_DIM, t * tq:(t + 1) * tq]
                    vT_ref[hd, branch, t, HEAD_DIM:, :] = tail

    kk_tile = []

    def keys(half):
        rows = slice(half * (tm // 2), (half + 1) * (tm // 2))
        if half == 0:
            kk_tile.append(jnp.dot(h_sc[...], wk_ref[...], preferred_element_type=F32))
        kk = kk_tile[0][rows]
        t0 = (pl.program_id(0) * tm) % seq + half * (tm // 2)
        blk = (t0 + lax.broadcasted_iota(jnp.int32, (tm // 2, SLC_BLOCK), 0)) // SLC_BLOCK
        onehot = jnp.where(blk == lax.broadcasted_iota(jnp.int32, (tm // 2, SLC_BLOCK), 1), 1.0, 0.0).astype(BF16)
        n_half = (tm // 2) // CMP_STRIDE
        chunk_rows = slice(half * n_half, (half + 1) * n_half)
        for out_ref, idx in ((kc_ref, 0), (vc_ref, 1)):
            cmp_sc[idx, rows, :] = kk[:, idx * KV_WIDTH:(idx + 1) * KV_WIDTH]
            for p in range(CMP_STRIDE):
                at_p = cmp_sc[idx, pl.ds(half * (tm // 2) + p, n_half, stride=CMP_STRIDE), :]
                for hd in range(NSA_KV_HEADS):
                    out_ref[hd, chunk_rows, p * HEAD_DIM:(p + 1) * HEAD_DIM] = at_p[:, hd * HEAD_DIM:(hd + 1) * HEAD_DIM]
        for hd in range(NSA_KV_HEADS):
            def part(idx):
                c0 = idx * KV_WIDTH + hd * HEAD_DIM
                return kk[:, c0:c0 + HEAD_DIM]
            kx_ref[hd, rows, :HEAD_DIM] = _rms(part(2), gs_ref[...]).astype(BF16)
            kx_ref[hd, rows, HEAD_DIM:] = onehot
            kw_ref[hd, rows, :HEAD_DIM] = _rms(part(3), gw_ref[...]).astype(BF16)
            kw_ref[hd, rows, HEAD_DIM:] = onehot

    row = lax.broadcasted_iota(jnp.int32, (SG_CHUNK, SG_CHUNK), 0)
    col = lax.broadcasted_iota(jnp.int32, (SG_CHUNK, SG_CHUNK), 1)
    causal = col <= row
    w = [jnp.where(causal, ws_ref[g], 0.0).astype(BF16) for g in range(SG_GROUPS)]
    between = [lambda: keys(0), lambda: keys(1), values, queries]
    for c in range(tm // SG_CHUNK):
        rows = slice(c * SG_CHUNK, (c + 1) * SG_CHUNK)
        if c % 2 == 0:
            uv_pair = jnp.dot(h_sc[c * SG_CHUNK:(c + 2) * SG_CHUNK, :], wuv_ref[...], preferred_element_type=F32)
        uv = uv_pair[(c % 2) * SG_CHUNK:(c % 2 + 1) * SG_CHUNK]
        between[c]()
        u = jax.nn.gelu(uv[:, :SG_WIDTH])
        v = jax.nn.gelu(uv[:, SG_WIDTH:])
        mu = jnp.mean(v, axis=-1, keepdims=True)
        vc = v - mu
        vn = vc * lax.rsqrt(jnp.mean(vc * vc, axis=-1, keepdims=True) + EPS) * lng_ref[...] + lnb_ref[...]
        vb = vn.astype(BF16)
        s = jnp.concatenate(
            [jnp.dot(w[g], vb[:, g * HEAD_DIM:(g + 1) * HEAD_DIM], preferred_element_type=F32)
             for g in range(SG_GROUPS)], axis=-1) + bs_ref[...]
        a_ref[rows, :] = _rms(u * s, og_ref[...]).astype(BF16)


def _front(x2, g, w_uv, w_k, w_T, ln_g, ln_b, w_s, bs_full, out_g, g_slc, g_win, batch, seq):
    n = x2.shape[0]
    tq = ATTN_TILE
    tm = 2 * tq
    per_b = seq // tm
    hd = NSA_KV_HEADS
    g_rows = w_T.shape[0] - 2 * KV_WIDTH - NSA_WIDTH

    def per_head(width):
        return pl.BlockSpec((None, hd, tm, width), lambda i: (i // per_b, 0, i % per_b, 0))

    def per_head_shape(width, dtype):
        return jax.ShapeDtypeStruct((batch, hd, seq, width), dtype)

    def cols(height):
        return pl.BlockSpec((None, height, tm), lambda i: (i // per_b, 0, i % per_b))

    chunked_shape = jax.ShapeDtypeStruct((batch, hd, seq // CMP_STRIDE, CMP_STRIDE * HEAD_DIM), F32)
    chunked = pl.BlockSpec((None, hd, tm // CMP_STRIDE, CMP_STRIDE * HEAD_DIM),
                           lambda i: (i // per_b, 0, i % per_b, 0))

    return pl.pallas_call(
        functools.partial(_front_kernel, tm=tm, seq=seq),
        out_shape=(jax.ShapeDtypeStruct((n, SG_WIDTH), BF16),
                   per_head_shape(2 * HEAD_DIM, BF16), per_head_shape(2 * HEAD_DIM, BF16),
                   jax.ShapeDtypeStruct((batch, hd, 2, seq // tq, V_ROWS, tq), BF16),
                   chunked_shape, chunked_shape,
                   jax.ShapeDtypeStruct((batch, NSA_WIDTH, seq), F32),
                   jax.ShapeDtypeStruct((batch, g_rows, seq), F32)),
        grid=(n // tm,),
        in_specs=[pl.BlockSpec((tm, D_MODEL), lambda i: (i, 0)), _const_spec((1, D_MODEL)),
                  _const_spec((D_MODEL, 2 * SG_WIDTH)), _const_spec((D_MODEL, 4 * KV_WIDTH)),
                  _const_spec(w_T.shape),
                  _const_spec((1, SG_WIDTH)), _const_spec((1, SG_WIDTH)),
                  _const_spec((SG_GROUPS, SG_CHUNK, SG_CHUNK)),
                  _const_spec((SG_CHUNK, SG_WIDTH)), _const_spec((1, SG_WIDTH)),
                  _const_spec((1, HEAD_DIM)), _const_spec((1, HEAD_DIM))],
        out_specs=(pl.BlockSpec((tm, SG_WIDTH), lambda i: (i, 0)),
                   per_head(2 * HEAD_DIM), per_head(2 * HEAD_DIM),
                   pl.BlockSpec((None, hd, 2, tm // tq, V_ROWS, tq), lambda i: (i // per_b, 0, 0, i % per_b, 0, 0)),
                   chunked, chunked, cols(NSA_WIDTH), cols(g_rows)),
        scratch_shapes=[pltpu.VMEM((tm, D_MODEL), BF16), pltpu.VMEM((2, tm, KV_WIDTH), F32)],
        compiler_params=_cparams("parallel"),
        name="front",
    )(x2, g, w_uv, w_k, w_T, ln_g, ln_b, w_s, bs_full, out_g, g_slc, g_win)


def _cmp_kernel(ck_ref, cv_ref, pos_ref, w1_ref, b1_ref, w2_ref, b2_ref, g_ref, ok_ref, ov_ref):
    for idx, (c_ref, o_ref) in enumerate(((ck_ref, ok_ref), (cv_ref, ov_ref))):
        dot = _dot_1pass if idx else _dot_3pass
        c = c_ref[...]
        nb, half = c.shape
        first = dot(c, w1_ref[idx, :half, :])
        second = dot(c, w1_ref[idx, half:, :])
        pos = dot(pos_ref[idx], w1_ref[idx])[0:1, :]
        h1 = jax.nn.gelu(first + pltpu.roll(second, nb - 1, 0) + pos + b1_ref[idx])
        y = dot(h1, w2_ref[idx]) + b2_ref[idx]
        o_ref[...] = y if idx else _rms(y, g_ref[...])


def _compress(chunks_k, chunks_v, pos8, w1, b1, w2, b2, g):
    bh, nb, half = chunks_k.shape
    stream = pl.BlockSpec((None, nb, half), lambda i: (i, 0, 0))
    out = pl.BlockSpec((None, nb, HEAD_DIM), lambda i: (i, 0, 0))
    return pl.pallas_call(
        _cmp_kernel,
        out_shape=(jax.ShapeDtypeStruct((bh, nb, HEAD_DIM), F32),) * 2,
        grid=(bh,),
        in_specs=[stream, stream,
                  _const_spec(pos8.shape), _const_spec(w1.shape), _const_spec(b1.shape),
                  _const_spec(w2.shape), _const_spec(b2.shape), _const_spec((1, HEAD_DIM))],
        out_specs=(out, out),
        compiler_params=_cparams("parallel"),
        name="compress",
    )(chunks_k, chunks_v, pos8, w1, b1, w2, b2, g)


def _stable_rank(score, score_sc, rank_sc, rows_sc, tq, interleave):
    interleave = list(interleave)
    n_slc = score.shape[0]
    for k in range(n_slc):
        rows_sc[k] = score[k:k + 1, :]
    score_sc[0:n_slc, :] = score
    sub, k_chunk = 8, 16
    per_gap = -(-len(interleave) // (n_slc // k_chunk))
    row_in_group = lax.broadcasted_iota(jnp.int32, (sub, tq), 0)
    for k0 in range(0, n_slc, k_chunk):
        for _ in range(min(per_gap, len(interleave))):
            interleave.pop(0)()
        rks = [jnp.broadcast_to(rows_sc[k], (sub, tq)) for k in range(k0, k0 + k_chunk)]
        for r0 in range(0, n_slc, sub):
            sg = score_sc[r0:r0 + sub, :]
            acc = rank_sc[r0:r0 + sub, :] if k0 else jnp.zeros((sub, tq), F32)
            for k in range(k0, k0 + k_chunk):
                rk = rks[k - k0]
                ge = jnp.where(rk >= sg, 1.0, 0.0)
                gt = jnp.where(rk > sg, 1.0, 0.0)
                if r0 > k:
                    before = ge
                elif r0 + sub - 1 < k:
                    before = gt
                else:
                    before = jnp.where(row_in_group > k - r0, ge, gt)
                acc = acc + before
            rank_sc[r0:r0 + sub, :] = acc
    for rest in interleave:
        rest()


def _attn_kernel(qTa_ref, qTb_ref, glTa_ref, glTb_ref, kc_ref, vcT_ref, ovT_ref, kx_ref, kw_ref, vT_ref, mask_ref,
                 cma_ref, cmb_ref, gq_ref, shift_ref, oa_ref, ob_ref,
                 qw_sc, qx_sc, s_a, mo_a, mn_a, s_b, mo_b, mn_b, s_c, score_sc, rank_sc, rows_sc, mrun_sc, acc_sc,
                 *, tq, nq, bounded):
    i = pl.program_id(2)
    d = HEAD_DIM
    G = NSA_GROUP
    shift = shift_ref[0]
    n_slc = ovT_ref.shape[0]
    ncp = kc_ref.shape[0]
    q_tile = [i, nq - 1 - i]
    q_refs, gl_refs, cm_refs, o_refs = (qTa_ref, qTb_ref), (glTa_ref, glTb_ref), (cma_ref, cmb_ref), (oa_ref, ob_ref)

    def lanes(g):
        return slice(g * tq, (g + 1) * tq)

    if not bounded:
        mrun_sc[...] = jnp.full(mrun_sc.shape, -jnp.inf, F32)
    acc_sc[...] = jnp.zeros(acc_sc.shape, F32)
    gates = [jax.nn.sigmoid(gl_refs[slot][...]) for slot in range(2)]

    def scores(slot, branch, k_ref, k_idx, q_sc, mask, s_buf, mo_buf, mn_buf):
        for g in range(G):
            s = jnp.dot(k_ref[k_idx], q_sc[slot, :, lanes(g)], preferred_element_type=F32)
            if mask is not None:
                s = s + mask_ref[mask]
            s_buf[g] = s
            if not bounded:
                m_old = mrun_sc[slot, branch, g]
                m_new = jnp.maximum(m_old, jnp.max(s, axis=0, keepdims=True))
                mo_buf[g] = m_old
                mn_buf[g] = m_new
                mrun_sc[slot, branch, g] = m_new

    def accumulate(slot, branch, v_idx, s_buf, mo_buf, mn_buf):
        for g in range(G):
            if bounded:
                p = jnp.exp2(s_buf[g]).astype(BF16)
                acc_sc[slot, branch, g] += jnp.dot(vT_ref[branch, v_idx], p, preferred_element_type=F32)
            else:
                m_new = mn_buf[g]
                alpha = jnp.exp2(mo_buf[g] - m_new)
                p = jnp.exp2(s_buf[g] - m_new).astype(BF16)
                acc_sc[slot, branch, g] = alpha * acc_sc[slot, branch, g] + jnp.dot(
                    vT_ref[branch, v_idx], p, preferred_element_type=F32)

    buf_a = (s_a, mo_a, mn_a)
    buf_b = (s_b, mo_b, mn_b)

    def window_stages(slot):
        t = q_tile[slot]
        tiles = [jnp.maximum(t - 2, 0), jnp.maximum(t - 1, 0), t]
        w_masks = [jnp.where(t >= 2, 2, 3), jnp.where(t >= 1, 0, 3), 1]

        def head():
            scores(slot, 1, kw_ref, tiles[0], qw_sc, w_masks[0], *buf_a)
            scores(slot, 1, kw_ref, tiles[1], qw_sc, w_masks[1], *buf_b)

        def mid():
            accumulate(slot, 1, tiles[0], *buf_a)
            scores(slot, 1, kw_ref, tiles[2], qw_sc, w_masks[2], *buf_a)

        return [head, mid, lambda: accumulate(slot, 1, tiles[1], *buf_b),
                lambda: accumulate(slot, 1, tiles[2], *buf_a)]

    def queries(slot):
        for g in range(G):
            qg = q_refs[slot][g * d:(g + 1) * d, :]
            qg = qg * lax.rsqrt(jnp.mean(qg * qg, axis=0, keepdims=True) + EPS) * gq_ref[...]
            qw_sc[slot, 0:d, lanes(g)] = qg.astype(BF16)
        qw_sc[slot, d:2 * d, :] = jnp.full((d, G * tq), shift, F32).astype(BF16)

    def select(slot, before, inside):
        t = q_tile[slot]
        n_blk = n_slc if slot else n_slc // 2
        n_row = ncp if slot else ncp // 2
        for stage in before:
            stage()
        t_row = t * tq + lax.broadcasted_iota(jnp.int32, (1, tq), 1)
        sees_any = jnp.where(t_row >= CMP_BLOCK - 1, 1.0, 0.0)
        c_max = []
        for g in range(G):
            s = (jnp.dot(kc_ref[0:n_row, :], qw_sc[slot, :, lanes(g)], preferred_element_type=F32)
                 + cm_refs[slot][0:n_row, :])
            s_c[g, 0:n_row, :] = s
            if not bounded:
                c_max.append(jnp.max(s, axis=0, keepdims=True))
        oc = []
        for g in range(G):
            e = jnp.exp2(s_c[g, 0:n_row, :]) if bounded else jnp.exp2(s_c[g, 0:n_row, :] - c_max[g])
            s_c[g, 0:n_row, :] = e
            oc.append(jnp.dot(vcT_ref[:, 0:n_row], e.astype(BF16), preferred_element_type=F32))
        psum = jnp.zeros((n_row, tq), F32)
        for g in range(G):
            den = oc[g][d:d + 1, :]
            r = sees_any / jnp.where(den > 0, den, 1.0)
            psum = psum + s_c[g, 0:n_row, :] * r
            o_refs[slot][g * d:(g + 1) * d, :] = (gates[slot][g * 3:g * 3 + 1, :] * r) * oc[g][0:d, :]
        p_hi = psum.astype(BF16)
        p_lo = (psum - p_hi.astype(F32)).astype(BF16)
        imp = (jnp.dot(ovT_ref[0:n_blk, 0:n_row], p_hi, preferred_element_type=F32)
               + jnp.dot(ovT_ref[0:n_blk, 0:n_row], p_lo, preferred_element_type=F32))
        j_idx = lax.broadcasted_iota(jnp.int32, (n_blk, tq), 0)
        t_blk = (t * tq + lax.broadcasted_iota(jnp.int32, (n_blk, tq), 1)) // SLC_BLOCK
        forced = (j_idx == 0) | (j_idx == t_blk) | (j_idx == t_blk - 1)
        score = jnp.where(forced, FORCE_SCORE, jnp.where(j_idx <= t_blk, imp, -FORCE_SCORE))
        _stable_rank(score, score_sc, rank_sc, rows_sc, tq, inside)
        bias = jnp.where(rank_sc[0:n_blk, :] < min(SLC_TOPK, n_slc), shift, MASKED).astype(BF16)
        if n_blk < d:
            bias = jnp.concatenate([bias, jnp.full((d - n_blk, tq), MASKED, BF16)], axis=0)
        qx_sc[slot, 0:d, :] = qw_sc[slot, 0:d, :]
        for g in range(G):
            qx_sc[slot, d:2 * d, lanes(g)] = bias

    n_static = nq // 2

    def job(k):
        if k < n_static:
            return 1, k, None
        if k == nq:
            return 0, i, 1
        in_b = k <= q_tile[1]
        return (jnp.where(in_b, 1, 0), jnp.where(in_b, k, k - q_tile[1] - 1),
                jnp.where(k == q_tile[1], 1, 0))

    bufs = [buf_a, buf_b]

    def sweep_stage(k):
        def stage():
            if k + 1 <= nq:
                slot, tile, mask = job(k + 1)
                scores(slot, 0, kx_ref, tile, qx_sc, mask, *bufs[(k + 1) % 2])
            if k >= 0:
                slot, tile, _ = job(k)
                accumulate(slot, 0, tile, *bufs[k % 2])
        return stage

    queries(0)
    queries(1)
    win0, win1 = window_stages(0), window_stages(1)
    select(1, win1[:1], win1[1:] + win0[:1])
    static_sweep = [sweep_stage(k) for k in range(-1, n_static - 1)]
    select(0, win0[1:3], win0[3:] + static_sweep)
    for k in range(n_static - 1, nq + 1):
        sweep_stage(k)()

    for slot in range(2):
        for g in range(G):
            o = o_refs[slot][g * d:(g + 1) * d, :]
            for branch in range(2):
                acc = acc_sc[slot, branch, g]
                gate = gates[slot][g * 3 + branch + 1:g * 3 + branch + 2, :]
                o = o + (gate * (1.0 / acc[d:d + 1, :])) * acc[0:d, :]
            o_refs[slot][g * d:(g + 1) * d, :] = o


def _attention(qT, glT, kc_x, vcT_x, ovT, kx, kw, vT, masks, cmask, gq, shift, bounded):
    batch, _, seq = qT.shape
    tq = ATTN_TILE
    nq = seq // tq
    half = nq // 2
    hd = NSA_KV_HEADS
    gw = NSA_GROUP * HEAD_DIM
    ncp = kc_x.shape[1]
    n_slc = ovT.shape[0]

    def per_head(shape):
        return pl.BlockSpec((None, None) + shape, lambda b, h, i: (b, h) + (0,) * len(shape))

    def other(i):
        return nq - 1 - i

    stage = [pltpu.VMEM((NSA_GROUP, tq, tq), F32),
             pltpu.VMEM((NSA_GROUP, 1, tq), F32),
             pltpu.VMEM((NSA_GROUP, 1, tq), F32)]
    out_half = jax.ShapeDtypeStruct((batch, NSA_WIDTH, seq // 2), F32)
    return pl.pallas_call(
        functools.partial(_attn_kernel, tq=tq, nq=nq, bounded=bounded),
        out_shape=(out_half, out_half),
        grid=(batch, hd, half),
        in_specs=[pl.BlockSpec((None, gw, tq), lambda b, h, i: (b, h, i)),
                  pl.BlockSpec((None, gw, tq), lambda b, h, i: (b, h, other(i))),
                  pl.BlockSpec((None, GATE_ROWS, tq), lambda b, h, i: (b, h, i)),
                  pl.BlockSpec((None, GATE_ROWS, tq), lambda b, h, i: (b, h, other(i))),
                  pl.BlockSpec((None, ncp, 2 * HEAD_DIM), lambda b, h, i: (b * hd + h, 0, 0)),
                  pl.BlockSpec((None, V_ROWS, ncp), lambda b, h, i: (b * hd + h, 0, 0)),
                  _const_spec((n_slc, ncp)),
                  per_head((nq, tq, 2 * HEAD_DIM)), per_head((nq, tq, 2 * HEAD_DIM)),
                  per_head((2, nq, V_ROWS, tq)),
                  _const_spec((4, tq, tq)),
                  pl.BlockSpec((None, ncp, tq), lambda b, h, i: (i, 0, 0)),
                  pl.BlockSpec((None, ncp, tq), lambda b, h, i: (other(i), 0, 0)),
                  _const_spec((HEAD_DIM, 1)),
                  pl.BlockSpec(memory_space=pltpu.SMEM)],
        out_specs=(pl.BlockSpec((None, gw, tq), lambda b, h, i: (b, h, i)),
                   pl.BlockSpec((None, gw, tq), lambda b, h, i: (b, h, other(i) - half))),
        scratch_shapes=[pltpu.VMEM((2, 2 * HEAD_DIM, NSA_GROUP * tq), BF16),
                        pltpu.VMEM((2, 2 * HEAD_DIM, NSA_GROUP * tq), BF16)] + stage + stage + [
            pltpu.VMEM((NSA_GROUP, ncp, tq), F32),
            pltpu.VMEM((n_slc, tq), F32), pltpu.VMEM((n_slc, tq), F32), pltpu.VMEM((n_slc, 1, tq), F32),
            pltpu.VMEM((2, 2, NSA_GROUP, 1, tq), F32),
            pltpu.VMEM((2, 2, NSA_GROUP, V_ROWS, tq), F32)],
        compiler_params=_cparams("parallel", "parallel", "arbitrary"),
        name="nsa_attention_bounded" if bounded else "nsa_attention",
    )(qT, qT, glT, glT, kc_x, vcT_x, ovT, kx, kw, vT, masks, cmask, cmask, gq, shift)


def _mem_kv_kernel(mem_ref, g_ref, w_ref, kg_ref, k_ref, v_ref):
    mn = _rms(mem_ref[...], g_ref[...]).astype(BF16)
    kv = jnp.dot(mn, w_ref[...], preferred_element_type=F32)
    width = MEM_HEADS * MEM_HEAD_DIM
    for hh in range(MEM_HEADS):
        cols = slice(hh * MEM_HEAD_DIM, (hh + 1) * MEM_HEAD_DIM)
        k_ref[:, cols] = _rms(kv[:, cols], kg_ref[...]).astype(BF16)
    v_ref[...] = kv[:, width:].astype(BF16)


def _mem_kv(mem, g, w_mkv, k_g):
    batch, m_len, _ = mem.shape
    width = MEM_HEADS * MEM_HEAD_DIM
    spec = pl.BlockSpec((None, m_len, width), lambda b: (b, 0, 0))
    return pl.pallas_call(
        _mem_kv_kernel,
        out_shape=(jax.ShapeDtypeStruct((batch, m_len, width), BF16),) * 2,
        grid=(batch,),
        in_specs=[pl.BlockSpec((None, m_len, D_MODEL), lambda b: (b, 0, 0)),
                  _const_spec((1, D_MODEL)), _const_spec((D_MODEL, 2 * width)),
                  _const_spec((1, MEM_HEAD_DIM))],
        out_specs=(spec, spec),
        compiler_params=_cparams("parallel"),
        name="mem_kv",
    )(mem, g, w_mkv, k_g)


def _tail_kernel(a_ref, bTa_ref, bTb_ref, x_ref, gb_ref, wout_ref, gm_ref, wq_ref, qg_ref, k_ref, v_ref, wo_ref,
                 gf_ref, w1_ref, w2_ref, o_ref, *, per_b):
    in_first_half = (pl.program_id(0) % per_b) < per_b // 2
    bT = jnp.where(in_first_half, bTa_ref[...], bTb_ref[...])
    bn = (bT * lax.rsqrt(jnp.mean(bT * bT, axis=0, keepdims=True) + EPS) * gb_ref[...]).astype(BF16)
    x = x_ref[...] + jnp.dot(a_ref[...], wout_ref[:SG_WIDTH, :], preferred_element_type=F32)
    x = x + lax.dot_general(bn, wout_ref[SG_WIDTH:, :], TN_DIMS, preferred_element_type=F32)

    h = _rms(x, gm_ref[...]).astype(BF16)
    q = jnp.dot(h, wq_ref[...], preferred_element_type=F32)
    outs = []
    for hh in range(MEM_HEADS):
        cols = slice(hh * MEM_HEAD_DIM, (hh + 1) * MEM_HEAD_DIM)
        qh = (_rms(q[:, cols], qg_ref[...]) * MEM_HEAD_DIM ** -0.5).astype(BF16)
        s = lax.dot_general(qh, k_ref[:, cols], NT_DIMS, preferred_element_type=F32)
        e = jnp.exp(s - jnp.max(s, axis=-1, keepdims=True))
        p = e * (1.0 / jnp.sum(e, axis=-1, keepdims=True))
        outs.append(jnp.dot(p.astype(BF16), v_ref[:, cols], preferred_element_type=F32))
    o = jnp.concatenate(outs, axis=-1).astype(BF16)
    x = x + jnp.dot(o, wo_ref[...], preferred_element_type=F32)

    h = _rms(x, gf_ref[...]).astype(BF16)
    step = 1024
    for c0 in range(0, D_FF, step):
        a = jnp.maximum(jnp.dot(h, w1_ref[:, c0:c0 + step], preferred_element_type=F32), 0.0)
        x = x + jnp.dot((a * a).astype(BF16), w2_ref[c0:c0 + step, :], preferred_element_type=F32)
    o_ref[...] = x


def _tail(a_n, bT_halves, x2, g_b, w_out, g_mem, w_mq, q_g, mk, mv, w_mo, g_ffn, w1, w2, seq):
    n = x2.shape[0]
    tm = 512
    per_b = seq // tm
    half = per_b // 2
    m_len = mk.shape[1]
    width = MEM_HEADS * MEM_HEAD_DIM

    def weight(shape):
        return pl.BlockSpec(shape, lambda i: (0,) * len(shape), pipeline_mode=pl.Buffered(1))

    kv_spec = pl.BlockSpec((None, m_len, width), lambda i: (i // per_b, 0, 0))
    return pl.pallas_call(
        functools.partial(_tail_kernel, per_b=per_b),
        out_shape=jax.ShapeDtypeStruct((n, D_MODEL), F32),
        grid=(n // tm,),
        in_specs=[pl.BlockSpec((tm, SG_WIDTH), lambda i: (i, 0)),
                  pl.BlockSpec((None, NSA_WIDTH, tm), lambda i: (i // per_b, 0, jnp.minimum(i % per_b, half - 1))),
                  pl.BlockSpec((None, NSA_WIDTH, tm), lambda i: (i // per_b, 0, jnp.maximum(i % per_b - half, 0))),
                  pl.BlockSpec((tm, D_MODEL), lambda i: (i, 0)),
                  _const_spec((NSA_WIDTH, 1)), weight((D_MODEL, D_MODEL)),
                  _const_spec((1, D_MODEL)), weight((D_MODEL, width)), _const_spec((1, MEM_HEAD_DIM)),
                  kv_spec, kv_spec, weight((width, D_MODEL)),
                  _const_spec((1, D_MODEL)), weight((D_MODEL, D_FF)), weight((D_FF, D_MODEL))],
        out_specs=pl.BlockSpec((tm, D_MODEL), lambda i: (i, 0)),
        compiler_params=pltpu.CompilerParams(dimension_semantics=("parallel",), vmem_limit_bytes=TAIL_VMEM_LIMIT),
        name="tail",
    )(a_n, *bT_halves, x2, g_b, w_out, g_mem, w_mq, q_g, mk, mv, w_mo, g_ffn, w1, w2)


def _overlap_T(n_cmp_pad, n_cmp, n_slc):
    cs = jnp.arange(n_cmp_pad) * CMP_STRIDE
    ss = jnp.arange(n_slc) * SLC_BLOCK
    ov = jnp.clip(jnp.minimum(cs[None, :] + CMP_BLOCK, ss[:, None] + SLC_BLOCK)
                  - jnp.maximum(cs[None, :], ss[:, None]), 0, None).astype(F32) / CMP_BLOCK
    ov = jnp.where(jnp.arange(n_cmp_pad)[None, :] < n_cmp, ov, 0.0)
    return ov.astype(BF16)


def kernel(x, mem, norm_mix_g, w_in, sg_ln_g, sg_ln_b, sg_w, sg_b, q_norm_g, k_norm_g, cmp_pos, cmp_w1, cmp_b1, cmp_w2, cmp_b2, mix_out_g, w_out, norm_mem_g, mem_kv_norm_g, w_mq, w_mkv, mem_q_norm_g, mem_k_norm_g, w_mo, norm_ffn_g, w_ff1, w_ff2):
    batch, seq, _ = x.shape
    depth = w_in.shape[0]
    n = batch * seq
    tq = ATTN_TILE
    assert WINDOW == 2 * tq and seq % 512 == 0
    n_chunk = seq // CMP_STRIDE
    n_cmp = n_chunk - 1
    n_slc = seq // SLC_BLOCK
    assert n_slc <= HEAD_DIM
    ovT = _overlap_T(n_chunk, n_cmp, n_slc)
    row = lambda a: a.reshape(1, -1)
    r_idx = lax.broadcasted_iota(jnp.int32, (tq, tq), 0)
    c_idx = lax.broadcasted_iota(jnp.int32, (tq, tq), 1)
    masks = jnp.stack([jnp.zeros((tq, tq), F32), jnp.where(r_idx <= c_idx, 0.0, MASKED),
                       jnp.where(r_idx > c_idx, 0.0, MASKED), jnp.full((tq, tq), MASKED, F32)])
    blk_end = jnp.arange(n_chunk) * CMP_STRIDE + (CMP_BLOCK - 1)
    cmask = jnp.where((blk_end[:, None] <= jnp.arange(seq)[None, :]) & (jnp.arange(n_chunk)[:, None] < n_cmp),
                      0.0, MASKED).astype(F32).reshape(n_chunk, seq // tq, tq).transpose(1, 0, 2)

    x2 = x.reshape(n, D_MODEL)
    for l in range(depth):
        w = w_in[l].astype(BF16)
        w_gT = jnp.pad(w[:, COL_GATE:].T.reshape(NSA_KV_HEADS, GATES_PER_HEAD, D_MODEL),
                       ((0, 0), (0, GATE_ROWS - GATES_PER_HEAD), (0, 0))).reshape(NSA_KV_HEADS * GATE_ROWS, D_MODEL)
        stream = lambda s: w[:, COL_KV + s * KV_WIDTH:COL_KV + (s + 1) * KV_WIDTH]
        w_k = jnp.concatenate([stream(0), stream(1), stream(2), stream(4)], axis=1)
        w_T = jnp.concatenate([stream(3).T, stream(5).T, w_gT, w[:, COL_Q:COL_KV].T], axis=0)
        bs_full = jnp.repeat(sg_b[l].T, HEAD_DIM, axis=1)
        a_n, kx, kw, vT, k_cmp, v_cmp, qT, glT = _front(
            x2, row(norm_mix_g[l]), w[:, :COL_Q], w_k, w_T,
            row(sg_ln_g[l]), row(sg_ln_b[l]), sg_w[l], bs_full, row(mix_out_g[l, 0]),
            row(k_norm_g[l, 1]), row(k_norm_g[l, 2]), batch, seq)
        tiles = lambda a: a.reshape(batch, NSA_KV_HEADS, seq // tq, tq, a.shape[-1])
        chunks = lambda a: a.reshape(batch * NSA_KV_HEADS, n_chunk, CMP_STRIDE * HEAD_DIM)
        pos8 = jnp.broadcast_to(cmp_pos[l].reshape(2, 1, -1), (2, 8, CMP_BLOCK * HEAD_DIM))
        kc, vc = _compress(chunks(k_cmp), chunks(v_cmp), pos8, cmp_w1[l], cmp_b1[l].reshape(2, 1, -1), cmp_w2[l],
                           cmp_b2[l].reshape(2, 1, -1), row(k_norm_g[l, 0]))
        vcT = vc.transpose(0, 2, 1)

        gq = (q_norm_g[l] * (HEAD_DIM ** -0.5 * LOG2E)).reshape(HEAD_DIM, 1)
        kx, kw = tiles(kx), tiles(kw)
        peak = lambda g: jnp.max(jnp.abs(g))
        bound = HEAD_DIM ** 0.5 * LOG2E * peak(q_norm_g[l]) * peak(k_norm_g[l])

        kc_x = jnp.concatenate([kc, jnp.ones_like(kc[..., :1]), jnp.zeros_like(kc[..., 1:])], axis=-1).astype(BF16)
        vcT_x = jnp.concatenate([vcT, jnp.ones_like(vcT[:, :1]),
                                 jnp.zeros_like(vcT[:, :BF16_SUBLANES - 1])], axis=1).astype(BF16)
        ops = (qT, glT, kc_x, vcT_x, ovT, kx, kw, vT, masks, cmask, gq)
        oT = lax.cond(2.0 * bound <= MAX_SCORE_SPAN,
                      lambda ops, bound: _attention(*ops, (-bound).reshape(1), True),
                      lambda ops, bound: _attention(*ops, jnp.zeros((1,), F32), False),
                      ops, bound)

        mk, mv = _mem_kv(mem, row(mem_kv_norm_g[l]), w_mkv[l].astype(BF16), row(mem_k_norm_g[l]))
        x2 = _tail(a_n, oT, x2, mix_out_g[l, 1].reshape(NSA_WIDTH, 1), w_out[l].astype(BF16),
                   row(norm_mem_g[l]), w_mq[l].astype(BF16), row(mem_q_norm_g[l]), mk, mv, w_mo[l].astype(BF16),
                   row(norm_ffn_g[l]), w_ff1[l].astype(BF16), w_ff2[l].astype(BF16), seq)
    return x2.reshape(batch, seq, D_MODEL)
```
